```python
import jax, jax.numpy as jnp
from jax import lax
import numpy as np

D_MODEL = 1024
BATCH = 4
SEQ = 8192
DEPTH = 4

N_MIXERS = 4
N_META = 16
Q_BLOCK = 128
EPS = 1e-6
POOL_WINDOWS = (2, 4, 8, 16)
N_POOL_GROUPS = len(POOL_WINDOWS)
POOL_GROUP = D_MODEL // N_POOL_GROUPS
N_HEADS = 16
HEAD_DIM = D_MODEL // N_HEADS
MLA_HEADS = 16
MLA_Q_RANK = 384
MLA_KV_RANK = 256
MLA_NOPE = 64
MLA_ROPE = 32
MLA_V = 64
ROPE_THETA = 10000.0
D_FF = ((-(-8 * D_MODEL // 3) + 255) // 256) * 256

kernel_name = "hybrid_pool_sb_mla_fox_trunk"


def _n_layers_of(m):
    return len(range(m, DEPTH, N_MIXERS))


def rmsnorm(x, g):
    xf = x.astype(jnp.float32)
    y = xf * lax.rsqrt(jnp.mean(xf * xf, axis=-1, keepdims=True) + EPS)
    return (y * g.astype(jnp.float32)).astype(x.dtype)


def swiglu(h, w_gate, w_up, w_down):
    return (jax.nn.silu(h @ w_gate) * (h @ w_up)) @ w_down


def sweep_queries(attend, q_parts, kv_parts):
    L = q_parts[0].shape[1]
    pos = jnp.arange(L)
    meta_out = attend(tuple(a[:, :N_META] for a in q_parts), pos[:N_META],
                      tuple(a[:, :N_META] for a in kv_parts), pos[:N_META])
    n_blk = (L - N_META) // Q_BLOCK

    def body(i):
        start = N_META + i * Q_BLOCK
        qs = tuple(lax.dynamic_slice_in_dim(a, start, Q_BLOCK, axis=1) for a in q_parts)
        return attend(qs, start + jnp.arange(Q_BLOCK), kv_parts, pos)

    out = lax.map(body, jnp.arange(n_blk))
    B = out.shape[1]
    out = jnp.moveaxis(out, 0, 1).reshape((B, n_blk * Q_BLOCK) + out.shape[3:])
    return jnp.concatenate([meta_out, out], axis=1)


def softmax_block(q, k, v, qpos, kpos, scale, q_decay=None, k_decay=None):
    s = jnp.einsum('bqhd,bkhd->bhqk', q, k).astype(jnp.float32) * scale
    if q_decay is not None:
        s = s + (jnp.transpose(q_decay, (0, 2, 1))[:, :, :, None]
                 - jnp.transpose(k_decay, (0, 2, 1))[:, :, None, :]).astype(jnp.float32)
    mask = kpos[None, :] <= qpos[:, None]
    s = jnp.where(mask, s, jnp.finfo(jnp.float32).min)
    p = jax.nn.softmax(s, axis=-1)
    return jnp.einsum('bhqk,bkhd->bqhd', p.astype(v.dtype), v)


def pool_mixer(h, w, scale):
    B, L, _ = h.shape
    hf = h.astype(jnp.float32)
    pos = jnp.arange(L)
    outs = []
    for g, win in enumerate(POOL_WINDOWS):
        xg = hf[..., g * POOL_GROUP:(g + 1) * POOL_GROUP]
        cs = jnp.cumsum(xg, axis=1)
        lag = jnp.pad(cs[:, :-win], ((0, 0), (win, 0), (0, 0)))
        cnt = jnp.minimum(pos + 1, win).astype(jnp.float32)[None, :, None]
        outs.append((cs - lag) / cnt - xg)
    pooled = jnp.stack(outs, axis=2).astype(h.dtype)
    mixed = jnp.einsum('blgc,gcd->blgd', pooled, w).reshape(B, L, D_MODEL)
    return mixed * scale


def _sb_attend(qs, qpos, kvs, kpos):
    (q,) = qs
    k, v = kvs
    z = jnp.einsum('bqhd,bkhd->bhqk', q, k).astype(jnp.float32) * (HEAD_DIM ** -0.5)
    mask = kpos[None, :] < qpos[:, None]
    log_keep = jnp.where(mask, jax.nn.log_sigmoid(-z), 0.0)
    later = lax.cumsum(log_keep, axis=3, reverse=True) - log_keep
    a = jnp.where(mask, jnp.exp(jax.nn.log_sigmoid(z) + later), 0.0)
    return jnp.einsum('bhqk,bkhd->bqhd', a.astype(v.dtype), v)


def sb_mixer(h, w_qkv, w_o):
    B, L, _ = h.shape
    qkv = (h @ w_qkv).reshape(B, L, 3, N_HEADS, HEAD_DIM)
    q, k, v = qkv[:, :, 0], qkv[:, :, 1], qkv[:, :, 2]
    o = sweep_queries(_sb_attend, (q,), (k, v))
    return o.reshape(B, L, N_HEADS * HEAD_DIM) @ w_o


def _rope(x, cos, sin):
    xf = x.astype(jnp.float32)
    half = xf.shape[-1] // 2
    x1, x2 = xf[..., :half], xf[..., half:]
    return jnp.concatenate([x1 * cos - x2 * sin, x2 * cos + x1 * sin], axis=-1).astype(x.dtype)


def _mla_attend(qs, qpos, kvs, kpos):
    (q,) = qs
    k, v = kvs
    return softmax_block(q, k, v, qpos, kpos, (MLA_NOPE + MLA_ROPE) ** -0.5)


def mla_mixer(h, w_down, q_norm, kv_norm, w_uq, w_ukv, w_o):
    B, L, _ = h.shape
    down = h @ w_down
    c_q = rmsnorm(down[..., :MLA_Q_RANK], q_norm)
    c_kv = rmsnorm(down[..., MLA_Q_RANK:MLA_Q_RANK + MLA_KV_RANK], kv_norm)
    k_rope = down[..., MLA_Q_RANK + MLA_KV_RANK:]
    q = (c_q @ w_uq).reshape(B, L, MLA_HEADS, MLA_NOPE + MLA_ROPE)
    kv = (c_kv @ w_ukv).reshape(B, L, MLA_HEADS, MLA_NOPE + MLA_V)
    q_nope, q_rope = q[..., :MLA_NOPE], q[..., MLA_NOPE:]
    k_nope, v = kv[..., :MLA_NOPE], kv[..., MLA_NOPE:]
    inv = ROPE_THETA ** (-jnp.arange(0, MLA_ROPE, 2, dtype=jnp.float32) / MLA_ROPE)
    ang = jnp.arange(L, dtype=jnp.float32)[:, None] * inv[None, :]
    cos, sin = jnp.cos(ang), jnp.sin(ang)
    q_rope = _rope(q_rope, cos[:, None, :], sin[:, None, :])
    k_rope = _rope(k_rope, cos, sin)
    q = jnp.concatenate([q_nope, q_rope], axis=-1)
    k = jnp.concatenate([k_nope, jnp.broadcast_to(k_rope[:, :, None, :], (B, L, MLA_HEADS, MLA_ROPE))], axis=-1)
    o = sweep_queries(_mla_attend, (q,), (k, v))
    return o.reshape(B, L, MLA_HEADS * MLA_V) @ w_o


def _fox_attend(qs, qpos, kvs, kpos):
    q, fq = qs
    k, v, fk = kvs
    return softmax_block(q, k, v, qpos, kpos, HEAD_DIM ** -0.5, fq, fk)


def fox_mixer(h, w_qkvf, b_f, w_o):
    B, L, _ = h.shape
    proj = h @ w_qkvf
    qkv = proj[..., :3 * N_HEADS * HEAD_DIM].reshape(B, L, 3, N_HEADS, HEAD_DIM)
    q, k, v = qkv[:, :, 0], qkv[:, :, 1], qkv[:, :, 2]
    f_logit = proj[..., 3 * N_HEADS * HEAD_DIM:].astype(jnp.float32) + b_f.astype(jnp.float32)
    F = jnp.cumsum(jax.nn.log_sigmoid(f_logit), axis=1)
    o = sweep_queries(_fox_attend, (q, F), (k, v, F))
    return o.reshape(B, L, N_HEADS * HEAD_DIM) @ w_o


def setup_inputs(seed: int = 0) -> dict:
    key = jax.random.key(seed)
    ks = jax.random.split(key, 24)
    f32 = jnp.float32

    def w(k, shape, fan_in):
        return jax.random.normal(k, shape, f32) * (fan_in ** -0.5)

    def gain(k, shape):
        return 1.0 + 0.02 * jax.random.normal(k, shape, f32)

    nA, nB, nC, nD = (_n_layers_of(m) for m in range(N_MIXERS))
    D = D_MODEL
    return {
        "x": jax.random.normal(ks[0], (BATCH, SEQ, D), f32),
        "meta": jax.random.normal(ks[1], (N_META, D), f32),
        "norm_mix": gain(ks[2], (DEPTH, D)),
        "norm_ffn": gain(ks[3], (DEPTH, D)),
        "pool_w": w(ks[4], (nA, N_POOL_GROUPS, POOL_GROUP, POOL_GROUP), POOL_GROUP),
        "pool_scale": gain(ks[5], (nA, D)),
        "sb_w_qkv": w(ks[6], (nB, D, 3 * N_HEADS * HEAD_DIM), D),
        "sb_w_o": w(ks[7], (nB, N_HEADS * HEAD_DIM, D), N_HEADS * HEAD_DIM),
        "mla_w_down": w(ks[8], (nC, D, MLA_Q_RANK + MLA_KV_RANK + MLA_ROPE), D),
        "mla_q_norm": gain(ks[9], (nC, MLA_Q_RANK)),
        "mla_kv_norm": gain(ks[10], (nC, MLA_KV_RANK)),
        "mla_w_uq": w(ks[11], (nC, MLA_Q_RANK, MLA_HEADS * (MLA_NOPE + MLA_ROPE)), MLA_Q_RANK),
        "mla_w_ukv": w(ks[12], (nC, MLA_KV_RANK, MLA_HEADS * (MLA_NOPE + MLA_V)), MLA_KV_RANK),
        "mla_w_o": w(ks[13], (nC, MLA_HEADS * MLA_V, D), MLA_HEADS * MLA_V),
        "fox_w_qkvf": w(ks[14], (nD, D, 3 * N_HEADS * HEAD_DIM + N_HEADS), D),
        "fox_b_f": 2.0 + 0.5 * jax.random.normal(ks[15], (nD, N_HEADS), f32),
        "fox_w_o": w(ks[16], (nD, N_HEADS * HEAD_DIM, D), N_HEADS * HEAD_DIM),
        "ffn_w_gate": w(ks[17], (DEPTH, D, D_FF), D),
        "ffn_w_up": w(ks[18], (DEPTH, D, D_FF), D),
        "ffn_w_down": w(ks[19], (DEPTH, D_FF, D), D_FF),
        "final_norm": gain(ks[20], (D,)),
    }


def reference(x, meta, norm_mix, norm_ffn, pool_w, pool_scale, sb_w_qkv, sb_w_o,
              mla_w_down, mla_q_norm, mla_kv_norm, mla_w_uq, mla_w_ukv, mla_w_o,
              fox_w_qkvf, fox_b_f, fox_w_o, ffn_w_gate, ffn_w_up, ffn_w_down, final_norm):
    B = x.shape[0]
    meta_b = jnp.broadcast_to(meta[None].astype(x.dtype), (B, N_META, D_MODEL))
    h = jnp.concatenate([meta_b, x], axis=1)
    for i in range(DEPTH):
        m, j = i % N_MIXERS, i // N_MIXERS
        a = rmsnorm(h, norm_mix[i])
        if m == 0:
            mix = pool_mixer(a, pool_w[j], pool_scale[j])
        elif m == 1:
            mix = sb_mixer(a, sb_w_qkv[j], sb_w_o[j])
        elif m == 2:
            mix = mla_mixer(a, mla_w_down[j], mla_q_norm[j], mla_kv_norm[j],
                            mla_w_uq[j], mla_w_ukv[j], mla_w_o[j])
        else:
            mix = fox_mixer(a, fox_w_qkvf[j], fox_b_f[j], fox_w_o[j])
        h = h + mix
        h = h + swiglu(rmsnorm(h, norm_ffn[i]), ffn_w_gate[i], ffn_w_up[i], ffn_w_down[i])
    h = rmsnorm(h, final_norm)
    return h[:, N_META:]
```

```python
import functools

import jax
import jax.numpy as jnp
from jax import lax
from jax.experimental import pallas as pl
from jax.experimental.pallas import tpu as pltpu

F32 = jnp.float32
BF16 = jnp.bfloat16

D_MODEL = 1024
N_META = 16
EPS = 1e-6
POOL_WINDOWS = (2, 4, 8, 16)
POOL_GROUP = D_MODEL // len(POOL_WINDOWS)
N_HEADS = 16
HEAD_DIM = 64
MLA_Q_RANK = 384
MLA_KV_RANK = 256
MLA_NOPE = 64
MLA_ROPE = 32
MLA_V = 64
ROPE_THETA = 10000.0

LANES = 128
PAD_FRONT = 256
ROW0 = PAD_FRONT - N_META
HALO = max(POOL_WINDOWS)
MASK_VALUE = -1e30
VMEM_LIMIT = 56 * 1024 * 1024

ROW_TILE = 1024
ATT_TILE = 256
FF_TILE = 256


def _divisor_tile(n, target):
    best = None
    for t in range(8, min(n, target) + 1, 8):
        if n % t == 0:
            best = t
    assert best is not None, (n, target)
    return best


def _params(*sem):
    return pltpu.CompilerParams(dimension_semantics=sem, vmem_limit_bytes=VMEM_LIMIT)


def _rmsnorm(x, g):
    return x * lax.rsqrt(jnp.mean(x * x, axis=-1, keepdims=True) + EPS) * g


def _log_sigmoid_neg(z):
    return -(jnp.maximum(z, 0.0) + jnp.log1p(jnp.exp(-jnp.abs(z))))


def _dot(a, b):
    return jnp.dot(a, b, preferred_element_type=F32)


def _dot_nt(a, b):
    return lax.dot_general(a, b, (((1,), (1,)), ((), ())), preferred_element_type=F32)


def _pool_kernel(h_ref, g_ref, w_ref, sc_ref, o_ref, abuf, *, tile):
    t = pl.program_id(1)
    h = h_ref[0]
    row = t * tile + lax.broadcasted_iota(jnp.int32, (tile, 1), 0)
    a = jnp.where(row >= ROW0, _rmsnorm(h, g_ref[...]), 0.0)

    @pl.when(t == 0)
    def _():
        abuf[0:HALO, :] = jnp.zeros((HALO, D_MODEL), F32)

    @pl.when(t > 0)
    def _():
        abuf[0:HALO, :] = abuf[tile:tile + HALO, :]

    abuf[HALO:HALO + tile, :] = a
    pos1 = row - ROW0 + 1
    outs = []
    for g, win in enumerate(POOL_WINDOWS):
        cols = slice(g * POOL_GROUP, (g + 1) * POOL_GROUP)
        xg = a[:, cols]
        acc = xg
        for j in range(1, win):
            acc = acc + abuf[HALO - j:HALO - j + tile, cols]
        cnt = jnp.clip(pos1, 1, win).astype(F32)
        pooled = acc / cnt - xg
        outs.append(_dot(pooled.astype(BF16), w_ref[g]))
    mixed = jnp.concatenate(outs, axis=1) * sc_ref[...]
    o_ref[0] = h + mixed


def _pool_layer(h, g, w, sc):
    B, Lp, D = h.shape
    tile = ATT_TILE
    kern = functools.partial(_pool_kernel, tile=tile)
    return pl.pallas_call(
        kern,
        grid=(B, Lp // tile),
        in_specs=[
            pl.BlockSpec((1, tile, D), lambda b, t: (b, t, 0)),
            pl.BlockSpec((1, D), lambda b, t: (0, 0)),
            pl.BlockSpec((len(POOL_WINDOWS), POOL_GROUP, POOL_GROUP), lambda b, t: (0, 0, 0)),
            pl.BlockSpec((1, D), lambda b, t: (0, 0)),
        ],
        out_specs=pl.BlockSpec((1, tile, D), lambda b, t: (b, t, 0)),
        out_shape=jax.ShapeDtypeStruct((B, Lp, D), F32),
        scratch_shapes=[pltpu.VMEM((tile + HALO, D), F32)],
        compiler_params=_params("arbitrary", "arbitrary"),
        name="pool_layer",
    )(h, g.reshape(1, D), w.astype(BF16), sc.reshape(1, D))


def _norm_proj_kernel(h_ref, g_ref, w_ref, o_ref, a_scr):
    @pl.when(pl.program_id(1) == 0)
    def _():
        a_scr[...] = _rmsnorm(h_ref[...], g_ref[...]).astype(BF16)

    o_ref[...] = _dot(a_scr[...], w_ref[...]).astype(o_ref.dtype)


def _norm_proj(h2, g, w, tn):
    M, D = h2.shape
    N = w.shape[1]
    tm = _divisor_tile(M, ROW_TILE)
    return pl.pallas_call(
        _norm_proj_kernel,
        grid=(M // tm, N // tn),
        in_specs=[
            pl.BlockSpec((tm, D), lambda i, j: (i, 0)),
            pl.BlockSpec((1, D), lambda i, j: (0, 0)),
            pl.BlockSpec((D, tn), lambda i, j: (0, j)),
        ],
        out_specs=pl.BlockSpec((tm, tn), lambda i, j: (i, j)),
        out_shape=jax.ShapeDtypeStruct((M, N), BF16),
        scratch_shapes=[pltpu.VMEM((tm, D), BF16)],
        compiler_params=_params("arbitrary", "arbitrary"),
        name="norm_proj",
    )(h2, g.reshape(1, D), w)


def _fox_proj_kernel(h_ref, g_ref, w_ref, wf_ref, o_ref, f_ref, a_scr):
    @pl.when(pl.program_id(1) == 0)
    def _():
        a = _rmsnorm(h_ref[...], g_ref[...]).astype(BF16)
        a_scr[...] = a
        f_ref[...] = _dot(a, wf_ref[...])

    o_ref[...] = _dot(a_scr[...], w_ref[...]).astype(o_ref.dtype)


def _fox_proj(h2, g, w, wf, tn):
    M, D = h2.shape
    N = w.shape[1]
    tm = _divisor_tile(M, ROW_TILE)
    return pl.pallas_call(
        _fox_proj_kernel,
        grid=(M // tm, N // tn),
        in_specs=[
            pl.BlockSpec((tm, D), lambda i, j: (i, 0)),
            pl.BlockSpec((1, D), lambda i, j: (0, 0)),
            pl.BlockSpec((D, tn), lambda i, j: (0, j)),
            pl.BlockSpec((D, LANES), lambda i, j: (0, 0)),
        ],
        out_specs=[
            pl.BlockSpec((tm, tn), lambda i, j: (i, j)),
            pl.BlockSpec((tm, LANES), lambda i, j: (i, 0)),
        ],
        out_shape=[
            jax.ShapeDtypeStruct((M, N), BF16),
            jax.ShapeDtypeStruct((M, LANES), F32),
        ],
        scratch_shapes=[pltpu.VMEM((tm, D), BF16)],
        compiler_params=_params("arbitrary", "arbitrary"),
        name="fox_proj",
    )(h2, g.reshape(1, D), w, wf)


def _proj_res_kernel(o_ref, w_ref, h_ref, out_ref):
    out_ref[...] = h_ref[...] + _dot(o_ref[...], w_ref[...])


def _proj_residual(o2, w, h2):
    M, D = h2.shape
    K = o2.shape[1]
    tm = _divisor_tile(M, ROW_TILE)
    return pl.pallas_call(
        _proj_res_kernel,
        grid=(M // tm,),
        in_specs=[
            pl.BlockSpec((tm, K), lambda i: (i, 0)),
            pl.BlockSpec((K, D), lambda i: (0, 0)),
            pl.BlockSpec((tm, D), lambda i: (i, 0)),
        ],
        out_specs=pl.BlockSpec((tm, D), lambda i: (i, 0)),
        out_shape=jax.ShapeDtypeStruct((M, D), F32),
        compiler_params=_params("arbitrary"),
        name="proj_residual",
    )(o2, w, h2)


def _ffn_kernel(h_ref, g_ref, wg_ref, wu_ref, wd_ref, o_ref, a_scr, acc_scr):
    j = pl.program_id(1)

    @pl.when(j == 0)
    def _():
        a_scr[...] = _rmsnorm(h_ref[...], g_ref[...]).astype(BF16)
        acc_scr[...] = jnp.zeros_like(acc_scr)

    a = a_scr[...]
    gate = _dot(a, wg_ref[...])
    up = _dot(a, wu_ref[...])
    act = (gate * jax.nn.sigmoid(gate) * up).astype(BF16)
    acc_scr[...] += _dot(act, wd_ref[...])

    @pl.when(j == pl.num_programs(1) - 1)
    def _():
        o_ref[...] = h_ref[...] + acc_scr[...]


def _ffn(h2, g, wg, wu, wd):
    M, D = h2.shape
    F = wg.shape[1]
    tm = _divisor_tile(M, ROW_TILE)
    tf = FF_TILE
    return pl.pallas_call(
        _ffn_kernel,
        grid=(M // tm, F // tf),
        in_specs=[
            pl.BlockSpec((tm, D), lambda i, j: (i, 0)),
            pl.BlockSpec((1, D), lambda i, j: (0, 0)),
            pl.BlockSpec((D, tf), lambda i, j: (0, j)),
            pl.BlockSpec((D, tf), lambda i, j: (0, j)),
            pl.BlockSpec((tf, D), lambda i, j: (j, 0)),
        ],
        out_specs=pl.BlockSpec((tm, D), lambda i, j: (i, 0)),
        out_shape=jax.ShapeDtypeStruct((M, D), F32),
        scratch_shapes=[pltpu.VMEM((tm, D), BF16), pltpu.VMEM((tm, D), F32)],
        compiler_params=_params("arbitrary", "arbitrary"),
        name="ffn",
    )(h2, g.reshape(1, D), wg, wu, wd)


def _split_heads(x, width):
    if width == 2 * LANES:
        return x[:, :LANES], x[:, LANES:]
    lane = lax.broadcasted_iota(jnp.int32, (1, LANES), 1)
    zero = jnp.zeros_like(x)
    return jnp.where(lane < HEAD_DIM, x, zero), jnp.where(lane >= HEAD_DIM, x, zero)


def _split_keys(x, width):
    if width == 2 * LANES:
        return x[:, :LANES], x[:, LANES:]
    return x, x


def _pair_values(v):
    lane = lax.broadcasted_iota(jnp.int32, (1, LANES), 1)
    zero = jnp.zeros_like(v)
    return jnp.concatenate(
        [jnp.where(lane < HEAD_DIM, v, zero), jnp.where(lane >= HEAD_DIM, v, zero)], axis=0)


def _pair_lanes(x0, x1):
    lane = lax.broadcasted_iota(jnp.int32, (1, LANES), 1)
    return jnp.where(lane < HEAD_DIM, x0, x1)


def _softmax_attn_kernel(*refs, tile, width, scale, decay):
    if decay:
        q_ref, k_ref, v_ref, fcol_ref, frow_ref, o_ref, m_scr, l_scr, acc_scr = refs
    else:
        q_ref, k_ref, v_ref, o_ref, m_scr, l_scr, acc_scr = refs
    hp = pl.program_id(1)
    qi = pl.program_id(2)
    q0, q1 = _split_heads(q_ref[0], width)
    qs = (q0, q1)
    row = qi * tile + lax.broadcasted_iota(jnp.int32, (tile, 1), 0)
    if decay:
        lane = lax.broadcasted_iota(jnp.int32, (1, LANES), 1)
        fcol = fcol_ref[0]
        fq = tuple(jnp.sum(jnp.where(lane == 2 * hp + e, fcol, 0.0), axis=1, keepdims=True)
                   for e in range(2))

    m_scr[...] = jnp.full_like(m_scr, MASK_VALUE)
    l_scr[...] = jnp.zeros_like(l_scr)
    acc_scr[...] = jnp.zeros_like(acc_scr)

    def block(j, masked):
        start = pl.multiple_of(j * tile, tile)
        k = k_ref[0, pl.ds(start, tile), :]
        v = v_ref[0, pl.ds(start, tile), :]
        ks = _split_keys(k, width)
        if masked:
            col = start + lax.broadcasted_iota(jnp.int32, (1, tile), 1)
            valid = (col >= ROW0) & (col <= row)
        ps, alphas = [], []
        for e in range(2):
            s = _dot_nt(qs[e], ks[e]) * scale
            if decay:
                fk = frow_ref[0, pl.ds(2 * hp + e, 1), pl.ds(start, tile)]
                s = s + (fq[e] - fk)
            if masked:
                s = jnp.where(valid, s, MASK_VALUE)
            m_old = m_scr[e]
            m_new = jnp.maximum(m_old, jnp.max(s, axis=1, keepdims=True))
            alpha = jnp.exp(m_old - m_new)
            p = jnp.exp(s - m_new)
            l_scr[e] = alpha * l_scr[e] + jnp.sum(p, axis=1, keepdims=True)
            m_scr[e] = m_new
            ps.append(p.astype(BF16))
            alphas.append(alpha)
        pv = _dot(jnp.concatenate(ps, axis=1), _pair_values(v))
        acc_scr[...] = _pair_lanes(alphas[0], alphas[1]) * acc_scr[...] + pv

    block(0, True)
    lax.fori_loop(1, qi, lambda j, c: (block(j, False), c)[1], 0)

    @pl.when(qi > 0)
    def _():
        block(qi, True)

    o_ref[0] = (acc_scr[...] / _pair_lanes(l_scr[0], l_scr[1])).astype(o_ref.dtype)


def _softmax_attention(q, k, v, *, width, scale, n_pairs, offs=(0, 0, 0), fcol=None, frow=None):
    B, Lp, _ = v.shape
    qo, ko, vo = offs
    tile = ATT_TILE
    decay = fcol is not None
    kern = functools.partial(_softmax_attn_kernel, tile=tile, width=width, scale=scale, decay=decay)
    in_specs = [
        pl.BlockSpec((1, tile, width), lambda b, h, i: (b, i, qo + h)),
        pl.BlockSpec((1, Lp, width), lambda b, h, i: (b, 0, ko + h)),
        pl.BlockSpec((1, Lp, LANES), lambda b, h, i: (b, 0, vo + h)),
    ]
    args = [q, k, v]
    if decay:
        in_specs += [
            pl.BlockSpec((1, tile, LANES), lambda b, h, i: (b, i, 0)),
            pl.BlockSpec((1, N_HEADS, Lp), lambda b, h, i: (b, 0, 0)),
        ]
        args += [fcol, frow]
    return pl.pallas_call(
        kern,
        grid=(B, n_pairs, Lp // tile),
        in_specs=in_specs,
        out_specs=pl.BlockSpec((1, tile, LANES), lambda b, h, i: (b, i, h)),
        out_shape=jax.ShapeDtypeStruct((B, Lp, n_pairs * LANES), BF16),
        scratch_shapes=[
            pltpu.VMEM((2, tile, 1), F32),
            pltpu.VMEM((2, tile, 1), F32),
            pltpu.VMEM((tile, LANES), F32),
        ],
        compiler_params=_params("arbitrary", "arbitrary", "arbitrary"),
        name="fox_attention" if decay else "mla_attention",
    )(*args)


def _sb_attn_kernel(q_ref, k_ref, v_ref, tri_ref, o_ref, r_scr, acc_scr, *, tile, scale):
    qi = pl.program_id(2)
    q0, q1 = _split_heads(q_ref[0], LANES)
    qs = (q0, q1)
    row = qi * tile + lax.broadcasted_iota(jnp.int32, (tile, 1), 0)
    r_scr[...] = jnp.zeros_like(r_scr)
    acc_scr[...] = jnp.zeros_like(acc_scr)
    tri2 = tri_ref[...]

    def block(j, masked):
        start = pl.multiple_of(j * tile, tile)
        k = k_ref[0, pl.ds(start, tile), :]
        v = v_ref[0, pl.ds(start, tile), :]
        if masked:
            col = start + lax.broadcasted_iota(jnp.int32, (1, tile), 1)
            valid = (col >= ROW0) & (col < row)
        ws = []
        for e in range(2):
            z = _dot_nt(qs[e], k) * scale
            lk = _log_sigmoid_neg(z)
            if masked:
                lk = jnp.where(valid, lk, 0.0)
            lk_hi = lk.astype(BF16)
            lk_lo = (lk - lk_hi.astype(F32)).astype(BF16)
            later = _dot(jnp.concatenate([lk_hi, lk_lo], axis=1), tri2) + r_scr[e]
            w = jnp.exp(lk + z + later)
            if masked:
                w = jnp.where(valid, w, 0.0)
            r_scr[e] = r_scr[e] + jnp.sum(lk, axis=1, keepdims=True)
            ws.append(w.astype(BF16))
        acc_scr[...] += _dot(jnp.concatenate(ws, axis=1), _pair_values(v))

    block(qi, True)

    def body(t, c):
        block(qi - 1 - t, False)
        return c

    lax.fori_loop(0, qi - 1, body, 0)

    @pl.when(qi > 0)
    def _():
        block(0, True)

    o_ref[0] = acc_scr[...].astype(o_ref.dtype)


def _sb_attention(qkv, *, scale, n_pairs):
    B, Lp, _ = qkv.shape
    tile = ATT_TILE
    r = lax.broadcasted_iota(jnp.int32, (tile, tile), 0)
    c = lax.broadcasted_iota(jnp.int32, (tile, tile), 1)
    tri = (r > c).astype(BF16)
    tri2 = jnp.concatenate([tri, tri], axis=0)
    kern = functools.partial(_sb_attn_kernel, tile=tile, scale=scale)
    return pl.pallas_call(
        kern,
        grid=(B, n_pairs, Lp // tile),
        in_specs=[
            pl.BlockSpec((1, tile, LANES), lambda b, h, i: (b, i, h)),
            pl.BlockSpec((1, Lp, LANES), lambda b, h, i: (b, 0, n_pairs + h)),
            pl.BlockSpec((1, Lp, LANES), lambda b, h, i: (b, 0, 2 * n_pairs + h)),
            pl.BlockSpec((2 * tile, tile), lambda b, h, i: (0, 0)),
        ],
        out_specs=pl.BlockSpec((1, tile, LANES), lambda b, h, i: (b, i, h)),
        out_shape=jax.ShapeDtypeStruct((B, Lp, n_pairs * LANES), BF16),
        scratch_shapes=[
            pltpu.VMEM((2, tile, 1), F32),
            pltpu.VMEM((tile, LANES), F32),
        ],
        compiler_params=_params("arbitrary", "arbitrary", "arbitrary"),
        name="sb_attention",
    )(qkv, qkv, qkv, tri2)


def _mla_proj_kernel(h_ref, g_ref, wd_ref, qn_ref, kvn_ref, wq_ref, wk_ref, wv_ref, inv_ref,
                     q_ref, k_ref, v_ref, *, tm, rows_per_batch):
    i = pl.program_id(0)
    a = _rmsnorm(h_ref[...], g_ref[...]).astype(BF16)
    down = _dot(a, wd_ref[...])
    cq = _rmsnorm(down[:, :MLA_Q_RANK], qn_ref[...]).astype(BF16)
    ckv = _rmsnorm(down[:, MLA_Q_RANK:MLA_Q_RANK + MLA_KV_RANK], kvn_ref[...]).astype(BF16)
    kr = down[:, MLA_Q_RANK + MLA_KV_RANK:]

    row = (i * tm) % rows_per_batch + lax.broadcasted_iota(jnp.int32, (tm, 1), 0)
    pos = (row - ROW0).astype(F32)
    ang = pos * inv_ref[...]
    lane = lax.broadcasted_iota(jnp.int32, (1, LANES), 1)
    half = MLA_ROPE // 2
    in_lo = (lane >= MLA_NOPE) & (lane < MLA_NOPE + half)
    in_hi = (lane >= MLA_NOPE + half) & (lane < MLA_NOPE + MLA_ROPE)
    cos = jnp.cos(ang)
    sin = jnp.sin(ang)
    c_tab = jnp.where(lane < MLA_NOPE, 1.0, jnp.where(in_lo | in_hi, cos, 0.0))
    s_lo = jnp.where(in_lo, -sin, 0.0)
    s_hi = jnp.where(in_hi, sin, 0.0)

    def rope(x):
        return (x * c_tab + pltpu.roll(x, LANES - half, 1) * s_lo + pltpu.roll(x, half, 1) * s_hi)

    q = _dot(cq, wq_ref[...])
    kn = _dot(ckv, wk_ref[...])
    v_ref[...] = _dot(ckv, wv_ref[...]).astype(BF16)
    kr_rot = rope(kr)
    for hd in range(N_HEADS):
        cols = slice(hd * LANES, (hd + 1) * LANES)
        q_ref[:, cols] = rope(q[:, cols]).astype(BF16)
        k_ref[:, cols] = (kn[:, cols] + kr_rot).astype(BF16)


def _mla_proj(h2, g, wd, qn, kvn, wq, wk, wv, inv_lane, rows_per_batch):
    M, D = h2.shape
    tm = _divisor_tile(rows_per_batch, 512)
    kern = functools.partial(_mla_proj_kernel, tm=tm, rows_per_batch=rows_per_batch)
    full = lambda a: pl.BlockSpec(a.shape, lambda i: (0,) * a.ndim)
    g2, qn2, kvn2 = g.reshape(1, D), qn.reshape(1, -1), kvn.reshape(1, -1)
    return pl.pallas_call(
        kern,
        grid=(M // tm,),
        in_specs=[pl.BlockSpec((tm, D), lambda i: (i, 0)), full(g2), full(wd), full(qn2), full(kvn2),
                  full(wq), full(wk), full(wv), full(inv_lane)],
        out_specs=[
            pl.BlockSpec((tm, N_HEADS * LANES), lambda i: (i, 0)),
            pl.BlockSpec((tm, N_HEADS * LANES), lambda i: (i, 0)),
            pl.BlockSpec((tm, N_HEADS * MLA_V), lambda i: (i, 0)),
        ],
        out_shape=[
            jax.ShapeDtypeStruct((M, N_HEADS * LANES), BF16),
            jax.ShapeDtypeStruct((M, N_HEADS * LANES), BF16),
            jax.ShapeDtypeStruct((M, N_HEADS * MLA_V), BF16),
        ],
        compiler_params=_params("arbitrary"),
        name="mla_proj",
    )(h2, g2, wd, qn2, kvn2, wq, wk, wv, inv_lane)


def _forget_cumsum_kernel(f_ref, b_ref, fcol_ref, frow_ref, carry, *, tile):
    t = pl.program_id(1)

    @pl.when(t == 0)
    def _():
        carry[...] = jnp.zeros_like(carry)

    row_in = lax.broadcasted_iota(jnp.int32, (tile, 1), 0)
    row = t * tile + row_in
    x = jnp.where(row >= ROW0, _log_sigmoid_neg(-(f_ref[0] + b_ref[...])), 0.0)
    shift = 1
    while shift < tile:
        x = x + jnp.where(row_in >= shift, pltpu.roll(x, shift, 0), 0.0)
        shift *= 2
    x = x + carry[...]
    carry[...] = x[tile - 1:tile, :]
    fcol_ref[0] = x
    frow_ref[0] = x.T[:N_HEADS, :]


def _forget_cumsum(flog, b_lane):
    B, Lp, _ = flog.shape
    tile = ATT_TILE
    kern = functools.partial(_forget_cumsum_kernel, tile=tile)
    return pl.pallas_call(
        kern,
        grid=(B, Lp // tile),
        in_specs=[
            pl.BlockSpec((1, tile, LANES), lambda b, t: (b, t, 0)),
            pl.BlockSpec((1, LANES), lambda b, t: (0, 0)),
        ],
        out_specs=[
            pl.BlockSpec((1, tile, LANES), lambda b, t: (b, t, 0)),
            pl.BlockSpec((1, N_HEADS, tile), lambda b, t: (b, 0, t)),
        ],
        out_shape=[
            jax.ShapeDtypeStruct((B, Lp, LANES), F32),
            jax.ShapeDtypeStruct((B, N_HEADS, Lp), F32),
        ],
        scratch_shapes=[pltpu.VMEM((1, LANES), F32)],
        compiler_params=_params("arbitrary", "arbitrary"),
        name="forget_cumsum",
    )(flog, b_lane)


def _final_norm_kernel(h_ref, g_ref, o_ref):
    o_ref[0] = _rmsnorm(h_ref[0], g_ref[...])


def _final_norm(h, g, seq):
    B, Lp, D = h.shape
    tile = PAD_FRONT
    skip = PAD_FRONT // tile
    return pl.pallas_call(
        _final_norm_kernel,
        grid=(B, seq // tile),
        in_specs=[
            pl.BlockSpec((1, tile, D), lambda b, t: (b, t + skip, 0)),
            pl.BlockSpec((1, D), lambda b, t: (0, 0)),
        ],
        out_specs=pl.BlockSpec((1, tile, D), lambda b, t: (b, t, 0)),
        out_shape=jax.ShapeDtypeStruct((B, seq, D), F32),
        compiler_params=_params("arbitrary", "arbitrary"),
        name="final_norm",
    )(h, g.reshape(1, D))


def _pad_heads(w, per_head):
    K = w.shape[0]
    w3 = w.reshape(K, N_HEADS, per_head)
    w3 = jnp.pad(w3, ((0, 0), (0, 0), (0, LANES - per_head)))
    return w3.reshape(K, N_HEADS * LANES)


def kernel(x, meta, norm_mix, norm_ffn, pool_w, pool_scale, sb_w_qkv, sb_w_o, mla_w_down, mla_q_norm,
           mla_kv_norm, mla_w_uq, mla_w_ukv, mla_w_o, fox_w_qkvf, fox_b_f, fox_w_o, ffn_w_gate, ffn_w_up,
           ffn_w_down, final_norm):
    B, S, D = x.shape
    assert D == D_MODEL and S % ATT_TILE == 0
    Lp = S + PAD_FRONT
    M = B * Lp
    HD = N_HEADS * HEAD_DIM

    meta_b = jnp.broadcast_to(meta[None].astype(x.dtype), (B, N_META, D))
    h = jnp.concatenate([jnp.zeros((B, ROW0, D), x.dtype), meta_b, x], axis=1)

    def ffn(h, i):
        return _ffn(h.reshape(M, D), norm_ffn[i], ffn_w_gate[i].astype(BF16), ffn_w_up[i].astype(BF16),
                    ffn_w_down[i].astype(BF16)).reshape(B, Lp, D)

    n_pairs = N_HEADS // 2

    def pool_mixer(h, i, j):
        return _pool_layer(h, norm_mix[i], pool_w[j], pool_scale[j])

    def sb_mixer(h, i, j):
        qkv = _norm_proj(h.reshape(M, D), norm_mix[i], sb_w_qkv[j].astype(BF16), tn=HD)
        o = _sb_attention(qkv.reshape(B, Lp, 3 * HD), scale=HEAD_DIM ** -0.5, n_pairs=n_pairs)
        return _proj_residual(o.reshape(M, HD), sb_w_o[j].astype(BF16), h.reshape(M, D)).reshape(B, Lp, D)

    def mla_mixer(h, i, j):
        return _mla_layer(h, norm_mix[i], mla_w_down[j], mla_q_norm[j], mla_kv_norm[j], mla_w_uq[j],
                          mla_w_ukv[j], mla_w_o[j])

    def fox_mixer(h, i, j):
        wf = jnp.pad(fox_w_qkvf[j][:, 3 * HD:], ((0, 0), (0, LANES - N_HEADS))).astype(BF16)
        qkv, flog = _fox_proj(h.reshape(M, D), norm_mix[i], fox_w_qkvf[j][:, :3 * HD].astype(BF16), wf, tn=HD)
        b_lane = jnp.pad(fox_b_f[j].astype(F32), (0, LANES - N_HEADS)).reshape(1, LANES)
        fcol, frow = _forget_cumsum(flog.reshape(B, Lp, LANES), b_lane)
        qkv = qkv.reshape(B, Lp, 3 * HD)
        o = _softmax_attention(qkv, qkv, qkv, width=LANES, scale=HEAD_DIM ** -0.5, n_pairs=n_pairs,
                               offs=(0, n_pairs, 2 * n_pairs), fcol=fcol, frow=frow)
        return _proj_residual(o.reshape(M, HD), fox_w_o[j].astype(BF16), h.reshape(M, D)).reshape(B, Lp, D)

    mixers = (pool_mixer, sb_mixer, mla_mixer, fox_mixer)
    for i in range(norm_mix.shape[0]):
        h = mixers[i % len(mixers)](h, i, i // len(mixers))
        h = ffn(h, i)
    return _final_norm(h, final_norm, S)


def _mla_layer(h, g, w_down, q_norm, kv_norm, w_uq, w_ukv, w_o):
    B, Lp, D = h.shape
    M = B * Lp
    n_lat = MLA_Q_RANK + MLA_KV_RANK
    wd_rope = jnp.pad(w_down[:, n_lat:], ((0, 0), (MLA_NOPE, LANES - MLA_NOPE - MLA_ROPE)))
    wd_p = jnp.concatenate([w_down[:, :n_lat], wd_rope], axis=1).astype(BF16)
    wq_p = _pad_heads(w_uq, MLA_NOPE + MLA_ROPE).astype(BF16)
    wkv3 = w_ukv.reshape(MLA_KV_RANK, N_HEADS, MLA_NOPE + MLA_V)
    wk_p = _pad_heads(wkv3[:, :, :MLA_NOPE].reshape(MLA_KV_RANK, -1), MLA_NOPE).astype(BF16)
    wv_p = wkv3[:, :, MLA_NOPE:].reshape(MLA_KV_RANK, N_HEADS * MLA_V).astype(BF16)
    inv = ROPE_THETA ** (-jnp.arange(0, MLA_ROPE, 2, dtype=F32) / MLA_ROPE)
    inv_lane = jnp.concatenate([jnp.zeros((MLA_NOPE,), F32), inv, inv,
                                jnp.zeros((LANES - MLA_NOPE - MLA_ROPE,), F32)]).reshape(1, LANES)
    q, k, v = _mla_proj(h.reshape(M, D), g, wd_p, q_norm, kv_norm, wq_p, wk_p, wv_p, inv_lane, Lp)
    o = _softmax_attention(q.reshape(B, Lp, -1), k.reshape(B, Lp, -1), v.reshape(B, Lp, -1),
                           width=2 * LANES, scale=(MLA_NOPE + MLA_ROPE) ** -0.5, n_pairs=N_HEADS // 2)
    return _proj_residual(o.reshape(M, -1), w_o.astype(BF16), h.reshape(M, D)).reshape(B, Lp, D)
```

```python
import functools

import jax
import jax.numpy as jnp
from jax import lax
from jax.experimental import pallas as pl
from jax.experimental.pallas import tpu as pltpu

F32 = jnp.float32
BF16 = jnp.bfloat16

D_MODEL = 1024
N_META = 16
EPS = 1e-6
POOL_WINDOWS = (2, 4, 8, 16)
POOL_GROUP = D_MODEL // len(POOL_WINDOWS)
N_HEADS = 16
HEAD_DIM = 64
MLA_Q_RANK = 384
MLA_KV_RANK = 256
MLA_NOPE = 64
MLA_ROPE = 32
MLA_V = 64
ROPE_THETA = 10000.0

LANES = 128
PAD_FRONT = 256
ROW0 = PAD_FRONT - N_META
HALO = max(POOL_WINDOWS)
MASK_VALUE = -1e30
DEAD_LOG_WEIGHT = -104.0
VMEM_LIMIT = 56 * 1024 * 1024

ROW_TILE = 1024
ATT_TILE = 256
FF_TILE = 256


def _divisor_tile(n, target):
    best = None
    for t in range(8, min(n, target) + 1, 8):
        if n % t == 0:
            best = t
    assert best is not None, (n, target)
    return best


def _params(*sem):
    return pltpu.CompilerParams(dimension_semantics=sem, vmem_limit_bytes=VMEM_LIMIT)


def _rmsnorm(x, g):
    return x * lax.rsqrt(jnp.mean(x * x, axis=-1, keepdims=True) + EPS) * g


def _log_sigmoid_neg(z):
    return -(jnp.maximum(z, 0.0) + jnp.log1p(jnp.exp(-jnp.abs(z))))


def _dot(a, b):
    return jnp.dot(a, b, preferred_element_type=F32)


def _dot_nt(a, b):
    return lax.dot_general(a, b, (((1,), (1,)), ((), ())), preferred_element_type=F32)


def _pool_kernel(h_ref, g_ref, w_ref, sc_ref, o_ref, abuf, *, tile):
    t = pl.program_id(1)
    h = h_ref[0]
    row = t * tile + lax.broadcasted_iota(jnp.int32, (tile, 1), 0)
    a = jnp.where(row >= ROW0, _rmsnorm(h, g_ref[...]), 0.0)

    @pl.when(t == 0)
    def _():
        abuf[0:HALO, :] = jnp.zeros((HALO, D_MODEL), F32)

    @pl.when(t > 0)
    def _():
        abuf[0:HALO, :] = abuf[tile:tile + HALO, :]

    abuf[HALO:HALO + tile, :] = a
    pos1 = row - ROW0 + 1
    outs = []
    for g, win in enumerate(POOL_WINDOWS):
        cols = slice(g * POOL_GROUP, (g + 1) * POOL_GROUP)
        xg = a[:, cols]
        acc = xg
        for j in range(1, win):
            acc = acc + abuf[HALO - j:HALO - j + tile, cols]
        cnt = jnp.clip(pos1, 1, win).astype(F32)
        pooled = acc / cnt - xg
        outs.append(_dot(pooled.astype(BF16), w_ref[g]))
    mixed = jnp.concatenate(outs, axis=1) * sc_ref[...]
    o_ref[0] = h + mixed


def _pool_layer(h, g, w, sc):
    B, Lp, D = h.shape
    tile = ATT_TILE
    kern = functools.partial(_pool_kernel, tile=tile)
    return pl.pallas_call(
        kern,
        grid=(B, Lp // tile),
        in_specs=[
            pl.BlockSpec((1, tile, D), lambda b, t: (b, t, 0)),
            pl.BlockSpec((1, D), lambda b, t: (0, 0)),
            pl.BlockSpec((len(POOL_WINDOWS), POOL_GROUP, POOL_GROUP), lambda b, t: (0, 0, 0)),
            pl.BlockSpec((1, D), lambda b, t: (0, 0)),
        ],
        out_specs=pl.BlockSpec((1, tile, D), lambda b, t: (b, t, 0)),
        out_shape=jax.ShapeDtypeStruct((B, Lp, D), F32),
        scratch_shapes=[pltpu.VMEM((tile + HALO, D), F32)],
        compiler_params=_params("arbitrary", "arbitrary"),
        name="pool_layer",
    )(h, g.reshape(1, D), w.astype(BF16), sc.reshape(1, D))


def _norm_proj_kernel(h_ref, g_ref, w_ref, o_ref, a_scr):
    @pl.when(pl.program_id(1) == 0)
    def _():
        a_scr[...] = _rmsnorm(h_ref[...], g_ref[...]).astype(BF16)

    o_ref[...] = _dot(a_scr[...], w_ref[...]).astype(o_ref.dtype)


def _norm_proj(h2, g, w, tn):
    M, D = h2.shape
    N = w.shape[1]
    tm = _divisor_tile(M, ROW_TILE)
    return pl.pallas_call(
        _norm_proj_kernel,
        grid=(M // tm, N // tn),
        in_specs=[
            pl.BlockSpec((tm, D), lambda i, j: (i, 0)),
            pl.BlockSpec((1, D), lambda i, j: (0, 0)),
            pl.BlockSpec((D, tn), lambda i, j: (0, j)),
        ],
        out_specs=pl.BlockSpec((tm, tn), lambda i, j: (i, j)),
        out_shape=jax.ShapeDtypeStruct((M, N), BF16),
        scratch_shapes=[pltpu.VMEM((tm, D), BF16)],
        compiler_params=_params("arbitrary", "arbitrary"),
        name="norm_proj",
    )(h2, g.reshape(1, D), w)


def _fox_proj_kernel(h_ref, g_ref, w_ref, wf_ref, o_ref, f_ref, a_scr):
    @pl.when(pl.program_id(1) == 0)
    def _():
        a = _rmsnorm(h_ref[...], g_ref[...]).astype(BF16)
        a_scr[...] = a
        f_ref[...] = _dot(a, wf_ref[...])

    o_ref[...] = _dot(a_scr[...], w_ref[...]).astype(o_ref.dtype)


def _fox_proj(h2, g, w, wf, tn):
    M, D = h2.shape
    N = w.shape[1]
    tm = _divisor_tile(M, ROW_TILE)
    return pl.pallas_call(
        _fox_proj_kernel,
        grid=(M // tm, N // tn),
        in_specs=[
            pl.BlockSpec((tm, D), lambda i, j: (i, 0)),
            pl.BlockSpec((1, D), lambda i, j: (0, 0)),
            pl.BlockSpec((D, tn), lambda i, j: (0, j)),
            pl.BlockSpec((D, LANES), lambda i, j: (0, 0)),
        ],
        out_specs=[
            pl.BlockSpec((tm, tn), lambda i, j: (i, j)),
            pl.BlockSpec((tm, LANES), lambda i, j: (i, 0)),
        ],
        out_shape=[
            jax.ShapeDtypeStruct((M, N), BF16),
            jax.ShapeDtypeStruct((M, LANES), F32),
        ],
        scratch_shapes=[pltpu.VMEM((tm, D), BF16)],
        compiler_params=_params("arbitrary", "arbitrary"),
        name="fox_proj",
    )(h2, g.reshape(1, D), w, wf)


def _proj_res_kernel(o_ref, w_ref, h_ref, out_ref):
    out_ref[...] = h_ref[...] + _dot(o_ref[...], w_ref[...])


def _proj_residual(o2, w, h2):
    M, D = h2.shape
    K = o2.shape[1]
    tm = _divisor_tile(M, ROW_TILE)
    return pl.pallas_call(
        _proj_res_kernel,
        grid=(M // tm,),
        in_specs=[
            pl.BlockSpec((tm, K), lambda i: (i, 0)),
            pl.BlockSpec((K, D), lambda i: (0, 0)),
            pl.BlockSpec((tm, D), lambda i: (i, 0)),
        ],
        out_specs=pl.BlockSpec((tm, D), lambda i: (i, 0)),
        out_shape=jax.ShapeDtypeStruct((M, D), F32),
        compiler_params=_params("arbitrary"),
        name="proj_residual",
    )(o2, w, h2)


def _ffn_kernel(h_ref, g_ref, wg_ref, wu_ref, wd_ref, o_ref, a_scr, acc_scr):
    j = pl.program_id(1)

    @pl.when(j == 0)
    def _():
        a_scr[...] = _rmsnorm(h_ref[...], g_ref[...]).astype(BF16)
        acc_scr[...] = jnp.zeros_like(acc_scr)

    a = a_scr[...]
    gate = _dot(a, wg_ref[...])
    up = _dot(a, wu_ref[...])
    act = (gate * jax.nn.sigmoid(gate) * up).astype(BF16)
    acc_scr[...] += _dot(act, wd_ref[...])

    @pl.when(j == pl.num_programs(1) - 1)
    def _():
        o_ref[...] = h_ref[...] + acc_scr[...]


def _ffn(h2, g, wg, wu, wd):
    M, D = h2.shape
    F = wg.shape[1]
    tm = _divisor_tile(M, ROW_TILE)
    tf = FF_TILE
    return pl.pallas_call(
        _ffn_kernel,
        grid=(M // tm, F // tf),
        in_specs=[
            pl.BlockSpec((tm, D), lambda i, j: (i, 0)),
            pl.BlockSpec((1, D), lambda i, j: (0, 0)),
            pl.BlockSpec((D, tf), lambda i, j: (0, j)),
            pl.BlockSpec((D, tf), lambda i, j: (0, j)),
            pl.BlockSpec((tf, D), lambda i, j: (j, 0)),
        ],
        out_specs=pl.BlockSpec((tm, D), lambda i, j: (i, 0)),
        out_shape=jax.ShapeDtypeStruct((M, D), F32),
        scratch_shapes=[pltpu.VMEM((tm, D), BF16), pltpu.VMEM((tm, D), F32)],
        compiler_params=_params("arbitrary", "arbitrary"),
        name="ffn",
    )(h2, g.reshape(1, D), wg, wu, wd)


def _split_heads(x, width):
    if width == 2 * LANES:
        return x[:, :LANES], x[:, LANES:]
    lane = lax.broadcasted_iota(jnp.int32, (1, LANES), 1)
    zero = jnp.zeros_like(x)
    return jnp.where(lane < HEAD_DIM, x, zero), jnp.where(lane >= HEAD_DIM, x, zero)


def _split_keys(x, width):
    if width == 2 * LANES:
        return x[:, :LANES], x[:, LANES:]
    return x, x


def _pair_values(v):
    lane = lax.broadcasted_iota(jnp.int32, (1, LANES), 1)
    zero = jnp.zeros_like(v)
    return jnp.concatenate(
        [jnp.where(lane < HEAD_DIM, v, zero), jnp.where(lane >= HEAD_DIM, v, zero)], axis=0)


def _pair_lanes(x0, x1):
    lane = lax.broadcasted_iota(jnp.int32, (1, LANES), 1)
    return jnp.where(lane < HEAD_DIM, x0, x1)


def _softmax_attn_kernel(*refs, tile, width, scale, decay):
    if decay:
        q_ref, k_ref, v_ref, fcol_ref, frow_ref, o_ref, m_scr, l_scr, acc_scr = refs
    else:
        q_ref, k_ref, v_ref, o_ref, m_scr, l_scr, acc_scr = refs
    hp = pl.program_id(1)
    qi = pl.program_id(2)
    q0, q1 = _split_heads(q_ref[0], width)
    qs = (q0, q1)
    row = qi * tile + lax.broadcasted_iota(jnp.int32, (tile, 1), 0)
    if decay:
        lane = lax.broadcasted_iota(jnp.int32, (1, LANES), 1)
        fcol = fcol_ref[0]
        fq = tuple(jnp.sum(jnp.where(lane == 2 * hp + e, fcol, 0.0), axis=1, keepdims=True)
                   for e in range(2))

    m_scr[...] = jnp.full_like(m_scr, MASK_VALUE)
    l_scr[...] = jnp.zeros_like(l_scr)
    acc_scr[...] = jnp.zeros_like(acc_scr)

    def block(j, masked):
        start = pl.multiple_of(j * tile, tile)
        k = k_ref[0, pl.ds(start, tile), :]
        v = v_ref[0, pl.ds(start, tile), :]
        ks = _split_keys(k, width)
        if masked:
            col = start + lax.broadcasted_iota(jnp.int32, (1, tile), 1)
            valid = (col >= ROW0) & (col <= row)
        ps, alphas = [], []
        for e in range(2):
            s = _dot_nt(qs[e], ks[e]) * scale
            if decay:
                fk = frow_ref[0, pl.ds(2 * hp + e, 1), pl.ds(start, tile)]
                s = s + (fq[e] - fk)
            if masked:
                s = jnp.where(valid, s, MASK_VALUE)
            m_old = m_scr[e]
            m_new = jnp.maximum(m_old, jnp.max(s, axis=1, keepdims=True))
            alpha = jnp.exp(m_old - m_new)
            p = jnp.exp(s - m_new)
            l_scr[e] = alpha * l_scr[e] + jnp.sum(p, axis=1, keepdims=True)
            m_scr[e] = m_new
            ps.append(p.astype(BF16))
            alphas.append(alpha)
        pv = _dot(jnp.concatenate(ps, axis=1), _pair_values(v))
        acc_scr[...] = _pair_lanes(alphas[0], alphas[1]) * acc_scr[...] + pv

    block(0, True)
    lax.fori_loop(1, qi, lambda j, c: (block(j, False), c)[1], 0)

    @pl.when(qi > 0)
    def _():
        block(qi, True)

    o_ref[0] = (acc_scr[...] / _pair_lanes(l_scr[0], l_scr[1])).astype(o_ref.dtype)


def _softmax_attention(q, k, v, *, width, scale, n_pairs, offs=(0, 0, 0), fcol=None, frow=None):
    B, Lp, _ = v.shape
    qo, ko, vo = offs
    tile = ATT_TILE
    decay = fcol is not None
    kern = functools.partial(_softmax_attn_kernel, tile=tile, width=width, scale=scale, decay=decay)
    in_specs = [
        pl.BlockSpec((1, tile, width), lambda b, h, i: (b, i, qo + h)),
        pl.BlockSpec((1, Lp, width), lambda b, h, i: (b, 0, ko + h)),
        pl.BlockSpec((1, Lp, LANES), lambda b, h, i: (b, 0, vo + h)),
    ]
    args = [q, k, v]
    if decay:
        in_specs += [
            pl.BlockSpec((1, tile, LANES), lambda b, h, i: (b, i, 0)),
            pl.BlockSpec((1, N_HEADS, Lp), lambda b, h, i: (b, 0, 0)),
        ]
        args += [fcol, frow]
    return pl.pallas_call(
        kern,
        grid=(B, n_pairs, Lp // tile),
        in_specs=in_specs,
        out_specs=pl.BlockSpec((1, tile, LANES), lambda b, h, i: (b, i, h)),
        out_shape=jax.ShapeDtypeStruct((B, Lp, n_pairs * LANES), BF16),
        scratch_shapes=[
            pltpu.VMEM((2, tile, 1), F32),
            pltpu.VMEM((2, tile, 1), F32),
            pltpu.VMEM((tile, LANES), F32),
        ],
        compiler_params=_params("arbitrary", "arbitrary", "arbitrary"),
        name="fox_attention" if decay else "mla_attention",
    )(*args)


def _sb_attn_kernel(q_ref, k_ref, v_ref, tri_ref, o_ref, r_scr, acc_scr, *, tile, scale):
    qi = pl.program_id(2)
    q0, q1 = _split_heads(q_ref[0], LANES)
    qs = (q0, q1)
    row = qi * tile + lax.broadcasted_iota(jnp.int32, (tile, 1), 0)
    r_scr[...] = jnp.zeros_like(r_scr)
    acc_scr[...] = jnp.zeros_like(acc_scr)
    tri2 = tri_ref[...]

    def block(j, masked):
        start = pl.multiple_of(j * tile, tile)
        k = k_ref[0, pl.ds(start, tile), :]
        v = v_ref[0, pl.ds(start, tile), :]
        if masked:
            col = start + lax.broadcasted_iota(jnp.int32, (1, tile), 1)
            valid = (col >= ROW0) & (col < row)
        ws = []
        for e in range(2):
            z = _dot_nt(qs[e], k) * scale
            lk = _log_sigmoid_neg(z)
            if masked:
                lk = jnp.where(valid, lk, 0.0)
            lk_hi = lk.astype(BF16)
            lk_lo = (lk - lk_hi.astype(F32)).astype(BF16)
            later = _dot(jnp.concatenate([lk_hi, lk_lo], axis=1), tri2) + r_scr[e]
            w = jnp.exp(lk + z + later)
            if masked:
                w = jnp.where(valid, w, 0.0)
            r_scr[e] = r_scr[e] + jnp.sum(lk, axis=1, keepdims=True)
            ws.append(w.astype(BF16))
        acc_scr[...] += _dot(jnp.concatenate(ws, axis=1), _pair_values(v))

    def live():
        return (jnp.max(jnp.maximum(r_scr[0], r_scr[1])) > DEAD_LOG_WEIGHT).astype(jnp.int32)

    block(qi, True)

    def body(c):
        block(c[0], False)
        return c[0] - 1, live()

    _, alive = lax.while_loop(lambda c: (c[0] >= 1) & (c[1] > 0), body, (qi - 1, live()))

    @pl.when((qi > 0) & (alive > 0))
    def _():
        block(0, True)

    o_ref[0] = acc_scr[...].astype(o_ref.dtype)


def _sb_attention(qkv, *, scale, n_pairs):
    B, Lp, _ = qkv.shape
    tile = ATT_TILE
    r = lax.broadcasted_iota(jnp.int32, (tile, tile), 0)
    c = lax.broadcasted_iota(jnp.int32, (tile, tile), 1)
    tri = (r > c).astype(BF16)
    tri2 = jnp.concatenate([tri, tri], axis=0)
    kern = functools.partial(_sb_attn_kernel, tile=tile, scale=scale)
    return pl.pallas_call(
        kern,
        grid=(B, n_pairs, Lp // tile),
        in_specs=[
            pl.BlockSpec((1, tile, LANES), lambda b, h, i: (b, i, h)),
            pl.BlockSpec((1, Lp, LANES), lambda b, h, i: (b, 0, n_pairs + h)),
            pl.BlockSpec((1, Lp, LANES), lambda b, h, i: (b, 0, 2 * n_pairs + h)),
            pl.BlockSpec((2 * tile, tile), lambda b, h, i: (0, 0)),
        ],
        out_specs=pl.BlockSpec((1, tile, LANES), lambda b, h, i: (b, i, h)),
        out_shape=jax.ShapeDtypeStruct((B, Lp, n_pairs * LANES), BF16),
        scratch_shapes=[
            pltpu.VMEM((2, tile, 1), F32),
            pltpu.VMEM((tile, LANES), F32),
        ],
        compiler_params=_params("arbitrary", "arbitrary", "arbitrary"),
        name="sb_attention",
    )(qkv, qkv, qkv, tri2)


def _mla_proj_kernel(h_ref, g_ref, wd_ref, qn_ref, kvn_ref, wq_ref, wk_ref, wv_ref, inv_ref,
                     q_ref, k_ref, v_ref, *, tm, rows_per_batch):
    i = pl.program_id(0)
    a = _rmsnorm(h_ref[...], g_ref[...]).astype(BF16)
    down = _dot(a, wd_ref[...])
    cq = _rmsnorm(down[:, :MLA_Q_RANK], qn_ref[...]).astype(BF16)
    ckv = _rmsnorm(down[:, MLA_Q_RANK:MLA_Q_RANK + MLA_KV_RANK], kvn_ref[...]).astype(BF16)
    kr = down[:, MLA_Q_RANK + MLA_KV_RANK:]

    row = (i * tm) % rows_per_batch + lax.broadcasted_iota(jnp.int32, (tm, 1), 0)
    pos = (row - ROW0).astype(F32)
    ang = pos * inv_ref[...]
    lane = lax.broadcasted_iota(jnp.int32, (1, LANES), 1)
    half = MLA_ROPE // 2
    in_lo = (lane >= MLA_NOPE) & (lane < MLA_NOPE + half)
    in_hi = (lane >= MLA_NOPE + half) & (lane < MLA_NOPE + MLA_ROPE)
    cos = jnp.cos(ang)
    sin = jnp.sin(ang)
    c_tab = jnp.where(lane < MLA_NOPE, 1.0, jnp.where(in_lo | in_hi, cos, 0.0))
    s_lo = jnp.where(in_lo, -sin, 0.0)
    s_hi = jnp.where(in_hi, sin, 0.0)

    def rope(x):
        return (x * c_tab + pltpu.roll(x, LANES - half, 1) * s_lo + pltpu.roll(x, half, 1) * s_hi)

    q = _dot(cq, wq_ref[...])
    kn = _dot(ckv, wk_ref[...])
    v_ref[...] = _dot(ckv, wv_ref[...]).astype(BF16)
    kr_rot = rope(kr)
    for hd in range(N_HEADS):
        cols = slice(hd * LANES, (hd + 1) * LANES)
        q_ref[:, cols] = rope(q[:, cols]).astype(BF16)
        k_ref[:, cols] = (kn[:, cols] + kr_rot).astype(BF16)


def _mla_proj(h2, g, wd, qn, kvn, wq, wk, wv, inv_lane, rows_per_batch):
    M, D = h2.shape
    tm = _divisor_tile(rows_per_batch, 512)
    kern = functools.partial(_mla_proj_kernel, tm=tm, rows_per_batch=rows_per_batch)
    full = lambda a: pl.BlockSpec(a.shape, lambda i: (0,) * a.ndim)
    g2, qn2, kvn2 = g.reshape(1, D), qn.reshape(1, -1), kvn.reshape(1, -1)
    return pl.pallas_call(
        kern,
        grid=(M // tm,),
        in_specs=[pl.BlockSpec((tm, D), lambda i: (i, 0)), full(g2), full(wd), full(qn2), full(kvn2),
                  full(wq), full(wk), full(wv), full(inv_lane)],
        out_specs=[
            pl.BlockSpec((tm, N_HEADS * LANES), lambda i: (i, 0)),
            pl.BlockSpec((tm, N_HEADS * LANES), lambda i: (i, 0)),
            pl.BlockSpec((tm, N_HEADS * MLA_V), lambda i: (i, 0)),
        ],
        out_shape=[
            jax.ShapeDtypeStruct((M, N_HEADS * LANES), BF16),
            jax.ShapeDtypeStruct((M, N_HEADS * LANES), BF16),
            jax.ShapeDtypeStruct((M, N_HEADS * MLA_V), BF16),
        ],
        compiler_params=_params("arbitrary"),
        name="mla_proj",
    )(h2, g2, wd, qn2, kvn2, wq, wk, wv, inv_lane)


def _forget_cumsum_kernel(f_ref, b_ref, fcol_ref, frow_ref, carry, *, tile):
    t = pl.program_id(1)

    @pl.when(t == 0)
    def _():
        carry[...] = jnp.zeros_like(carry)

    row_in = lax.broadcasted_iota(jnp.int32, (tile, 1), 0)
    row = t * tile + row_in
    x = jnp.where(row >= ROW0, _log_sigmoid_neg(-(f_ref[0] + b_ref[...])), 0.0)
    shift = 1
    while shift < tile:
        x = x + jnp.where(row_in >= shift, pltpu.roll(x, shift, 0), 0.0)
        shift *= 2
    x = x + carry[...]
    carry[...] = x[tile - 1:tile, :]
    fcol_ref[0] = x
    frow_ref[0] = x.T[:N_HEADS, :]


def _forget_cumsum(flog, b_lane):
    B, Lp, _ = flog.shape
    tile = ATT_TILE
    kern = functools.partial(_forget_cumsum_kernel, tile=tile)
    return pl.pallas_call(
        kern,
        grid=(B, Lp // tile),
        in_specs=[
            pl.BlockSpec((1, tile, LANES), lambda b, t: (b, t, 0)),
            pl.BlockSpec((1, LANES), lambda b, t: (0, 0)),
        ],
        out_specs=[
            pl.BlockSpec((1, tile, LANES), lambda b, t: (b, t, 0)),
            pl.BlockSpec((1, N_HEADS, tile), lambda b, t: (b, 0, t)),
        ],
        out_shape=[
            jax.ShapeDtypeStruct((B, Lp, LANES), F32),
            jax.ShapeDtypeStruct((B, N_HEADS, Lp), F32),
        ],
        scratch_shapes=[pltpu.VMEM((1, LANES), F32)],
        compiler_params=_params("arbitrary", "arbitrary"),
        name="forget_cumsum",
    )(flog, b_lane)


def _final_norm_kernel(h_ref, g_ref, o_ref):
    o_ref[0] = _rmsnorm(h_ref[0], g_ref[...])


def _final_norm(h, g, seq):
    B, Lp, D = h.shape
    tile = PAD_FRONT
    skip = PAD_FRONT // tile
    return pl.pallas_call(
        _final_norm_kernel,
        grid=(B, seq // tile),
        in_specs=[
            pl.BlockSpec((1, tile, D), lambda b, t: (b, t + skip, 0)),
            pl.BlockSpec((1, D), lambda b, t: (0, 0)),
        ],
        out_specs=pl.BlockSpec((1, tile, D), lambda b, t: (b, t, 0)),
        out_shape=jax.ShapeDtypeStruct((B, seq, D), F32),
        compiler_params=_params("arbitrary", "arbitrary"),
        name="final_norm",
    )(h, g.reshape(1, D))


def _pad_heads(w, per_head):
    K = w.shape[0]
    w3 = w.reshape(K, N_HEADS, per_head)
    w3 = jnp.pad(w3, ((0, 0), (0, 0), (0, LANES - per_head)))
    return w3.reshape(K, N_HEADS * LANES)


def kernel(x, meta, norm_mix, norm_ffn, pool_w, pool_scale, sb_w_qkv, sb_w_o, mla_w_down, mla_q_norm,
           mla_kv_norm, mla_w_uq, mla_w_ukv, mla_w_o, fox_w_qkvf, fox_b_f, fox_w_o, ffn_w_gate, ffn_w_up,
           ffn_w_down, final_norm):
    B, S, D = x.shape
    assert D == D_MODEL and S % ATT_TILE == 0
    Lp = S + PAD_FRONT
    M = B * Lp
    HD = N_HEADS * HEAD_DIM

    meta_b = jnp.broadcast_to(meta[None].astype(x.dtype), (B, N_META, D))
    h = jnp.concatenate([jnp.zeros((B, ROW0, D), x.dtype), meta_b, x], axis=1)

    def ffn(h, i):
        return _ffn(h.reshape(M, D), norm_ffn[i], ffn_w_gate[i].astype(BF16), ffn_w_up[i].astype(BF16),
                    ffn_w_down[i].astype(BF16)).reshape(B, Lp, D)

    n_pairs = N_HEADS // 2

    def pool_mixer(h, i, j):
        return _pool_layer(h, norm_mix[i], pool_w[j], pool_scale[j])

    def sb_mixer(h, i, j):
        qkv = _norm_proj(h.reshape(M, D), norm_mix[i], sb_w_qkv[j].astype(BF16), tn=HD)
        o = _sb_attention(qkv.reshape(B, Lp, 3 * HD), scale=HEAD_DIM ** -0.5, n_pairs=n_pairs)
        return _proj_residual(o.reshape(M, HD), sb_w_o[j].astype(BF16), h.reshape(M, D)).reshape(B, Lp, D)

    def mla_mixer(h, i, j):
        return _mla_layer(h, norm_mix[i], mla_w_down[j], mla_q_norm[j], mla_kv_norm[j], mla_w_uq[j],
                          mla_w_ukv[j], mla_w_o[j])

    def fox_mixer(h, i, j):
        wf = jnp.pad(fox_w_qkvf[j][:, 3 * HD:], ((0, 0), (0, LANES - N_HEADS))).astype(BF16)
        qkv, flog = _fox_proj(h.reshape(M, D), norm_mix[i], fox_w_qkvf[j][:, :3 * HD].astype(BF16), wf, tn=HD)
        b_lane = jnp.pad(fox_b_f[j].astype(F32), (0, LANES - N_HEADS)).reshape(1, LANES)
        fcol, frow = _forget_cumsum(flog.reshape(B, Lp, LANES), b_lane)
        qkv = qkv.reshape(B, Lp, 3 * HD)
        o = _softmax_attention(qkv, qkv, qkv, width=LANES, scale=HEAD_DIM ** -0.5, n_pairs=n_pairs,
                               offs=(0, n_pairs, 2 * n_pairs), fcol=fcol, frow=frow)
        return _proj_residual(o.reshape(M, HD), fox_w_o[j].astype(BF16), h.reshape(M, D)).reshape(B, Lp, D)

    mixers = (pool_mixer, sb_mixer, mla_mixer, fox_mixer)
    for i in range(norm_mix.shape[0]):
        h = mixers[i % len(mixers)](h, i, i // len(mixers))
        h = ffn(h, i)
    return _final_norm(h, final_norm, S)


def _mla_layer(h, g, w_down, q_norm, kv_norm, w_uq, w_ukv, w_o):
    B, Lp, D = h.shape
    M = B * Lp
    n_lat = MLA_Q_RANK + MLA_KV_RANK
    wd_rope = jnp.pad(w_down[:, n_lat:], ((0, 0), (MLA_NOPE, LANES - MLA_NOPE - MLA_ROPE)))
    wd_p = jnp.concatenate([w_down[:, :n_lat], wd_rope], axis=1).astype(BF16)
    wq_p = _pad_heads(w_uq, MLA_NOPE + MLA_ROPE).astype(BF16)
    wkv3 = w_ukv.reshape(MLA_KV_RANK, N_HEADS, MLA_NOPE + MLA_V)
    wk_p = _pad_heads(wkv3[:, :, :MLA_NOPE].reshape(MLA_KV_RANK, -1), MLA_NOPE).astype(BF16)
    wv_p = wkv3[:, :, MLA_NOPE:].reshape(MLA_KV_RANK, N_HEADS * MLA_V).astype(BF16)
    inv = ROPE_THETA ** (-jnp.arange(0, MLA_ROPE, 2, dtype=F32) / MLA_ROPE)
    inv_lane = jnp.concatenate([jnp.zeros((MLA_NOPE,), F32), inv, inv,
                                jnp.zeros((LANES - MLA_NOPE - MLA_ROPE,), F32)]).reshape(1, LANES)
    q, k, v = _mla_proj(h.reshape(M, D), g, wd_p, q_norm, kv_norm, wq_p, wk_p, wv_p, inv_lane, Lp)
    o = _softmax_attention(q.reshape(B, Lp, -1), k.reshape(B, Lp, -1), v.reshape(B, Lp, -1),
                           width=2 * LANES, scale=(MLA_NOPE + MLA_ROPE) ** -0.5, n_pairs=N_HEADS // 2)
    return _proj_residual(o.reshape(M, -1), w_o.astype(BF16), h.reshape(M, D)).reshape(B, Lp, D)
```

```python
import functools

import jax
import jax.numpy as jnp
from jax import lax
from jax.experimental import pallas as pl
from jax.experimental.pallas import tpu as pltpu

F32 = jnp.float32
BF16 = jnp.bfloat16

D_MODEL = 1024
N_META = 16
EPS = 1e-6
POOL_WINDOWS = (2, 4, 8, 16)
POOL_GROUP = D_MODEL // len(POOL_WINDOWS)
N_HEADS = 16
HEAD_DIM = 64
MLA_Q_RANK = 384
MLA_KV_RANK = 256
MLA_NOPE = 64
MLA_ROPE = 32
MLA_V = 64
ROPE_THETA = 10000.0

LANES = 128
PAD_FRONT = 256
ROW0 = PAD_FRONT - N_META
HALO = max(POOL_WINDOWS)
MASK_VALUE = -1e30
LOG2E = 1.4426950408889634
DEAD_LOG_WEIGHT = -104.0
VMEM_LIMIT = 56 * 1024 * 1024

ROW_TILE = 1024
ATT_TILE = 256
FF_TILE = 256


def _divisor_tile(n, target):
    best = None
    for t in range(8, min(n, target) + 1, 8):
        if n % t == 0:
            best = t
    assert best is not None, (n, target)
    return best


def _params(*sem):
    return pltpu.CompilerParams(dimension_semantics=sem, vmem_limit_bytes=VMEM_LIMIT)


def _rmsnorm(x, g):
    return x * lax.rsqrt(jnp.mean(x * x, axis=-1, keepdims=True) + EPS) * g


def _log_sigmoid_neg(z):
    return -(jnp.maximum(z, 0.0) + jnp.log1p(jnp.exp(-jnp.abs(z))))


def _dot(a, b):
    return jnp.dot(a, b, preferred_element_type=F32)


def _dot_nt(a, b):
    return lax.dot_general(a, b, (((1,), (1,)), ((), ())), preferred_element_type=F32)


def _pool_kernel(h_ref, g_ref, w_ref, sc_ref, o_ref, abuf, *, tile):
    t = pl.program_id(1)
    h = h_ref[0]
    row = t * tile + lax.broadcasted_iota(jnp.int32, (tile, 1), 0)
    a = jnp.where(row >= ROW0, _rmsnorm(h, g_ref[...]), 0.0)

    @pl.when(t == 0)
    def _():
        abuf[0:HALO, :] = jnp.zeros((HALO, D_MODEL), F32)

    @pl.when(t > 0)
    def _():
        abuf[0:HALO, :] = abuf[tile:tile + HALO, :]

    abuf[HALO:HALO + tile, :] = a
    pos1 = row - ROW0 + 1
    outs = []
    for g, win in enumerate(POOL_WINDOWS):
        cols = slice(g * POOL_GROUP, (g + 1) * POOL_GROUP)
        xg = a[:, cols]
        acc = xg
        for j in range(1, win):
            acc = acc + abuf[HALO - j:HALO - j + tile, cols]
        cnt = jnp.clip(pos1, 1, win).astype(F32)
        pooled = acc / cnt - xg
        outs.append(_dot(pooled.astype(BF16), w_ref[g]))
    mixed = jnp.concatenate(outs, axis=1) * sc_ref[...]
    o_ref[0] = h + mixed


def _pool_layer(h, g, w, sc):
    B, Lp, D = h.shape
    tile = ATT_TILE
    kern = functools.partial(_pool_kernel, tile=tile)
    return pl.pallas_call(
        kern,
        grid=(B, Lp // tile),
        in_specs=[
            pl.BlockSpec((1, tile, D), lambda b, t: (b, t, 0)),
            pl.BlockSpec((1, D), lambda b, t: (0, 0)),
            pl.BlockSpec((len(POOL_WINDOWS), POOL_GROUP, POOL_GROUP), lambda b, t: (0, 0, 0)),
            pl.BlockSpec((1, D), lambda b, t: (0, 0)),
        ],
        out_specs=pl.BlockSpec((1, tile, D), lambda b, t: (b, t, 0)),
        out_shape=jax.ShapeDtypeStruct((B, Lp, D), F32),
        scratch_shapes=[pltpu.VMEM((tile + HALO, D), F32)],
        compiler_params=_params("arbitrary", "arbitrary"),
        name="pool_layer",
    )(h, g.reshape(1, D), w.astype(BF16), sc.reshape(1, D))


def _norm_proj_kernel(h_ref, g_ref, w_ref, o_ref, a_scr):
    @pl.when(pl.program_id(1) == 0)
    def _():
        a_scr[...] = _rmsnorm(h_ref[...], g_ref[...]).astype(BF16)

    o_ref[...] = _dot(a_scr[...], w_ref[...]).astype(o_ref.dtype)


def _norm_proj(h2, g, w, tn):
    M, D = h2.shape
    N = w.shape[1]
    tm = _divisor_tile(M, ROW_TILE)
    return pl.pallas_call(
        _norm_proj_kernel,
        grid=(M // tm, N // tn),
        in_specs=[
            pl.BlockSpec((tm, D), lambda i, j: (i, 0)),
            pl.BlockSpec((1, D), lambda i, j: (0, 0)),
            pl.BlockSpec((D, tn), lambda i, j: (0, j)),
        ],
        out_specs=pl.BlockSpec((tm, tn), lambda i, j: (i, j)),
        out_shape=jax.ShapeDtypeStruct((M, N), BF16),
        scratch_shapes=[pltpu.VMEM((tm, D), BF16)],
        compiler_params=_params("arbitrary", "arbitrary"),
        name="norm_proj",
    )(h2, g.reshape(1, D), w)


def _fox_proj_kernel(h_ref, g_ref, w_ref, wvt_ref, wf_ref, o_ref, vt_ref, f_ref, a_scr):
    @pl.when(pl.program_id(1) == 0)
    def _():
        a = _rmsnorm(h_ref[...], g_ref[...]).astype(BF16)
        a_scr[...] = a
        f_ref[...] = _dot(a, wf_ref[...])
        vt_ref[...] = _dot_nt(wvt_ref[...], a).astype(vt_ref.dtype)

    o_ref[...] = _dot(a_scr[...], w_ref[...]).astype(o_ref.dtype)


def _fox_proj(h2, g, w_qk, w_vt, wf, tn):
    M, D = h2.shape
    N = w_qk.shape[1]
    Dv = w_vt.shape[0]
    tm = _divisor_tile(M, ROW_TILE)
    return pl.pallas_call(
        _fox_proj_kernel,
        grid=(M // tm, N // tn),
        in_specs=[
            pl.BlockSpec((tm, D), lambda i, j: (i, 0)),
            pl.BlockSpec((1, D), lambda i, j: (0, 0)),
            pl.BlockSpec((D, tn), lambda i, j: (0, j)),
            pl.BlockSpec((Dv, D), lambda i, j: (0, 0)),
            pl.BlockSpec((D, LANES), lambda i, j: (0, 0)),
        ],
        out_specs=[
            pl.BlockSpec((tm, tn), lambda i, j: (i, j)),
            pl.BlockSpec((Dv, tm), lambda i, j: (0, i)),
            pl.BlockSpec((tm, LANES), lambda i, j: (i, 0)),
        ],
        out_shape=[
            jax.ShapeDtypeStruct((M, N), BF16),
            jax.ShapeDtypeStruct((Dv, M), BF16),
            jax.ShapeDtypeStruct((M, LANES), F32),
        ],
        scratch_shapes=[pltpu.VMEM((tm, D), BF16)],
        compiler_params=_params("arbitrary", "arbitrary"),
        name="fox_proj",
    )(h2, g.reshape(1, D), w_qk, w_vt, wf)


def _proj_res_kernel(o_ref, w_ref, h_ref, out_ref):
    out_ref[...] = h_ref[...] + _dot(o_ref[...], w_ref[...])


def _proj_residual(o2, w, h2):
    M, D = h2.shape
    K = o2.shape[1]
    tm = _divisor_tile(M, ROW_TILE)
    return pl.pallas_call(
        _proj_res_kernel,
        grid=(M // tm,),
        in_specs=[
            pl.BlockSpec((tm, K), lambda i: (i, 0)),
            pl.BlockSpec((K, D), lambda i: (0, 0)),
            pl.BlockSpec((tm, D), lambda i: (i, 0)),
        ],
        out_specs=pl.BlockSpec((tm, D), lambda i: (i, 0)),
        out_shape=jax.ShapeDtypeStruct((M, D), F32),
        compiler_params=_params("arbitrary"),
        name="proj_residual",
    )(o2, w, h2)


def _ffn_kernel(h_ref, g_ref, wg_ref, wu_ref, wd_ref, o_ref, a_scr, acc_scr):
    j = pl.program_id(1)

    @pl.when(j == 0)
    def _():
        a_scr[...] = _rmsnorm(h_ref[...], g_ref[...]).astype(BF16)
        acc_scr[...] = jnp.zeros_like(acc_scr)

    a = a_scr[...]
    gate = _dot(a, wg_ref[...])
    up = _dot(a, wu_ref[...])
    act = (gate * jax.nn.sigmoid(gate) * up).astype(BF16)
    acc_scr[...] += _dot(act, wd_ref[...])

    @pl.when(j == pl.num_programs(1) - 1)
    def _():
        o_ref[...] = h_ref[...] + acc_scr[...]


def _ffn(h2, g, wg, wu, wd):
    M, D = h2.shape
    F = wg.shape[1]
    tm = _divisor_tile(M, ROW_TILE)
    tf = FF_TILE
    return pl.pallas_call(
        _ffn_kernel,
        grid=(M // tm, F // tf),
        in_specs=[
            pl.BlockSpec((tm, D), lambda i, j: (i, 0)),
            pl.BlockSpec((1, D), lambda i, j: (0, 0)),
            pl.BlockSpec((D, tf), lambda i, j: (0, j)),
            pl.BlockSpec((D, tf), lambda i, j: (0, j)),
            pl.BlockSpec((tf, D), lambda i, j: (j, 0)),
        ],
        out_specs=pl.BlockSpec((tm, D), lambda i, j: (i, 0)),
        out_shape=jax.ShapeDtypeStruct((M, D), F32),
        scratch_shapes=[pltpu.VMEM((tm, D), BF16), pltpu.VMEM((tm, D), F32)],
        compiler_params=_params("arbitrary", "arbitrary"),
        name="ffn",
    )(h2, g.reshape(1, D), wg, wu, wd)


def _split_heads(x, width):
    if width == 2 * LANES:
        return x[:, :LANES], x[:, LANES:]
    lane = lax.broadcasted_iota(jnp.int32, (1, LANES), 1)
    zero = jnp.zeros_like(x)
    return jnp.where(lane < HEAD_DIM, x, zero), jnp.where(lane >= HEAD_DIM, x, zero)


def _split_keys(x, width):
    if width == 2 * LANES:
        return x[:, :LANES], x[:, LANES:]
    return x, x


def _pair_values(v):
    lane = lax.broadcasted_iota(jnp.int32, (1, LANES), 1)
    zero = jnp.zeros_like(v)
    return jnp.concatenate(
        [jnp.where(lane < HEAD_DIM, v, zero), jnp.where(lane >= HEAD_DIM, v, zero)], axis=0)


def _pair_lanes(x0, x1):
    lane = lax.broadcasted_iota(jnp.int32, (1, LANES), 1)
    return jnp.where(lane < HEAD_DIM, x0, x1)


def _pair_values_t(vt):
    sub = lax.broadcasted_iota(jnp.int32, (LANES, 1), 0)
    zero = jnp.zeros_like(vt)
    return jnp.concatenate(
        [jnp.where(sub < HEAD_DIM, vt, zero), jnp.where(sub >= HEAD_DIM, vt, zero)], axis=1)


def _pair_rows(x0, x1):
    sub = lax.broadcasted_iota(jnp.int32, (LANES, 1), 0)
    return jnp.where(sub < HEAD_DIM, x0, x1)


def _softmax_attn_kernel(*refs, tile, width, scale, decay):
    if decay:
        q_ref, k_ref, vt_ref, fcol_ref, frow_ref, o_ref = refs[:6]
    else:
        q_ref, k_ref, vt_ref, o_ref = refs[:4]
    m_scr, l_scr, acc_scr, raw0, raw1, p0, p1, al0, al1 = refs[-9:]
    raw_s, p_s, al_s = (raw0, raw1), (p0, p1), (al0, al1)
    hp = pl.program_id(1)
    qi = pl.program_id(2)
    qs = _split_heads(q_ref[0], width)
    qcol = qi * tile + lax.broadcasted_iota(jnp.int32, (1, tile), 1)
    if decay:
        lane = lax.broadcasted_iota(jnp.int32, (1, LANES), 1)
        fq = tuple(frow_ref[0, pl.ds(2 * hp + e, 1), pl.ds(pl.multiple_of(qi * tile, tile), tile)] * LOG2E
                   for e in range(2))

    m_scr[...] = jnp.full_like(m_scr, MASK_VALUE)
    l_scr[...] = jnp.zeros_like(l_scr)
    acc_scr[...] = jnp.zeros_like(acc_scr)

    def scores(t, s):
        k = k_ref[0, pl.ds(pl.multiple_of(t * tile, tile), tile), :]
        ks = _split_keys(k, width)
        for e in range(2):
            raw_s[s][e] = _dot_nt(ks[e], qs[e])

    def weights(t, s, masked):
        start = pl.multiple_of(t * tile, tile)
        if masked:
            key = start + lax.broadcasted_iota(jnp.int32, (tile, 1), 0)
            valid = (key >= ROW0) & (key <= qcol)
        if decay:
            fcol = fcol_ref[0, pl.ds(start, tile), :]
        for e in range(2):
            raw = raw_s[s][e]
            m_old = m_scr[e]
            if decay:
                fk = jnp.sum(jnp.where(lane == 2 * hp + e, fcol, 0.0), axis=1, keepdims=True)
                u = raw * (scale * LOG2E) - fk * LOG2E
                if masked:
                    u = jnp.where(valid, u, MASK_VALUE)
                m_new = jnp.maximum(m_old, jnp.max(u, axis=0, keepdims=True) + fq[e])
                p = jnp.exp2(u - (m_new - fq[e]))
                alpha = jnp.exp2(m_old - m_new)
            else:
                if masked:
                    raw = jnp.where(valid, raw, MASK_VALUE)
                m_new = jnp.maximum(m_old, jnp.max(raw, axis=0, keepdims=True))
                p = jnp.exp2((raw - m_new) * (scale * LOG2E))
                alpha = jnp.exp2((m_old - m_new) * (scale * LOG2E))
            l_scr[e] = alpha * l_scr[e] + jnp.sum(p, axis=0, keepdims=True)
            m_scr[e] = m_new
            p_s[s][e * tile:(e + 1) * tile, :] = p.astype(BF16)
            al_s[s][e] = alpha

    def accumulate(t, s):
        vt = vt_ref[:, pl.ds(pl.multiple_of(t * tile, tile), tile)]
        pv = _dot(_pair_values_t(vt), p_s[s][...])
        acc_scr[...] = _pair_rows(al_s[s][0], al_s[s][1]) * acc_scr[...] + pv

    scores(0, 0)
    weights(0, 0, True)

    @pl.when(qi == 0)
    def _():
        accumulate(0, 0)

    @pl.when(qi > 0)
    def _():
        scores(1, 1)
        n_steady = qi - 1

        def pair(i, c):
            t = 2 * i + 1
            scores(t + 1, 0)
            weights(t, 1, False)
            accumulate(t - 1, 0)
            scores(t + 2, 1)
            weights(t + 1, 0, False)
            accumulate(t, 1)
            return c

        lax.fori_loop(0, n_steady // 2, pair, 0)

        @pl.when(n_steady % 2 == 1)
        def _():
            t = qi - 1
            scores(qi, 0)
            weights(t, 1, False)
            accumulate(t - 1, 0)
            weights(qi, 0, True)
            accumulate(t, 1)
            accumulate(qi, 0)

        @pl.when(n_steady % 2 == 0)
        def _():
            weights(qi, 1, True)
            accumulate(qi - 1, 0)
            accumulate(qi, 1)

    o_ref[0] = (acc_scr[...] / _pair_rows(l_scr[0], l_scr[1])).T.astype(o_ref.dtype)


def _softmax_attention(q, k, vt, *, width, scale, n_pairs, offs=(0, 0), fcol=None, frow=None):
    B, Lp, _ = q.shape
    qo, ko = offs
    tile = ATT_TILE
    decay = fcol is not None
    kern = functools.partial(_softmax_attn_kernel, tile=tile, width=width, scale=scale, decay=decay)
    in_specs = [
        pl.BlockSpec((1, tile, width), lambda b, h, i: (b, i, qo + h)),
        pl.BlockSpec((1, Lp, width), lambda b, h, i: (b, 0, ko + h)),
        pl.BlockSpec((LANES, Lp), lambda b, h, i: (h, b)),
    ]
    args = [q, k, vt]
    if decay:
        in_specs += [
            pl.BlockSpec((1, Lp, LANES), lambda b, h, i: (b, 0, 0)),
            pl.BlockSpec((1, N_HEADS, Lp), lambda b, h, i: (b, 0, 0)),
        ]
        args += [fcol, frow]
    return pl.pallas_call(
        kern,
        grid=(B, n_pairs, Lp // tile),
        in_specs=in_specs,
        out_specs=pl.BlockSpec((1, tile, LANES), lambda b, h, i: (b, i, h)),
        out_shape=jax.ShapeDtypeStruct((B, Lp, n_pairs * LANES), BF16),
        scratch_shapes=[
            pltpu.VMEM((2, 1, tile), F32),
            pltpu.VMEM((2, 1, tile), F32),
            pltpu.VMEM((LANES, tile), F32),
            pltpu.VMEM((2, tile, tile), F32),
            pltpu.VMEM((2, tile, tile), F32),
            pltpu.VMEM((2 * tile, tile), BF16),
            pltpu.VMEM((2 * tile, tile), BF16),
            pltpu.VMEM((2, 1, tile), F32),
            pltpu.VMEM((2, 1, tile), F32),
        ],
        compiler_params=_params("arbitrary", "arbitrary", "arbitrary"),
        name="fox_attention" if decay else "mla_attention",
    )(*args)


def _sb_attn_kernel(q_ref, k_ref, v_ref, tri_ref, o_ref, r_scr, acc_scr, *, tile, scale):
    qi = pl.program_id(2)
    q0, q1 = _split_heads(q_ref[0], LANES)
    qs = (q0, q1)
    row = qi * tile + lax.broadcasted_iota(jnp.int32, (tile, 1), 0)
    r_scr[...] = jnp.zeros_like(r_scr)
    acc_scr[...] = jnp.zeros_like(acc_scr)
    tri2 = tri_ref[...]

    def block(j, masked):
        start = pl.multiple_of(j * tile, tile)
        k = k_ref[0, pl.ds(start, tile), :]
        v = v_ref[0, pl.ds(start, tile), :]
        if masked:
            col = start + lax.broadcasted_iota(jnp.int32, (1, tile), 1)
            valid = (col >= ROW0) & (col < row)
        ws = []
        for e in range(2):
            z = _dot_nt(qs[e], k) * scale
            lk = _log_sigmoid_neg(z)
            if masked:
                lk = jnp.where(valid, lk, 0.0)
            lk_hi = lk.astype(BF16)
            lk_lo = (lk - lk_hi.astype(F32)).astype(BF16)
            later = _dot(jnp.concatenate([lk_hi, lk_lo], axis=1), tri2) + r_scr[e]
            w = jnp.exp(lk + z + later)
            if masked:
                w = jnp.where(valid, w, 0.0)
            r_scr[e] = r_scr[e] + jnp.sum(lk, axis=1, keepdims=True)
            ws.append(w.astype(BF16))
        acc_scr[...] += _dot(jnp.concatenate(ws, axis=1), _pair_values(v))

    def live():
        return (jnp.max(jnp.maximum(r_scr[0], r_scr[1])) > DEAD_LOG_WEIGHT).astype(jnp.int32)

    block(qi, True)

    def body(c):
        block(c[0], False)
        return c[0] - 1, live()

    _, alive = lax.while_loop(lambda c: (c[0] >= 1) & (c[1] > 0), body, (qi - 1, live()))

    @pl.when((qi > 0) & (alive > 0))
    def _():
        block(0, True)

    o_ref[0] = acc_scr[...].astype(o_ref.dtype)


def _sb_attention(qkv, *, scale, n_pairs):
    B, Lp, _ = qkv.shape
    tile = ATT_TILE
    r = lax.broadcasted_iota(jnp.int32, (tile, tile), 0)
    c = lax.broadcasted_iota(jnp.int32, (tile, tile), 1)
    tri = (r > c).astype(BF16)
    tri2 = jnp.concatenate([tri, tri], axis=0)
    kern = functools.partial(_sb_attn_kernel, tile=tile, scale=scale)
    return pl.pallas_call(
        kern,
        grid=(B, n_pairs, Lp // tile),
        in_specs=[
            pl.BlockSpec((1, tile, LANES), lambda b, h, i: (b, i, h)),
            pl.BlockSpec((1, Lp, LANES), lambda b, h, i: (b, 0, n_pairs + h)),
            pl.BlockSpec((1, Lp, LANES), lambda b, h, i: (b, 0, 2 * n_pairs + h)),
            pl.BlockSpec((2 * tile, tile), lambda b, h, i: (0, 0)),
        ],
        out_specs=pl.BlockSpec((1, tile, LANES), lambda b, h, i: (b, i, h)),
        out_shape=jax.ShapeDtypeStruct((B, Lp, n_pairs * LANES), BF16),
        scratch_shapes=[
            pltpu.VMEM((2, tile, 1), F32),
            pltpu.VMEM((tile, LANES), F32),
        ],
        compiler_params=_params("arbitrary", "arbitrary", "arbitrary"),
        name="sb_attention",
    )(qkv, qkv, qkv, tri2)


def _mla_proj_kernel(h_ref, g_ref, wd_ref, qn_ref, kvn_ref, wq_ref, wk_ref, wvt_ref, inv_ref,
                     q_ref, k_ref, vt_ref, *, tm, rows_per_batch):
    i = pl.program_id(0)
    a = _rmsnorm(h_ref[...], g_ref[...]).astype(BF16)
    down = _dot(a, wd_ref[...])
    cq = _rmsnorm(down[:, :MLA_Q_RANK], qn_ref[...]).astype(BF16)
    ckv = _rmsnorm(down[:, MLA_Q_RANK:MLA_Q_RANK + MLA_KV_RANK], kvn_ref[...]).astype(BF16)
    kr = down[:, MLA_Q_RANK + MLA_KV_RANK:]

    row = (i * tm) % rows_per_batch + lax.broadcasted_iota(jnp.int32, (tm, 1), 0)
    pos = (row - ROW0).astype(F32)
    ang = pos * inv_ref[...]
    lane = lax.broadcasted_iota(jnp.int32, (1, LANES), 1)
    half = MLA_ROPE // 2
    in_lo = (lane >= MLA_NOPE) & (lane < MLA_NOPE + half)
    in_hi = (lane >= MLA_NOPE + half) & (lane < MLA_NOPE + MLA_ROPE)
    cos = jnp.cos(ang)
    sin = jnp.sin(ang)
    c_tab = jnp.where(lane < MLA_NOPE, 1.0, jnp.where(in_lo | in_hi, cos, 0.0))
    s_lo = jnp.where(in_lo, -sin, 0.0)
    s_hi = jnp.where(in_hi, sin, 0.0)

    def rope(x):
        return (x * c_tab + pltpu.roll(x, LANES - half, 1) * s_lo + pltpu.roll(x, half, 1) * s_hi)

    q = _dot(cq, wq_ref[...])
    kn = _dot(ckv, wk_ref[...])
    vt_ref[...] = _dot_nt(wvt_ref[...], ckv).astype(BF16)
    kr_rot = rope(kr)
    for hd in range(N_HEADS):
        cols = slice(hd * LANES, (hd + 1) * LANES)
        q_ref[:, cols] = rope(q[:, cols]).astype(BF16)
        k_ref[:, cols] = (kn[:, cols] + kr_rot).astype(BF16)


def _mla_proj(h2, g, wd, qn, kvn, wq, wk, wvt, inv_lane, rows_per_batch):
    M, D = h2.shape
    tm = _divisor_tile(rows_per_batch, 512)
    kern = functools.partial(_mla_proj_kernel, tm=tm, rows_per_batch=rows_per_batch)
    full = lambda a: pl.BlockSpec(a.shape, lambda i: (0,) * a.ndim)
    g2, qn2, kvn2 = g.reshape(1, D), qn.reshape(1, -1), kvn.reshape(1, -1)
    return pl.pallas_call(
        kern,
        grid=(M // tm,),
        in_specs=[pl.BlockSpec((tm, D), lambda i: (i, 0)), full(g2), full(wd), full(qn2), full(kvn2),
                  full(wq), full(wk), full(wvt), full(inv_lane)],
        out_specs=[
            pl.BlockSpec((tm, N_HEADS * LANES), lambda i: (i, 0)),
            pl.BlockSpec((tm, N_HEADS * LANES), lambda i: (i, 0)),
            pl.BlockSpec((N_HEADS * MLA_V, tm), lambda i: (0, i)),
        ],
        out_shape=[
            jax.ShapeDtypeStruct((M, N_HEADS * LANES), BF16),
            jax.ShapeDtypeStruct((M, N_HEADS * LANES), BF16),
            jax.ShapeDtypeStruct((N_HEADS * MLA_V, M), BF16),
        ],
        compiler_params=_params("arbitrary"),
        name="mla_proj",
    )(h2, g2, wd, qn2, kvn2, wq, wk, wvt, inv_lane)


def _forget_cumsum_kernel(f_ref, b_ref, fcol_ref, frow_ref, carry, *, tile):
    t = pl.program_id(1)

    @pl.when(t == 0)
    def _():
        carry[...] = jnp.zeros_like(carry)

    row_in = lax.broadcasted_iota(jnp.int32, (tile, 1), 0)
    row = t * tile + row_in
    x = jnp.where(row >= ROW0, _log_sigmoid_neg(-(f_ref[0] + b_ref[...])), 0.0)
    shift = 1
    while shift < tile:
        x = x + jnp.where(row_in >= shift, pltpu.roll(x, shift, 0), 0.0)
        shift *= 2
    x = x + carry[...]
    carry[...] = x[tile - 1:tile, :]
    fcol_ref[0] = x
    frow_ref[0] = x.T[:N_HEADS, :]


def _forget_cumsum(flog, b_lane):
    B, Lp, _ = flog.shape
    tile = ATT_TILE
    kern = functools.partial(_forget_cumsum_kernel, tile=tile)
    return pl.pallas_call(
        kern,
        grid=(B, Lp // tile),
        in_specs=[
            pl.BlockSpec((1, tile, LANES), lambda b, t: (b, t, 0)),
            pl.BlockSpec((1, LANES), lambda b, t: (0, 0)),
        ],
        out_specs=[
            pl.BlockSpec((1, tile, LANES), lambda b, t: (b, t, 0)),
            pl.BlockSpec((1, N_HEADS, tile), lambda b, t: (b, 0, t)),
        ],
        out_shape=[
            jax.ShapeDtypeStruct((B, Lp, LANES), F32),
            jax.ShapeDtypeStruct((B, N_HEADS, Lp), F32),
        ],
        scratch_shapes=[pltpu.VMEM((1, LANES), F32)],
        compiler_params=_params("arbitrary", "arbitrary"),
        name="forget_cumsum",
    )(flog, b_lane)


def _final_norm_kernel(h_ref, g_ref, o_ref):
    o_ref[0] = _rmsnorm(h_ref[0], g_ref[...])


def _final_norm(h, g, seq):
    B, Lp, D = h.shape
    tile = PAD_FRONT
    skip = PAD_FRONT // tile
    return pl.pallas_call(
        _final_norm_kernel,
        grid=(B, seq // tile),
        in_specs=[
            pl.BlockSpec((1, tile, D), lambda b, t: (b, t + skip, 0)),
            pl.BlockSpec((1, D), lambda b, t: (0, 0)),
        ],
        out_specs=pl.BlockSpec((1, tile, D), lambda b, t: (b, t, 0)),
        out_shape=jax.ShapeDtypeStruct((B, seq, D), F32),
        compiler_params=_params("arbitrary", "arbitrary"),
        name="final_norm",
    )(h, g.reshape(1, D))


def _pad_heads(w, per_head):
    K = w.shape[0]
    w3 = w.reshape(K, N_HEADS, per_head)
    w3 = jnp.pad(w3, ((0, 0), (0, 0), (0, LANES - per_head)))
    return w3.reshape(K, N_HEADS * LANES)


def kernel(x, meta, norm_mix, norm_ffn, pool_w, pool_scale, sb_w_qkv, sb_w_o, mla_w_down, mla_q_norm,
           mla_kv_norm, mla_w_uq, mla_w_ukv, mla_w_o, fox_w_qkvf, fox_b_f, fox_w_o, ffn_w_gate, ffn_w_up,
           ffn_w_down, final_norm):
    B, S, D = x.shape
    assert D == D_MODEL and S % ATT_TILE == 0
    Lp = S + PAD_FRONT
    M = B * Lp
    HD = N_HEADS * HEAD_DIM

    meta_b = jnp.broadcast_to(meta[None].astype(x.dtype), (B, N_META, D))
    h = jnp.concatenate([jnp.zeros((B, ROW0, D), x.dtype), meta_b, x], axis=1)

    def ffn(h, i):
        return _ffn(h.reshape(M, D), norm_ffn[i], ffn_w_gate[i].astype(BF16), ffn_w_up[i].astype(BF16),
                    ffn_w_down[i].astype(BF16)).reshape(B, Lp, D)

    n_pairs = N_HEADS // 2

    def pool_mixer(h, i, j):
        return _pool_layer(h, norm_mix[i], pool_w[j], pool_scale[j])

    def sb_mixer(h, i, j):
        qkv = _norm_proj(h.reshape(M, D), norm_mix[i], sb_w_qkv[j].astype(BF16), tn=HD)
        o = _sb_attention(qkv.reshape(B, Lp, 3 * HD), scale=HEAD_DIM ** -0.5, n_pairs=n_pairs)
        return _proj_residual(o.reshape(M, HD), sb_w_o[j].astype(BF16), h.reshape(M, D)).reshape(B, Lp, D)

    def mla_mixer(h, i, j):
        return _mla_layer(h, norm_mix[i], mla_w_down[j], mla_q_norm[j], mla_kv_norm[j], mla_w_uq[j],
                          mla_w_ukv[j], mla_w_o[j])

    def fox_mixer(h, i, j):
        wf = jnp.pad(fox_w_qkvf[j][:, 3 * HD:], ((0, 0), (0, LANES - N_HEADS))).astype(BF16)
        qk, vt, flog = _fox_proj(h.reshape(M, D), norm_mix[i], fox_w_qkvf[j][:, :2 * HD].astype(BF16),
                                 fox_w_qkvf[j][:, 2 * HD:3 * HD].T.astype(BF16), wf, tn=HD)
        b_lane = jnp.pad(fox_b_f[j].astype(F32), (0, LANES - N_HEADS)).reshape(1, LANES)
        fcol, frow = _forget_cumsum(flog.reshape(B, Lp, LANES), b_lane)
        qk = qk.reshape(B, Lp, 2 * HD)
        o = _softmax_attention(qk, qk, vt, width=LANES, scale=HEAD_DIM ** -0.5, n_pairs=n_pairs,
                               offs=(0, n_pairs), fcol=fcol, frow=frow)
        return _proj_residual(o.reshape(M, HD), fox_w_o[j].astype(BF16), h.reshape(M, D)).reshape(B, Lp, D)

    mixers = (pool_mixer, sb_mixer, mla_mixer, fox_mixer)
    for i in range(norm_mix.shape[0]):
        h = mixers[i % len(mixers)](h, i, i // len(mixers))
        h = ffn(h, i)
    return _final_norm(h, final_norm, S)


def _mla_layer(h, g, w_down, q_norm, kv_norm, w_uq, w_ukv, w_o):
    B, Lp, D = h.shape
    M = B * Lp
    n_lat = MLA_Q_RANK + MLA_KV_RANK
    wd_rope = jnp.pad(w_down[:, n_lat:], ((0, 0), (MLA_NOPE, LANES - MLA_NOPE - MLA_ROPE)))
    wd_p = jnp.concatenate([w_down[:, :n_lat], wd_rope], axis=1).astype(BF16)
    wq_p = _pad_heads(w_uq, MLA_NOPE + MLA_ROPE).astype(BF16)
    wkv3 = w_ukv.reshape(MLA_KV_RANK, N_HEADS, MLA_NOPE + MLA_V)
    wk_p = _pad_heads(wkv3[:, :, :MLA_NOPE].reshape(MLA_KV_RANK, -1), MLA_NOPE).astype(BF16)
    wvt_p = wkv3[:, :, MLA_NOPE:].reshape(MLA_KV_RANK, N_HEADS * MLA_V).T.astype(BF16)
    inv = ROPE_THETA ** (-jnp.arange(0, MLA_ROPE, 2, dtype=F32) / MLA_ROPE)
    inv_lane = jnp.concatenate([jnp.zeros((MLA_NOPE,), F32), inv, inv,
                                jnp.zeros((LANES - MLA_NOPE - MLA_ROPE,), F32)]).reshape(1, LANES)
    q, k, vt = _mla_proj(h.reshape(M, D), g, wd_p, q_norm, kv_norm, wq_p, wk_p, wvt_p, inv_lane, Lp)
    o = _softmax_attention(q.reshape(B, Lp, -1), k.reshape(B, Lp, -1), vt,
                           width=2 * LANES, scale=(MLA_NOPE + MLA_ROPE) ** -0.5, n_pairs=N_HEADS // 2)
    return _proj_residual(o.reshape(M, -1), w_o.astype(BF16), h.reshape(M, D)).reshape(B, Lp, D)
```

```python
import functools

import jax
import jax.numpy as jnp
from jax import lax
from jax.experimental import pallas as pl
from jax.experimental.pallas import tpu as pltpu

F32 = jnp.float32
BF16 = jnp.bfloat16

D_MODEL = 1024
N_META = 16
EPS = 1e-6
POOL_WINDOWS = (2, 4, 8, 16)
POOL_GROUP = D_MODEL // len(POOL_WINDOWS)
N_HEADS = 16
HEAD_DIM = 64
MLA_Q_RANK = 384
MLA_KV_RANK = 256
MLA_NOPE = 64
MLA_ROPE = 32
MLA_V = 64
ROPE_THETA = 10000.0

LANES = 128
PAD_FRONT = 256
ROW0 = PAD_FRONT - N_META
HALO = max(POOL_WINDOWS)
MASK_VALUE = -1e30
LOG2E = 1.4426950408889634
ACC_ROWS = LANES + 16
DEAD_LOG_WEIGHT = -104.0
VMEM_LIMIT = 56 * 1024 * 1024

ROW_TILE = 1024
ATT_TILE = 256
ATT_GROUP = 4
SB_TILE = 128
SB_GROUP = 4
FF_TILE = 256


def _divisor_tile(n, target):
    best = None
    for t in range(8, min(n, target) + 1, 8):
        if n % t == 0:
            best = t
    assert best is not None, (n, target)
    return best


def _params(*sem):
    return pltpu.CompilerParams(dimension_semantics=sem, vmem_limit_bytes=VMEM_LIMIT)


def _rmsnorm(x, g):
    return x * lax.rsqrt(jnp.mean(x * x, axis=-1, keepdims=True) + EPS) * g


def _log_sigmoid_neg(z):
    return -(jnp.maximum(z, 0.0) + jnp.log(1.0 + jnp.exp(-jnp.abs(z))))


def _dot(a, b):
    return jnp.dot(a, b, preferred_element_type=F32)


def _dot_nt(a, b):
    return lax.dot_general(a, b, (((1,), (1,)), ((), ())), preferred_element_type=F32)


def _pool_kernel(h_ref, g_ref, w_ref, sc_ref, o_ref, abuf, *, tile):
    t = pl.program_id(1)
    h = h_ref[0]
    row = t * tile + lax.broadcasted_iota(jnp.int32, (tile, 1), 0)
    a = jnp.where(row >= ROW0, _rmsnorm(h, g_ref[...]), 0.0)

    @pl.when(t == 0)
    def _():
        abuf[0:HALO, :] = jnp.zeros((HALO, D_MODEL), F32)

    @pl.when(t > 0)
    def _():
        abuf[0:HALO, :] = abuf[tile:tile + HALO, :]

    abuf[HALO:HALO + tile, :] = a
    pos1 = row - ROW0 + 1
    outs = []
    for g, win in enumerate(POOL_WINDOWS):
        cols = slice(g * POOL_GROUP, (g + 1) * POOL_GROUP)
        xg = a[:, cols]
        acc = xg
        for j in range(1, win):
            acc = acc + abuf[HALO - j:HALO - j + tile, cols]
        cnt = jnp.clip(pos1, 1, win).astype(F32)
        pooled = acc / cnt - xg
        outs.append(_dot(pooled.astype(BF16), w_ref[g]))
    mixed = jnp.concatenate(outs, axis=1) * sc_ref[...]
    o_ref[0] = h + mixed


def _pool_layer(h, g, w, sc):
    B, Lp, D = h.shape
    tile = ATT_TILE
    kern = functools.partial(_pool_kernel, tile=tile)
    return pl.pallas_call(
        kern,
        grid=(B, Lp // tile),
        in_specs=[
            pl.BlockSpec((1, tile, D), lambda b, t: (b, t, 0)),
            pl.BlockSpec((1, D), lambda b, t: (0, 0)),
            pl.BlockSpec((len(POOL_WINDOWS), POOL_GROUP, POOL_GROUP), lambda b, t: (0, 0, 0)),
            pl.BlockSpec((1, D), lambda b, t: (0, 0)),
        ],
        out_specs=pl.BlockSpec((1, tile, D), lambda b, t: (b, t, 0)),
        out_shape=jax.ShapeDtypeStruct((B, Lp, D), F32),
        scratch_shapes=[pltpu.VMEM((tile + HALO, D), F32)],
        compiler_params=_params("arbitrary", "arbitrary"),
        name="pool_layer",
    )(h, g.reshape(1, D), w.astype(BF16), sc.reshape(1, D))


def _qk_vt_proj_kernel(*refs, q_scale, has_gate):
    if has_gate:
        h_ref, g_ref, w_ref, wvt_ref, wf_ref, o_ref, vt_ref, f_ref, a_scr = refs
    else:
        h_ref, g_ref, w_ref, wvt_ref, o_ref, vt_ref, a_scr = refs

    @pl.when(pl.program_id(1) == 0)
    def _():
        a = _rmsnorm(h_ref[...], g_ref[...]).astype(BF16)
        a_scr[...] = a
        vt_ref[...] = _dot_nt(wvt_ref[...], a).astype(vt_ref.dtype)
        if has_gate:
            f_ref[...] = _dot(a, wf_ref[...])

    col_scale = jnp.where(pl.program_id(1) == 0, q_scale, 1.0)
    o_ref[...] = (_dot(a_scr[...], w_ref[...]) * col_scale).astype(o_ref.dtype)


def _qk_vt_proj(h2, g, w_qk, w_vt, wf=None, *, q_scale, tn):
    M, D = h2.shape
    N = w_qk.shape[1]
    Dv = w_vt.shape[0]
    tm = _divisor_tile(M, ROW_TILE)
    has_gate = wf is not None
    in_specs = [
        pl.BlockSpec((tm, D), lambda i, j: (i, 0)),
        pl.BlockSpec((1, D), lambda i, j: (0, 0)),
        pl.BlockSpec((D, tn), lambda i, j: (0, j)),
        pl.BlockSpec((Dv, D), lambda i, j: (0, 0)),
    ]
    out_specs = [
        pl.BlockSpec((tm, tn), lambda i, j: (i, j)),
        pl.BlockSpec((Dv, tm), lambda i, j: (0, i)),
    ]
    out_shape = [jax.ShapeDtypeStruct((M, N), BF16), jax.ShapeDtypeStruct((Dv, M), BF16)]
    args = [h2, g.reshape(1, D), w_qk, w_vt]
    if has_gate:
        in_specs.append(pl.BlockSpec((D, LANES), lambda i, j: (0, 0)))
        out_specs.append(pl.BlockSpec((tm, LANES), lambda i, j: (i, 0)))
        out_shape.append(jax.ShapeDtypeStruct((M, LANES), F32))
        args.append(wf)
    return pl.pallas_call(
        functools.partial(_qk_vt_proj_kernel, q_scale=q_scale, has_gate=has_gate),
        grid=(M // tm, N // tn),
        in_specs=in_specs,
        out_specs=out_specs,
        out_shape=out_shape,
        scratch_shapes=[pltpu.VMEM((tm, D), BF16)],
        compiler_params=_params("arbitrary", "arbitrary"),
        name="qk_vt_proj",
    )(*args)


def _proj_res_kernel(o_ref, w_ref, h_ref, out_ref):
    out_ref[...] = h_ref[...] + _dot(o_ref[...], w_ref[...])


def _proj_residual(o2, w, h2):
    M, D = h2.shape
    K = o2.shape[1]
    tm = _divisor_tile(M, ROW_TILE)
    return pl.pallas_call(
        _proj_res_kernel,
        grid=(M // tm,),
        in_specs=[
            pl.BlockSpec((tm, K), lambda i: (i, 0)),
            pl.BlockSpec((K, D), lambda i: (0, 0)),
            pl.BlockSpec((tm, D), lambda i: (i, 0)),
        ],
        out_specs=pl.BlockSpec((tm, D), lambda i: (i, 0)),
        out_shape=jax.ShapeDtypeStruct((M, D), F32),
        compiler_params=_params("arbitrary"),
        name="proj_residual",
    )(o2, w, h2)


def _ffn_kernel(h_ref, g_ref, wg_ref, wu_ref, wd_ref, o_ref, acc_scr, *, tf):
    h = h_ref[...]
    a = _rmsnorm(h, g_ref[...]).astype(BF16)
    for c in range(wg_ref.shape[1] // tf):
        cols = slice(c * tf, (c + 1) * tf)
        gate = _dot(a, wg_ref[:, cols])
        up = _dot(a, wu_ref[:, cols])
        act = (gate * jax.nn.sigmoid(gate) * up).astype(BF16)
        part = _dot(act, wd_ref[cols, :])
        if c == 0:
            acc_scr[...] = part
        else:
            acc_scr[...] += part
    o_ref[...] = h + acc_scr[...]


def _ffn(h2, g, wg, wu, wd):
    M, D = h2.shape
    F = wg.shape[1]
    assert F % FF_TILE == 0
    tm = _divisor_tile(M, ROW_TILE)
    resident = lambda shape: pl.BlockSpec(shape, lambda i: (0, 0), pipeline_mode=pl.Buffered(1))
    return pl.pallas_call(
        functools.partial(_ffn_kernel, tf=FF_TILE),
        grid=(M // tm,),
        in_specs=[
            pl.BlockSpec((tm, D), lambda i: (i, 0)),
            resident((1, D)),
            resident((D, F)),
            resident((D, F)),
            resident((F, D)),
        ],
        out_specs=pl.BlockSpec((tm, D), lambda i: (i, 0)),
        out_shape=jax.ShapeDtypeStruct((M, D), F32),
        scratch_shapes=[pltpu.VMEM((tm, D), F32)],
        compiler_params=_params("arbitrary"),
        name="ffn",
    )(h2, g.reshape(1, D), wg, wu, wd)


def _split_heads(x, width):
    if width == 2 * LANES:
        return x[:, :LANES], x[:, LANES:]
    lane = lax.broadcasted_iota(jnp.int32, (1, LANES), 1)
    zero = jnp.zeros_like(x)
    return jnp.where(lane < HEAD_DIM, x, zero), jnp.where(lane >= HEAD_DIM, x, zero)


def _split_keys(x, width):
    if width == 2 * LANES:
        return x[:, :LANES], x[:, LANES:]
    return x, x


def _pair_values_t(vt):
    sub = lax.broadcasted_iota(jnp.int32, (LANES, 1), 0)
    zero = jnp.zeros_like(vt)
    return jnp.concatenate(
        [jnp.where(sub < HEAD_DIM, vt, zero), jnp.where(sub >= HEAD_DIM, vt, zero)], axis=1)


def _pair_rows(x0, x1):
    sub = lax.broadcasted_iota(jnp.int32, (ACC_ROWS, 1), 0)
    return jnp.where((sub < HEAD_DIM) | (sub == LANES), x0, x1)


def _sum_rows(tile):
    sub = lax.broadcasted_iota(jnp.int32, (ACC_ROWS - LANES, 2 * tile), 0)
    col = lax.broadcasted_iota(jnp.int32, (ACC_ROWS - LANES, 2 * tile), 1)
    return jnp.where(sub == col // tile, 1.0, 0.0).astype(BF16)


def _softmax_attn_kernel(*refs, tile, width, decay, group):
    if decay:
        q_ref, k_ref, vt_ref, fcol_ref, frow_ref, o_ref = refs[:6]
    else:
        q_ref, k_ref, vt_ref, o_ref = refs[:4]
    m_scr, acc_scr, raw0, raw1, p0, p1, al0, al1 = refs[-8:]
    raw_s, p_s, al_s = (raw0, raw1), (p0, p1), (al0, al1)
    hg = pl.program_id(1)
    qi = pl.program_id(2)
    n_heads = 2 * group
    q = q_ref[0]
    qs = []
    for g in range(group):
        qs += list(_split_heads(q[:, g * width:(g + 1) * width], width))
    qcol = qi * tile + lax.broadcasted_iota(jnp.int32, (1, tile), 1)
    if decay:
        lane = lax.broadcasted_iota(jnp.int32, (1, LANES), 1)
        head0 = n_heads * hg
        fq = tuple(frow_ref[0, pl.ds(head0 + h, 1), pl.ds(pl.multiple_of(qi * tile, tile), tile)] * LOG2E
                   for h in range(n_heads))

    m_scr[...] = jnp.full_like(m_scr, MASK_VALUE)
    acc_scr[...] = jnp.zeros_like(acc_scr)
    sum_rows = _sum_rows(tile)

    def scores(t, s):
        k = k_ref[0, pl.ds(pl.multiple_of(t * tile, tile), tile), :]
        for g in range(group):
            ks = _split_keys(k[:, g * width:(g + 1) * width], width)
            for e in range(2):
                raw_s[s][2 * g + e] = _dot_nt(ks[e], qs[2 * g + e])

    def weights(t, s, masked):
        start = pl.multiple_of(t * tile, tile)
        if masked:
            key = start + lax.broadcasted_iota(jnp.int32, (tile, 1), 0)
            valid = (key >= ROW0) & (key <= qcol)
        if decay:
            fcol = fcol_ref[0, pl.ds(start, tile), :]
        for h in range(n_heads):
            raw = raw_s[s][h]
            m_old = m_scr[h]
            if decay:
                fk = jnp.sum(jnp.where(lane == head0 + h, fcol, 0.0), axis=1, keepdims=True)
                u = raw - fk * LOG2E
                if masked:
                    u = jnp.where(valid, u, MASK_VALUE)
                m_new = jnp.maximum(m_old, jnp.max(u, axis=0, keepdims=True) + fq[h])
                p = jnp.exp2(u - (m_new - fq[h]))
            else:
                if masked:
                    raw = jnp.where(valid, raw, MASK_VALUE)
                m_new = jnp.maximum(m_old, jnp.max(raw, axis=0, keepdims=True))
                p = jnp.exp2(raw - m_new)
            m_scr[h] = m_new
            p_s[s][h * tile:(h + 1) * tile, :] = p.astype(BF16)
            al_s[s][h] = jnp.exp2(m_old - m_new)

    def accumulate(t, s):
        vt = vt_ref[:, pl.ds(pl.multiple_of(t * tile, tile), tile)]
        for g in range(group):
            lhs = jnp.concatenate([_pair_values_t(vt[g * LANES:(g + 1) * LANES]), sum_rows], axis=0)
            pv = _dot(lhs, p_s[s][2 * g * tile:(2 * g + 2) * tile, :])
            rows = slice(g * ACC_ROWS, (g + 1) * ACC_ROWS)
            acc_scr[rows, :] = _pair_rows(al_s[s][2 * g], al_s[s][2 * g + 1]) * acc_scr[rows, :] + pv

    scores(0, 0)
    weights(0, 0, True)

    @pl.when(qi == 0)
    def _():
        accumulate(0, 0)

    @pl.when(qi > 0)
    def _():
        scores(1, 1)
        n_steady = qi - 1

        def pair(i, c):
            t = 2 * i + 1
            scores(t + 1, 0)
            weights(t, 1, False)
            accumulate(t - 1, 0)
            scores(t + 2, 1)
            weights(t + 1, 0, False)
            accumulate(t, 1)
            return c

        lax.fori_loop(0, n_steady // 2, pair, 0)

        @pl.when(n_steady % 2 == 1)
        def _():
            t = qi - 1
            scores(qi, 0)
            weights(t, 1, False)
            accumulate(t - 1, 0)
            weights(qi, 0, True)
            accumulate(t, 1)
            accumulate(qi, 0)

        @pl.when(n_steady % 2 == 0)
        def _():
            weights(qi, 1, True)
            accumulate(qi - 1, 0)
            accumulate(qi, 1)

    sub = lax.broadcasted_iota(jnp.int32, (LANES, 1), 0)
    for g in range(group):
        acc = acc_scr[g * ACC_ROWS:(g + 1) * ACC_ROWS, :]
        total = jnp.where(sub < HEAD_DIM, acc[LANES:LANES + 1], acc[LANES + 1:LANES + 2])
        o_ref[0, :, g * LANES:(g + 1) * LANES] = (acc[:LANES] / total).T.astype(o_ref.dtype)


def _softmax_attention(q, k, vt, *, width, n_pairs, offs=(0, 0), fcol=None, frow=None):
    B, Lp, _ = q.shape
    qo, ko = offs
    tile = ATT_TILE
    group = ATT_GROUP
    assert n_pairs % group == 0 and qo % group == 0 and ko % group == 0
    n_heads = 2 * group
    decay = fcol is not None
    kern = functools.partial(_softmax_attn_kernel, tile=tile, width=width, decay=decay, group=group)
    in_specs = [
        pl.BlockSpec((1, tile, group * width), lambda b, h, i: (b, i, qo // group + h)),
        pl.BlockSpec((1, Lp, group * width), lambda b, h, i: (b, 0, ko // group + h), pipeline_mode=pl.Buffered(1)),
        pl.BlockSpec((group * LANES, Lp), lambda b, h, i: (h, b), pipeline_mode=pl.Buffered(1)),
    ]
    args = [q, k, vt]
    if decay:
        in_specs += [
            pl.BlockSpec((1, Lp, LANES), lambda b, h, i: (b, 0, 0)),
            pl.BlockSpec((1, N_HEADS, Lp), lambda b, h, i: (b, 0, 0)),
        ]
        args += [fcol, frow]
    return pl.pallas_call(
        kern,
        grid=(B, n_pairs // group, Lp // tile),
        in_specs=in_specs,
        out_specs=pl.BlockSpec((1, tile, group * LANES), lambda b, h, i: (b, i, h)),
        out_shape=jax.ShapeDtypeStruct((B, Lp, n_pairs * LANES), BF16),
        scratch_shapes=[
            pltpu.VMEM((n_heads, 1, tile), F32),
            pltpu.VMEM((group * ACC_ROWS, tile), F32),
            pltpu.VMEM((n_heads, tile, tile), F32),
            pltpu.VMEM((n_heads, tile, tile), F32),
            pltpu.VMEM((n_heads * tile, tile), BF16),
            pltpu.VMEM((n_heads * tile, tile), BF16),
            pltpu.VMEM((n_heads, 1, tile), F32),
            pltpu.VMEM((n_heads, 1, tile), F32),
        ],
        compiler_params=_params("arbitrary", "arbitrary", "arbitrary"),
        name="fox_attention" if decay else "mla_attention",
    )(*args)


def _sb_attn_kernel(q_ref, k_ref, vt_ref, tri_ref, o_ref, r_scr, acc_scr, *, tile, group):
    qi = pl.program_id(2)
    n_heads = 2 * group
    first = ROW0 // tile
    q = q_ref[0]
    qs = []
    for g in range(group):
        qs += list(_split_heads(q[:, g * LANES:(g + 1) * LANES], LANES))
    qcol = qi * tile + lax.broadcasted_iota(jnp.int32, (1, tile), 1)
    r_scr[...] = jnp.zeros_like(r_scr)
    acc_scr[...] = jnp.zeros_like(acc_scr)
    tri2 = tri_ref[...]

    def block(t, masked):
        start = pl.multiple_of(t * tile, tile)
        k = k_ref[0, pl.ds(start, tile), :]
        vt = vt_ref[:, pl.ds(start, tile)]
        if masked:
            key = start + lax.broadcasted_iota(jnp.int32, (tile, 1), 0)
            valid = (key >= ROW0) & (key < qcol)
        zs = [_dot_nt(k[:, (h // 2) * LANES:(h // 2 + 1) * LANES], qs[h]) for h in range(n_heads)]
        lks = []
        for h in range(n_heads):
            lk = _log_sigmoid_neg(zs[h])
            lks.append(jnp.where(valid, lk, 0.0) if masked else lk)
        laters = []
        for h in range(n_heads):
            lk_hi = lks[h].astype(BF16)
            lk_lo = (lks[h] - lk_hi.astype(F32)).astype(BF16)
            laters.append(_dot(tri2, jnp.concatenate([lk_hi, lk_lo], axis=0)))
        ws = []
        for h in range(n_heads):
            w = jnp.exp(lks[h] + zs[h] + (laters[h] + r_scr[h]))
            ws.append((jnp.where(valid, w, 0.0) if masked else w).astype(BF16))
            r_scr[h] = r_scr[h] + jnp.sum(lks[h], axis=0, keepdims=True)
        for g in range(group):
            rows = slice(g * LANES, (g + 1) * LANES)
            acc_scr[rows, :] += _dot(_pair_values_t(vt[rows]), jnp.concatenate(ws[2 * g:2 * g + 2], axis=0))

    def live():
        return (jnp.max(r_scr[...]) > DEAD_LOG_WEIGHT).astype(jnp.int32)

    block(qi, True)

    def body(c):
        block(c[0], False)
        return c[0] - 1, live()

    _, alive = lax.while_loop(lambda c: (c[0] > first) & (c[1] > 0), body, (qi - 1, live()))

    @pl.when((qi > first) & (alive > 0))
    def _():
        block(first, True)

    for g in range(group):
        rows = slice(g * LANES, (g + 1) * LANES)
        o_ref[0, :, rows] = acc_scr[rows, :].T.astype(o_ref.dtype)


def _sb_attention(qk, vt, *, n_pairs):
    B, Lp, _ = qk.shape
    tile, group = SB_TILE, SB_GROUP
    assert n_pairs % group == 0
    n_groups = n_pairs // group
    width = group * LANES
    r = lax.broadcasted_iota(jnp.int32, (tile, tile), 0)
    c = lax.broadcasted_iota(jnp.int32, (tile, tile), 1)
    tri = (c > r).astype(BF16)
    tri2 = jnp.concatenate([tri, tri], axis=1)
    kern = functools.partial(_sb_attn_kernel, tile=tile, group=group)
    return pl.pallas_call(
        kern,
        grid=(B, n_groups, Lp // tile),
        in_specs=[
            pl.BlockSpec((1, tile, width), lambda b, h, i: (b, i, h)),
            pl.BlockSpec((1, Lp, width), lambda b, h, i: (b, 0, n_groups + h), pipeline_mode=pl.Buffered(1)),
            pl.BlockSpec((width, Lp), lambda b, h, i: (h, b), pipeline_mode=pl.Buffered(1)),
            pl.BlockSpec((tile, 2 * tile), lambda b, h, i: (0, 0)),
        ],
        out_specs=pl.BlockSpec((1, tile, width), lambda b, h, i: (b, i, h)),
        out_shape=jax.ShapeDtypeStruct((B, Lp, n_pairs * LANES), BF16),
        scratch_shapes=[
            pltpu.VMEM((2 * group, 1, tile), F32),
            pltpu.VMEM((width, tile), F32),
        ],
        compiler_params=_params("arbitrary", "arbitrary", "arbitrary"),
        name="sb_attention",
    )(qk, qk, vt, tri2)


def _mla_proj_kernel(h_ref, g_ref, wd_ref, qn_ref, kvn_ref, wq_ref, wk_ref, wvt_ref, inv_ref,
                     q_ref, k_ref, vt_ref, *, tm, rows_per_batch, q_scale):
    i = pl.program_id(0)
    a = _rmsnorm(h_ref[...], g_ref[...]).astype(BF16)
    down = _dot(a, wd_ref[...])
    cq = _rmsnorm(down[:, :MLA_Q_RANK], qn_ref[...]).astype(BF16)
    ckv = _rmsnorm(down[:, MLA_Q_RANK:MLA_Q_RANK + MLA_KV_RANK], kvn_ref[...]).astype(BF16)
    kr = down[:, MLA_Q_RANK + MLA_KV_RANK:]

    row = (i * tm) % rows_per_batch + lax.broadcasted_iota(jnp.int32, (tm, 1), 0)
    pos = (row - ROW0).astype(F32)
    ang = pos * inv_ref[...]
    lane = lax.broadcasted_iota(jnp.int32, (1, LANES), 1)
    half = MLA_ROPE // 2
    in_lo = (lane >= MLA_NOPE) & (lane < MLA_NOPE + half)
    in_hi = (lane >= MLA_NOPE + half) & (lane < MLA_NOPE + MLA_ROPE)
    cos = jnp.cos(ang)
    sin = jnp.sin(ang)
    c_tab = jnp.where(lane < MLA_NOPE, 1.0, jnp.where(in_lo | in_hi, cos, 0.0))
    s_lo = jnp.where(in_lo, -sin, 0.0)
    s_hi = jnp.where(in_hi, sin, 0.0)

    def rope(x):
        return (x * c_tab + pltpu.roll(x, LANES - half, 1) * s_lo + pltpu.roll(x, half, 1) * s_hi)

    q = _dot(cq, wq_ref[...])
    kn = _dot(ckv, wk_ref[...])
    vt_ref[...] = _dot_nt(wvt_ref[...], ckv).astype(BF16)
    kr_rot = rope(kr)
    for hd in range(N_HEADS):
        cols = slice(hd * LANES, (hd + 1) * LANES)
        q_ref[:, cols] = (rope(q[:, cols]) * q_scale).astype(BF16)
        k_ref[:, cols] = (kn[:, cols] + kr_rot).astype(BF16)


def _mla_proj(h2, g, wd, qn, kvn, wq, wk, wvt, inv_lane, rows_per_batch):
    M, D = h2.shape
    tm = _divisor_tile(rows_per_batch, 512)
    kern = functools.partial(_mla_proj_kernel, tm=tm, rows_per_batch=rows_per_batch,
                             q_scale=(MLA_NOPE + MLA_ROPE) ** -0.5 * LOG2E)
    full = lambda a: pl.BlockSpec(a.shape, lambda i: (0,) * a.ndim)
    g2, qn2, kvn2 = g.reshape(1, D), qn.reshape(1, -1), kvn.reshape(1, -1)
    return pl.pallas_call(
        kern,
        grid=(M // tm,),
        in_specs=[pl.BlockSpec((tm, D), lambda i: (i, 0)), full(g2), full(wd), full(qn2), full(kvn2),
                  full(wq), full(wk), full(wvt), full(inv_lane)],
        out_specs=[
            pl.BlockSpec((tm, N_HEADS * LANES), lambda i: (i, 0)),
            pl.BlockSpec((tm, N_HEADS * LANES), lambda i: (i, 0)),
            pl.BlockSpec((N_HEADS * MLA_V, tm), lambda i: (0, i)),
        ],
        out_shape=[
            jax.ShapeDtypeStruct((M, N_HEADS * LANES), BF16),
            jax.ShapeDtypeStruct((M, N_HEADS * LANES), BF16),
            jax.ShapeDtypeStruct((N_HEADS * MLA_V, M), BF16),
        ],
        compiler_params=_params("arbitrary"),
        name="mla_proj",
    )(h2, g2, wd, qn2, kvn2, wq, wk, wvt, inv_lane)


def _forget_cumsum_kernel(f_ref, b_ref, fcol_ref, frow_ref, carry, *, tile):
    t = pl.program_id(1)

    @pl.when(t == 0)
    def _():
        carry[...] = jnp.zeros_like(carry)

    row_in = lax.broadcasted_iota(jnp.int32, (tile, 1), 0)
    row = t * tile + row_in
    x = jnp.where(row >= ROW0, _log_sigmoid_neg(-(f_ref[0] + b_ref[...])), 0.0)
    shift = 1
    while shift < tile:
        x = x + jnp.where(row_in >= shift, pltpu.roll(x, shift, 0), 0.0)
        shift *= 2
    x = x + carry[...]
    carry[...] = x[tile - 1:tile, :]
    fcol_ref[0] = x
    frow_ref[0] = x.T[:N_HEADS, :]


def _forget_cumsum(flog, b_lane):
    B, Lp, _ = flog.shape
    tile = ATT_TILE
    kern = functools.partial(_forget_cumsum_kernel, tile=tile)
    return pl.pallas_call(
        kern,
        grid=(B, Lp // tile),
        in_specs=[
            pl.BlockSpec((1, tile, LANES), lambda b, t: (b, t, 0)),
            pl.BlockSpec((1, LANES), lambda b, t: (0, 0)),
        ],
        out_specs=[
            pl.BlockSpec((1, tile, LANES), lambda b, t: (b, t, 0)),
            pl.BlockSpec((1, N_HEADS, tile), lambda b, t: (b, 0, t)),
        ],
        out_shape=[
            jax.ShapeDtypeStruct((B, Lp, LANES), F32),
            jax.ShapeDtypeStruct((B, N_HEADS, Lp), F32),
        ],
        scratch_shapes=[pltpu.VMEM((1, LANES), F32)],
        compiler_params=_params("arbitrary", "arbitrary"),
        name="forget_cumsum",
    )(flog, b_lane)


def _final_norm_kernel(h_ref, g_ref, o_ref):
    o_ref[0] = _rmsnorm(h_ref[0], g_ref[...])


def _final_norm(h, g, seq):
    B, Lp, D = h.shape
    tile = PAD_FRONT
    skip = PAD_FRONT // tile
    return pl.pallas_call(
        _final_norm_kernel,
        grid=(B, seq // tile),
        in_specs=[
            pl.BlockSpec((1, tile, D), lambda b, t: (b, t + skip, 0)),
            pl.BlockSpec((1, D), lambda b, t: (0, 0)),
        ],
        out_specs=pl.BlockSpec((1, tile, D), lambda b, t: (b, t, 0)),
        out_shape=jax.ShapeDtypeStruct((B, seq, D), F32),
        compiler_params=_params("arbitrary", "arbitrary"),
        name="final_norm",
    )(h, g.reshape(1, D))


def _pad_heads(w, per_head):
    K = w.shape[0]
    w3 = w.reshape(K, N_HEADS, per_head)
    w3 = jnp.pad(w3, ((0, 0), (0, 0), (0, LANES - per_head)))
    return w3.reshape(K, N_HEADS * LANES)


def _mla_layer(h, g, w_down, q_norm, kv_norm, w_uq, w_ukv, w_o):
    B, Lp, D = h.shape
    M = B * Lp
    n_lat = MLA_Q_RANK + MLA_KV_RANK
    wd_rope = jnp.pad(w_down[:, n_lat:], ((0, 0), (MLA_NOPE, LANES - MLA_NOPE - MLA_ROPE)))
    wd_p = jnp.concatenate([w_down[:, :n_lat], wd_rope], axis=1).astype(BF16)
    wq_p = _pad_heads(w_uq, MLA_NOPE + MLA_ROPE).astype(BF16)
    wkv3 = w_ukv.reshape(MLA_KV_RANK, N_HEADS, MLA_NOPE + MLA_V)
    wk_p = _pad_heads(wkv3[:, :, :MLA_NOPE].reshape(MLA_KV_RANK, -1), MLA_NOPE).astype(BF16)
    wvt_p = wkv3[:, :, MLA_NOPE:].reshape(MLA_KV_RANK, N_HEADS * MLA_V).T.astype(BF16)
    inv = ROPE_THETA ** (-jnp.arange(0, MLA_ROPE, 2, dtype=F32) / MLA_ROPE)
    inv_lane = jnp.concatenate([jnp.zeros((MLA_NOPE,), F32), inv, inv,
                                jnp.zeros((LANES - MLA_NOPE - MLA_ROPE,), F32)]).reshape(1, LANES)
    q, k, vt = _mla_proj(h.reshape(M, D), g, wd_p, q_norm, kv_norm, wq_p, wk_p, wvt_p, inv_lane, Lp)
    o = _softmax_attention(q.reshape(B, Lp, -1), k.reshape(B, Lp, -1), vt,
                           width=2 * LANES, n_pairs=N_HEADS // 2)
    return _proj_residual(o.reshape(M, -1), w_o.astype(BF16), h.reshape(M, D)).reshape(B, Lp, D)


def kernel(x, meta, norm_mix, norm_ffn, pool_w, pool_scale, sb_w_qkv, sb_w_o, mla_w_down, mla_q_norm,
           mla_kv_norm, mla_w_uq, mla_w_ukv, mla_w_o, fox_w_qkvf, fox_b_f, fox_w_o, ffn_w_gate, ffn_w_up,
           ffn_w_down, final_norm):
    B, S, D = x.shape
    assert D == D_MODEL and S % ATT_TILE == 0 and PAD_FRONT % ATT_TILE == 0 and PAD_FRONT % SB_TILE == 0
    Lp = S + PAD_FRONT
    M = B * Lp
    HD = N_HEADS * HEAD_DIM
    n_pairs = N_HEADS // 2

    meta_b = jnp.broadcast_to(meta[None].astype(x.dtype), (B, N_META, D))
    h = jnp.concatenate([jnp.zeros((B, ROW0, D), x.dtype), meta_b, x], axis=1)

    def ffn(h, i):
        return _ffn(h.reshape(M, D), norm_ffn[i], ffn_w_gate[i].astype(BF16), ffn_w_up[i].astype(BF16),
                    ffn_w_down[i].astype(BF16)).reshape(B, Lp, D)

    def pool_mixer(h, i, j):
        return _pool_layer(h, norm_mix[i], pool_w[j], pool_scale[j])

    def sb_mixer(h, i, j):
        qk, vt = _qk_vt_proj(h.reshape(M, D), norm_mix[i], sb_w_qkv[j][:, :2 * HD].astype(BF16),
                             sb_w_qkv[j][:, 2 * HD:].T.astype(BF16), q_scale=HEAD_DIM ** -0.5, tn=HD)
        o = _sb_attention(qk.reshape(B, Lp, 2 * HD), vt, n_pairs=n_pairs)
        return _proj_residual(o.reshape(M, HD), sb_w_o[j].astype(BF16), h.reshape(M, D)).reshape(B, Lp, D)

    def mla_mixer(h, i, j):
        return _mla_layer(h, norm_mix[i], mla_w_down[j], mla_q_norm[j], mla_kv_norm[j], mla_w_uq[j],
                          mla_w_ukv[j], mla_w_o[j])

    def fox_mixer(h, i, j):
        wf = jnp.pad(fox_w_qkvf[j][:, 3 * HD:], ((0, 0), (0, LANES - N_HEADS))).astype(BF16)
        qk, vt, flog = _qk_vt_proj(h.reshape(M, D), norm_mix[i], fox_w_qkvf[j][:, :2 * HD].astype(BF16),
                                   fox_w_qkvf[j][:, 2 * HD:3 * HD].T.astype(BF16), wf,
                                   q_scale=HEAD_DIM ** -0.5 * LOG2E, tn=HD)
        b_lane = jnp.pad(fox_b_f[j].astype(F32), (0, LANES - N_HEADS)).reshape(1, LANES)
        fcol, frow = _forget_cumsum(flog.reshape(B, Lp, LANES), b_lane)
        qk = qk.reshape(B, Lp, 2 * HD)
        o = _softmax_attention(qk, qk, vt, width=LANES, n_pairs=n_pairs, offs=(0, n_pairs), fcol=fcol, frow=frow)
        return _proj_residual(o.reshape(M, HD), fox_w_o[j].astype(BF16), h.reshape(M, D)).reshape(B, Lp, D)

    mixers = (pool_mixer, sb_mixer, mla_mixer, fox_mixer)
    for i in range(norm_mix.shape[0]):
        h = mixers[i % len(mixers)](h, i, i // len(mixers))
        h = ffn(h, i)
    return _final_norm(h, final_norm, S)
```

```python
import functools

import jax
import jax.numpy as jnp
from jax import lax
from jax.experimental import pallas as pl
from jax.experimental.pallas import tpu as pltpu

F32 = jnp.float32
BF16 = jnp.bfloat16

D_MODEL = 1024
N_META = 16
EPS = 1e-6
POOL_WINDOWS = (2, 4, 8, 16)
POOL_GROUP = D_MODEL // len(POOL_WINDOWS)
N_HEADS = 16
HEAD_DIM = 64
MLA_Q_RANK = 384
MLA_KV_RANK = 256
MLA_NOPE = 64
MLA_ROPE = 32
MLA_V = 64
ROPE_THETA = 10000.0

LANES = 128
PAD_FRONT = 256
ROW0 = PAD_FRONT - N_META
HALO = max(POOL_WINDOWS)
MASK_VALUE = -1e30
LOG2E = 1.4426950408889634
DECAY_PIECES = 3
ACC_ROWS = LANES + 16
DEAD_LOG_WEIGHT = -104.0
VMEM_LIMIT = 56 * 1024 * 1024

ROW_TILE = 1024
ATT_TILE = 256
ATT_GROUP = 4
SB_TILE = 128
SB_LEAD = 3
SB_GROUP = 4
FF_TILE = 256


def _divisor_tile(n, target):
    best = None
    for t in range(8, min(n, target) + 1, 8):
        if n % t == 0:
            best = t
    assert best is not None, (n, target)
    return best


def _params(*sem):
    return pltpu.CompilerParams(dimension_semantics=sem, vmem_limit_bytes=VMEM_LIMIT)


def _rmsnorm(x, g):
    return x * lax.rsqrt(jnp.mean(x * x, axis=-1, keepdims=True) + EPS) * g


def _log_sigmoid_neg(z):
    return -(jnp.maximum(z, 0.0) + jnp.log(1.0 + jnp.exp(-jnp.abs(z))))


def _dot(a, b):
    return jnp.dot(a, b, preferred_element_type=F32)


def _dot_nt(a, b):
    return lax.dot_general(a, b, (((1,), (1,)), ((), ())), preferred_element_type=F32)


def _pool_kernel(x_ref, meta_ref, g_ref, w_ref, sc_ref, o_ref, abuf, *, tile):
    t = pl.program_id(1)
    front = jnp.concatenate([jnp.zeros((ROW0, D_MODEL), F32), meta_ref[...]], axis=0)
    h = jnp.where(t == 0, front, x_ref[0])
    row = t * tile + lax.broadcasted_iota(jnp.int32, (tile, 1), 0)
    a = jnp.where(row >= ROW0, _rmsnorm(h, g_ref[...]), 0.0)

    @pl.when(t == 0)
    def _():
        abuf[0:HALO, :] = jnp.zeros((HALO, D_MODEL), F32)

    @pl.when(t > 0)
    def _():
        abuf[0:HALO, :] = abuf[tile:tile + HALO, :]

    abuf[HALO:HALO + tile, :] = a
    pos1 = row - ROW0 + 1
    outs = []
    for g, win in enumerate(POOL_WINDOWS):
        cols = slice(g * POOL_GROUP, (g + 1) * POOL_GROUP)
        xg = a[:, cols]
        acc = xg
        for j in range(1, win):
            acc = acc + abuf[HALO - j:HALO - j + tile, cols]
        cnt = jnp.clip(pos1, 1, win).astype(F32)
        pooled = acc / cnt - xg
        outs.append(_dot(pooled.astype(BF16), w_ref[g]))
    mixed = jnp.concatenate(outs, axis=1) * sc_ref[...]
    o_ref[0] = h + mixed


def _pool_layer(x, meta, g, w, sc):
    B, S, D = x.shape
    tile = PAD_FRONT
    Lp = S + PAD_FRONT
    kern = functools.partial(_pool_kernel, tile=tile)
    return pl.pallas_call(
        kern,
        grid=(B, Lp // tile),
        in_specs=[
            pl.BlockSpec((1, tile, D), lambda b, t: (b, jnp.maximum(t - 1, 0), 0)),
            pl.BlockSpec((N_META, D), lambda b, t: (0, 0)),
            pl.BlockSpec((1, D), lambda b, t: (0, 0)),
            pl.BlockSpec((len(POOL_WINDOWS), POOL_GROUP, POOL_GROUP), lambda b, t: (0, 0, 0)),
            pl.BlockSpec((1, D), lambda b, t: (0, 0)),
        ],
        out_specs=pl.BlockSpec((1, tile, D), lambda b, t: (b, t, 0)),
        out_shape=jax.ShapeDtypeStruct((B, Lp, D), F32),
        scratch_shapes=[pltpu.VMEM((tile + HALO, D), F32)],
        compiler_params=_params("arbitrary", "arbitrary"),
        name="pool_layer",
    )(x, meta.astype(x.dtype), g.reshape(1, D), w.astype(BF16), sc.reshape(1, D))


def _qk_vt_proj_kernel(*refs, q_scale, has_gate):
    if has_gate:
        h_ref, g_ref, w_ref, wvt_ref, wf_ref, o_ref, vt_ref, f_ref, a_scr = refs
    else:
        h_ref, g_ref, w_ref, wvt_ref, o_ref, vt_ref, a_scr = refs

    @pl.when(pl.program_id(1) == 0)
    def _():
        a = _rmsnorm(h_ref[...], g_ref[...]).astype(BF16)
        a_scr[...] = a
        vt_ref[...] = _dot_nt(wvt_ref[...], a).astype(vt_ref.dtype)
        if has_gate:
            f_ref[...] = _dot(a, wf_ref[...])

    col_scale = jnp.where(pl.program_id(1) == 0, q_scale, 1.0)
    o_ref[...] = (_dot(a_scr[...], w_ref[...]) * col_scale).astype(o_ref.dtype)


def _qk_vt_proj(h2, g, w_qk, w_vt, wf=None, *, q_scale, tn):
    M, D = h2.shape
    N = w_qk.shape[1]
    Dv = w_vt.shape[0]
    tm = _divisor_tile(M, ROW_TILE)
    has_gate = wf is not None
    in_specs = [
        pl.BlockSpec((tm, D), lambda i, j: (i, 0)),
        pl.BlockSpec((1, D), lambda i, j: (0, 0)),
        pl.BlockSpec((D, tn), lambda i, j: (0, j)),
        pl.BlockSpec((Dv, D), lambda i, j: (0, 0)),
    ]
    out_specs = [
        pl.BlockSpec((tm, tn), lambda i, j: (i, j)),
        pl.BlockSpec((Dv, tm), lambda i, j: (0, i)),
    ]
    out_shape = [jax.ShapeDtypeStruct((M, N), BF16), jax.ShapeDtypeStruct((Dv, M), BF16)]
    args = [h2, g.reshape(1, D), w_qk, w_vt]
    if has_gate:
        in_specs.append(pl.BlockSpec((D, LANES), lambda i, j: (0, 0)))
        out_specs.append(pl.BlockSpec((tm, LANES), lambda i, j: (i, 0)))
        out_shape.append(jax.ShapeDtypeStruct((M, LANES), F32))
        args.append(wf)
    return pl.pallas_call(
        functools.partial(_qk_vt_proj_kernel, q_scale=q_scale, has_gate=has_gate),
        grid=(M // tm, N // tn),
        in_specs=in_specs,
        out_specs=out_specs,
        out_shape=out_shape,
        scratch_shapes=[pltpu.VMEM((tm, D), BF16)],
        compiler_params=_params("arbitrary", "arbitrary"),
        name="qk_vt_proj",
    )(*args)


def _ffn_kernel(*refs, tf, has_mix):
    if has_mix:
        h_ref, o_ref, wo_ref, g_ref, wg_ref, wu_ref, wd_ref, out_ref, acc_scr = refs
        h = h_ref[...] + _dot(o_ref[...], wo_ref[...])
    else:
        h_ref, g_ref, wg_ref, wu_ref, wd_ref, out_ref, acc_scr = refs
        h = h_ref[...]
    a = _rmsnorm(h, g_ref[...]).astype(BF16)
    for c in range(wg_ref.shape[1] // tf):
        cols = slice(c * tf, (c + 1) * tf)
        gate = _dot(a, wg_ref[:, cols])
        up = _dot(a, wu_ref[:, cols])
        act = (gate * jax.nn.sigmoid(gate) * up).astype(BF16)
        part = _dot(act, wd_ref[cols, :])
        if c == 0:
            acc_scr[...] = part
        else:
            acc_scr[...] += part
    out_ref[...] = h + acc_scr[...]


def _ffn(h2, g, wg, wu, wd, o2=None, wo=None):
    M, D = h2.shape
    F = wg.shape[1]
    assert F % FF_TILE == 0
    tm = _divisor_tile(M, ROW_TILE)
    has_mix = o2 is not None
    resident = lambda shape: pl.BlockSpec(shape, lambda i: (0, 0), pipeline_mode=pl.Buffered(1))
    rows = lambda width: pl.BlockSpec((tm, width), lambda i: (i, 0))
    in_specs, args = [rows(D)], [h2]
    if has_mix:
        in_specs += [rows(o2.shape[1]), resident(wo.shape)]
        args += [o2, wo]
    in_specs += [resident((1, D)), resident((D, F)), resident((D, F)), resident((F, D))]
    args += [g.reshape(1, D), wg, wu, wd]
    return pl.pallas_call(
        functools.partial(_ffn_kernel, tf=FF_TILE, has_mix=has_mix),
        grid=(M // tm,),
        in_specs=in_specs,
        out_specs=rows(D),
        out_shape=jax.ShapeDtypeStruct((M, D), F32),
        scratch_shapes=[pltpu.VMEM((tm, D), F32)],
        compiler_params=_params("arbitrary"),
        name="ffn",
    )(*args)


def _split_heads(x, width):
    if width == 2 * LANES:
        return x[:, :LANES], x[:, LANES:]
    lane = lax.broadcasted_iota(jnp.int32, (1, LANES), 1)
    zero = jnp.zeros_like(x)
    return jnp.where(lane < HEAD_DIM, x, zero), jnp.where(lane >= HEAD_DIM, x, zero)


def _split_keys(x, width):
    if width == 2 * LANES:
        return x[:, :LANES], x[:, LANES:]
    return x, x


def _pair_values_t(vt):
    sub = lax.broadcasted_iota(jnp.int32, (LANES, 1), 0)
    zero = jnp.zeros_like(vt)
    return jnp.concatenate(
        [jnp.where(sub < HEAD_DIM, vt, zero), jnp.where(sub >= HEAD_DIM, vt, zero)], axis=1)


def _pair_rows(x0, x1):
    sub = lax.broadcasted_iota(jnp.int32, (ACC_ROWS, 1), 0)
    return jnp.where((sub < HEAD_DIM) | (sub == LANES), x0, x1)


def _sum_rows(tile):
    sub = lax.broadcasted_iota(jnp.int32, (ACC_ROWS - LANES, 2 * tile), 0)
    col = lax.broadcasted_iota(jnp.int32, (ACC_ROWS - LANES, 2 * tile), 1)
    return jnp.where(sub == col // tile, 1.0, 0.0).astype(BF16)


def _softmax_attn_kernel(*refs, tile, width, decay, group):
    if decay:
        q_ref, k_ref, vt_ref, kf_ref, frow_ref, o_ref = refs[:6]
    else:
        q_ref, k_ref, vt_ref, o_ref = refs[:4]
    m_scr, acc_scr, raw0, raw1, p0, p1, al0, al1 = refs[-8:]
    raw_s, p_s, al_s = (raw0, raw1), (p0, p1), (al0, al1)
    hg = pl.program_id(1)
    qi = pl.program_id(2)
    n_heads = 2 * group
    q = q_ref[0]
    qs = []
    for g in range(group):
        qs += list(_split_heads(q[:, g * width:(g + 1) * width], width))
    qcol = qi * tile + lax.broadcasted_iota(jnp.int32, (1, tile), 1)
    if decay:
        lane = lax.broadcasted_iota(jnp.int32, (1, LANES), 1)
        for h in range(n_heads):
            e = h % 2
            ones = jnp.where((lane >= e * DECAY_PIECES) & (lane < (e + 1) * DECAY_PIECES), 1.0, 0.0)
            qs[h] = jnp.concatenate([qs[h], jnp.broadcast_to(ones.astype(BF16), (tile, LANES))], axis=1)
        head0 = n_heads * hg
        fq = tuple(frow_ref[0, pl.ds(head0 + h, 1), pl.ds(pl.multiple_of(qi * tile, tile), tile)] * LOG2E
                   for h in range(n_heads))

    m_scr[...] = jnp.full_like(m_scr, MASK_VALUE)
    acc_scr[...] = jnp.zeros_like(acc_scr)
    sum_rows = _sum_rows(tile)

    def scores(t, s):
        start = pl.multiple_of(t * tile, tile)
        k = k_ref[0, pl.ds(start, tile), :]
        if decay:
            kf = kf_ref[0, pl.ds(start, tile), :]
        for g in range(group):
            ks = _split_keys(k[:, g * width:(g + 1) * width], width)
            if decay:
                ks = tuple(jnp.concatenate([kk, kf[:, g * LANES:(g + 1) * LANES]], axis=1) for kk in ks)
            for e in range(2):
                raw_s[s][2 * g + e] = _dot_nt(ks[e], qs[2 * g + e])

    def weights(t, s, masked):
        start = pl.multiple_of(t * tile, tile)
        if masked:
            key = start + lax.broadcasted_iota(jnp.int32, (tile, 1), 0)
            valid = (key >= ROW0) & (key <= qcol)
        for h in range(n_heads):
            raw = raw_s[s][h]
            m_old = m_scr[h]
            if masked:
                raw = jnp.where(valid, raw, MASK_VALUE)
            if decay:
                m_new = jnp.maximum(m_old, jnp.max(raw, axis=0, keepdims=True) + fq[h])
                p = jnp.exp2(raw - (m_new - fq[h]))
            else:
                m_new = jnp.maximum(m_old, jnp.max(raw, axis=0, keepdims=True))
                p = jnp.exp2(raw - m_new)
            m_scr[h] = m_new
            p_s[s][h * tile:(h + 1) * tile, :] = p.astype(BF16)
            al_s[s][h] = jnp.exp2(m_old - m_new)

    def accumulate(t, s):
        vt = vt_ref[:, pl.ds(pl.multiple_of(t * tile, tile), tile)]
        for g in range(group):
            lhs = jnp.concatenate([_pair_values_t(vt[g * LANES:(g + 1) * LANES]), sum_rows], axis=0)
            pv = _dot(lhs, p_s[s][2 * g * tile:(2 * g + 2) * tile, :])
            rows = slice(g * ACC_ROWS, (g + 1) * ACC_ROWS)
            acc_scr[rows, :] = _pair_rows(al_s[s][2 * g], al_s[s][2 * g + 1]) * acc_scr[rows, :] + pv

    scores(0, 0)
    weights(0, 0, True)

    @pl.when(qi == 0)
    def _():
        accumulate(0, 0)

    @pl.when(qi > 0)
    def _():
        scores(1, 1)
        n_steady = qi - 1

        def pair(i, c):
            t = 2 * i + 1
            scores(t + 1, 0)
            weights(t, 1, False)
            accumulate(t - 1, 0)
            scores(t + 2, 1)
            weights(t + 1, 0, False)
            accumulate(t, 1)
            return c

        lax.fori_loop(0, n_steady // 2, pair, 0)

        @pl.when(n_steady % 2 == 1)
        def _():
            t = qi - 1
            scores(qi, 0)
            weights(t, 1, False)
            accumulate(t - 1, 0)
            weights(qi, 0, True)
            accumulate(t, 1)
            accumulate(qi, 0)

        @pl.when(n_steady % 2 == 0)
        def _():
            weights(qi, 1, True)
            accumulate(qi - 1, 0)
            accumulate(qi, 1)

    sub = lax.broadcasted_iota(jnp.int32, (LANES, 1), 0)
    for g in range(group):
        acc = acc_scr[g * ACC_ROWS:(g + 1) * ACC_ROWS, :]
        total = jnp.where(sub < HEAD_DIM, acc[LANES:LANES + 1], acc[LANES + 1:LANES + 2])
        o_ref[0, :, g * LANES:(g + 1) * LANES] = (acc[:LANES] / total).T.astype(o_ref.dtype)


def _softmax_attention(q, k, vt, *, width, n_pairs, offs=(0, 0), kf=None, frow=None):
    B, Lp, _ = q.shape
    qo, ko = offs
    tile = ATT_TILE
    group = ATT_GROUP
    assert n_pairs % group == 0 and qo % group == 0 and ko % group == 0
    n_heads = 2 * group
    decay = kf is not None
    kern = functools.partial(_softmax_attn_kernel, tile=tile, width=width, decay=decay, group=group)
    in_specs = [
        pl.BlockSpec((1, tile, group * width), lambda b, h, i: (b, i, qo // group + h)),
        pl.BlockSpec((1, Lp, group * width), lambda b, h, i: (b, 0, ko // group + h), pipeline_mode=pl.Buffered(1)),
        pl.BlockSpec((group * LANES, Lp), lambda b, h, i: (h, b), pipeline_mode=pl.Buffered(1)),
    ]
    args = [q, k, vt]
    if decay:
        in_specs += [
            pl.BlockSpec((1, Lp, group * LANES), lambda b, h, i: (b, 0, h), pipeline_mode=pl.Buffered(1)),
            pl.BlockSpec((1, N_HEADS, Lp), lambda b, h, i: (b, 0, 0)),
        ]
        args += [kf, frow]
    return pl.pallas_call(
        kern,
        grid=(B, n_pairs // group, Lp // tile),
        in_specs=in_specs,
        out_specs=pl.BlockSpec((1, tile, group * LANES), lambda b, h, i: (b, i, h)),
        out_shape=jax.ShapeDtypeStruct((B, Lp, n_pairs * LANES), BF16),
        scratch_shapes=[
            pltpu.VMEM((n_heads, 1, tile), F32),
            pltpu.VMEM((group * ACC_ROWS, tile), F32),
            pltpu.VMEM((n_heads, tile, tile), F32),
            pltpu.VMEM((n_heads, tile, tile), F32),
            pltpu.VMEM((n_heads * tile, tile), BF16),
            pltpu.VMEM((n_heads * tile, tile), BF16),
            pltpu.VMEM((n_heads, 1, tile), F32),
            pltpu.VMEM((n_heads, 1, tile), F32),
        ],
        compiler_params=_params("arbitrary", "arbitrary", "arbitrary"),
        name="fox_attention" if decay else "mla_attention",
    )(*args)


def _sb_attn_kernel(q_ref, k_ref, vt_ref, tri_ref, o_ref, r_scr, acc_scr, *, tile, group):
    qi = pl.program_id(2)
    n_heads = 2 * group
    first = ROW0 // tile
    q = q_ref[0]
    qs = []
    for g in range(group):
        qs += list(_split_heads(q[:, g * LANES:(g + 1) * LANES], LANES))
    qcol = qi * tile + lax.broadcasted_iota(jnp.int32, (1, tile), 1)
    r_scr[...] = jnp.zeros_like(r_scr)
    acc_scr[...] = jnp.zeros_like(acc_scr)
    tri2 = tri_ref[...]

    def walk(tiles):
        ks, vts, valids = [], [], []
        for t, masked in tiles:
            start = pl.multiple_of(t * tile, tile)
            ks.append(k_ref[0, pl.ds(start, tile), :])
            vts.append(vt_ref[:, pl.ds(start, tile)])
            key = start + lax.broadcasted_iota(jnp.int32, (tile, 1), 0)
            valids.append((key >= ROW0) & (key < qcol) if masked else None)
        n = len(tiles)
        zs = [[_dot_nt(ks[i][:, (h // 2) * LANES:(h // 2 + 1) * LANES], qs[h]) for h in range(n_heads)]
              for i in range(n)]
        lks = [[None] * n_heads for _ in range(n)]
        for i in range(n):
            for h in range(n_heads):
                lk = _log_sigmoid_neg(zs[i][h])
                lks[i][h] = lk if valids[i] is None else jnp.where(valids[i], lk, 0.0)
        laters = [[None] * n_heads for _ in range(n)]
        for i in range(n):
            for h in range(n_heads):
                lk_hi = lks[i][h].astype(BF16)
                lk_lo = (lks[i][h] - lk_hi.astype(F32)).astype(BF16)
                laters[i][h] = _dot(tri2, jnp.concatenate([lk_hi, lk_lo], axis=0))
        ws = [[None] * n_heads for _ in range(n)]
        for h in range(n_heads):
            r = r_scr[h]
            for i in range(n):
                w = jnp.exp(lks[i][h] + zs[i][h] + (laters[i][h] + r))
                ws[i][h] = (w if valids[i] is None else jnp.where(valids[i], w, 0.0)).astype(BF16)
                r = r + jnp.sum(lks[i][h], axis=0, keepdims=True)
            r_scr[h] = r
        for g in range(group):
            rows = slice(g * LANES, (g + 1) * LANES)
            lhs = jnp.concatenate([_pair_values_t(vts[i][rows]) for i in range(n)], axis=1)
            rhs = jnp.concatenate([ws[i][h] for i in range(n) for h in (2 * g, 2 * g + 1)], axis=0)
            acc_scr[rows, :] += _dot(lhs, rhs)

    def live():
        return (jnp.max(r_scr[...]) > DEAD_LOG_WEIGHT).astype(jnp.int32)

    lead = qi - (SB_LEAD - 1) > first

    @pl.when(lead)
    def _():
        walk(tuple((qi - i, i == 0) for i in range(SB_LEAD)))

    @pl.when(jnp.logical_not(lead))
    def _():
        walk(((qi, True),))

    def body(c):
        walk(((c[0], False),))
        return c[0] - 1, live()

    t0 = jnp.where(lead, qi - SB_LEAD, qi - 1)
    _, alive = lax.while_loop(lambda c: (c[0] > first) & (c[1] > 0), body, (t0, live()))

    @pl.when((qi > first) & (alive > 0))
    def _():
        walk(((first, True),))

    for g in range(group):
        rows = slice(g * LANES, (g + 1) * LANES)
        o_ref[0, :, rows] = acc_scr[rows, :].T.astype(o_ref.dtype)


def _sb_attention(qk, vt, *, n_pairs):
    B, Lp, _ = qk.shape
    tile, group = SB_TILE, SB_GROUP
    assert n_pairs % group == 0
    n_groups = n_pairs // group
    width = group * LANES
    r = lax.broadcasted_iota(jnp.int32, (tile, tile), 0)
    c = lax.broadcasted_iota(jnp.int32, (tile, tile), 1)
    tri = (c > r).astype(BF16)
    tri2 = jnp.concatenate([tri, tri], axis=1)
    kern = functools.partial(_sb_attn_kernel, tile=tile, group=group)
    return pl.pallas_call(
        kern,
        grid=(B, n_groups, Lp // tile),
        in_specs=[
            pl.BlockSpec((1, tile, width), lambda b, h, i: (b, i, h)),
            pl.BlockSpec((1, Lp, width), lambda b, h, i: (b, 0, n_groups + h), pipeline_mode=pl.Buffered(1)),
            pl.BlockSpec((width, Lp), lambda b, h, i: (h, b), pipeline_mode=pl.Buffered(1)),
            pl.BlockSpec((tile, 2 * tile), lambda b, h, i: (0, 0)),
        ],
        out_specs=pl.BlockSpec((1, tile, width), lambda b, h, i: (b, i, h)),
        out_shape=jax.ShapeDtypeStruct((B, Lp, n_pairs * LANES), BF16),
        scratch_shapes=[
            pltpu.VMEM((2 * group, 1, tile), F32),
            pltpu.VMEM((width, tile), F32),
        ],
        compiler_params=_params("arbitrary", "arbitrary", "arbitrary"),
        name="sb_attention",
    )(qk, qk, vt, tri2)


def _rope_table_kernel(inv_ref, o_ref, *, tile):
    row = pl.program_id(0) * tile + lax.broadcasted_iota(jnp.int32, (tile, 1), 0)
    ang = (row - ROW0).astype(F32) * inv_ref[...]
    lane = lax.broadcasted_iota(jnp.int32, (1, LANES), 1)
    half = MLA_ROPE // 2
    in_lo = (lane >= MLA_NOPE) & (lane < MLA_NOPE + half)
    in_hi = (lane >= MLA_NOPE + half) & (lane < MLA_NOPE + MLA_ROPE)
    cos = jnp.cos(ang)
    sin = jnp.sin(ang)
    o_ref[0] = jnp.where(lane < MLA_NOPE, 1.0, jnp.where(in_lo | in_hi, cos, 0.0))
    o_ref[1] = jnp.where(in_lo, -sin, 0.0)
    o_ref[2] = jnp.where(in_hi, sin, 0.0)


def _rope_tables(rows, inv_lane):
    tile = _divisor_tile(rows, 512)
    return pl.pallas_call(
        functools.partial(_rope_table_kernel, tile=tile),
        grid=(rows // tile,),
        in_specs=[pl.BlockSpec((1, LANES), lambda i: (0, 0))],
        out_specs=pl.BlockSpec((3, tile, LANES), lambda i: (0, i, 0)),
        out_shape=jax.ShapeDtypeStruct((3, rows, LANES), F32),
        compiler_params=_params("arbitrary"),
        name="rope_tables",
    )(inv_lane)


def _mla_proj_kernel(h_ref, g_ref, wd_ref, qn_ref, kvn_ref, wq_ref, wk_ref, wvt_ref, tab_ref,
                     q_ref, k_ref, vt_ref, *, q_scale):
    a = _rmsnorm(h_ref[...], g_ref[...]).astype(BF16)
    down = _dot(a, wd_ref[...])
    cq = _rmsnorm(down[:, :MLA_Q_RANK], qn_ref[...]).astype(BF16)
    ckv = _rmsnorm(down[:, MLA_Q_RANK:MLA_Q_RANK + MLA_KV_RANK], kvn_ref[...]).astype(BF16)
    kr = down[:, MLA_Q_RANK + MLA_KV_RANK:]
    c_tab, s_lo, s_hi = tab_ref[0], tab_ref[1], tab_ref[2]
    half = MLA_ROPE // 2

    def rope(x):
        return (x * c_tab + pltpu.roll(x, LANES - half, 1) * s_lo + pltpu.roll(x, half, 1) * s_hi)

    q = _dot(cq, wq_ref[...])
    kn = _dot(ckv, wk_ref[...])
    vt_ref[...] = _dot_nt(wvt_ref[...], ckv).astype(BF16)
    kr_rot = rope(kr)
    for hd in range(N_HEADS):
        cols = slice(hd * LANES, (hd + 1) * LANES)
        q_ref[:, cols] = (rope(q[:, cols]) * q_scale).astype(BF16)
        k_ref[:, cols] = (kn[:, cols] + kr_rot).astype(BF16)


def _mla_proj(h2, g, wd, qn, kvn, wq, wk, wvt, inv_lane, rows_per_batch):
    M, D = h2.shape
    tm = _divisor_tile(rows_per_batch, 512)
    tiles_per_batch = rows_per_batch // tm
    tables = _rope_tables(rows_per_batch, inv_lane)
    kern = functools.partial(_mla_proj_kernel, q_scale=(MLA_NOPE + MLA_ROPE) ** -0.5 * LOG2E)
    full = lambda a: pl.BlockSpec(a.shape, lambda i: (0,) * a.ndim)
    g2, qn2, kvn2 = g.reshape(1, D), qn.reshape(1, -1), kvn.reshape(1, -1)
    return pl.pallas_call(
        kern,
        grid=(M // tm,),
        in_specs=[pl.BlockSpec((tm, D), lambda i: (i, 0)), full(g2), full(wd), full(qn2), full(kvn2),
                  full(wq), full(wk), full(wvt),
                  pl.BlockSpec((3, tm, LANES), lambda i: (0, i % tiles_per_batch, 0))],
        out_specs=[
            pl.BlockSpec((tm, N_HEADS * LANES), lambda i: (i, 0)),
            pl.BlockSpec((tm, N_HEADS * LANES), lambda i: (i, 0)),
            pl.BlockSpec((N_HEADS * MLA_V, tm), lambda i: (0, i)),
        ],
        out_shape=[
            jax.ShapeDtypeStruct((M, N_HEADS * LANES), BF16),
            jax.ShapeDtypeStruct((M, N_HEADS * LANES), BF16),
            jax.ShapeDtypeStruct((N_HEADS * MLA_V, M), BF16),
        ],
        compiler_params=_params("arbitrary"),
        name="mla_proj",
    )(h2, g2, wd, qn2, kvn2, wq, wk, wvt, tables)


def _forget_cumsum_kernel(f_ref, b_ref, sel_ref, kf_ref, frow_ref, carry, *, tile):
    t = pl.program_id(1)

    @pl.when(t == 0)
    def _():
        carry[...] = jnp.zeros_like(carry)

    row_in = lax.broadcasted_iota(jnp.int32, (tile, 1), 0)
    row = t * tile + row_in
    x = jnp.where(row >= ROW0, _log_sigmoid_neg(-(f_ref[0] + b_ref[...])), 0.0)
    shift = 1
    while shift < tile:
        x = x + jnp.where(row_in >= shift, pltpu.roll(x, shift, 0), 0.0)
        shift *= 2
    x = x + carry[...]
    carry[...] = x[tile - 1:tile, :]
    frow_ref[0] = x.T[:N_HEADS, :]
    rest = -x * LOG2E
    kf = jnp.zeros(kf_ref.shape[1:], F32)
    for piece in range(DECAY_PIECES):
        part = rest.astype(BF16)
        rest = rest - part.astype(F32)
        kf = kf + _dot(part, sel_ref[piece])
    kf_ref[0] = kf.astype(BF16)


def _forget_cumsum(flog, b_lane):
    B, Lp, _ = flog.shape
    tile = ATT_TILE
    n_pairs = N_HEADS // 2
    head = lax.broadcasted_iota(jnp.int32, (DECAY_PIECES, LANES, n_pairs * LANES), 1)
    col = lax.broadcasted_iota(jnp.int32, (DECAY_PIECES, LANES, n_pairs * LANES), 2)
    piece = lax.broadcasted_iota(jnp.int32, (DECAY_PIECES, LANES, n_pairs * LANES), 0)
    sel = ((head < N_HEADS) & (col == (head // 2) * LANES + (head % 2) * DECAY_PIECES + piece)).astype(BF16)
    kern = functools.partial(_forget_cumsum_kernel, tile=tile)
    return pl.pallas_call(
        kern,
        grid=(B, Lp // tile),
        in_specs=[
            pl.BlockSpec((1, tile, LANES), lambda b, t: (b, t, 0)),
            pl.BlockSpec((1, LANES), lambda b, t: (0, 0)),
            pl.BlockSpec(sel.shape, lambda b, t: (0, 0, 0)),
        ],
        out_specs=[
            pl.BlockSpec((1, tile, n_pairs * LANES), lambda b, t: (b, t, 0)),
            pl.BlockSpec((1, N_HEADS, tile), lambda b, t: (b, 0, t)),
        ],
        out_shape=[
            jax.ShapeDtypeStruct((B, Lp, n_pairs * LANES), BF16),
            jax.ShapeDtypeStruct((B, N_HEADS, Lp), F32),
        ],
        scratch_shapes=[pltpu.VMEM((1, LANES), F32)],
        compiler_params=_params("arbitrary", "arbitrary"),
        name="forget_cumsum",
    )(flog, b_lane, sel)


def _final_norm_kernel(h_ref, g_ref, o_ref):
    o_ref[0] = _rmsnorm(h_ref[0], g_ref[...])


def _final_norm(h, g, seq):
    B, Lp, D = h.shape
    tile = PAD_FRONT
    skip = PAD_FRONT // tile
    return pl.pallas_call(
        _final_norm_kernel,
        grid=(B, seq // tile),
        in_specs=[
            pl.BlockSpec((1, tile, D), lambda b, t: (b, t + skip, 0)),
            pl.BlockSpec((1, D), lambda b, t: (0, 0)),
        ],
        out_specs=pl.BlockSpec((1, tile, D), lambda b, t: (b, t, 0)),
        out_shape=jax.ShapeDtypeStruct((B, seq, D), F32),
        compiler_params=_params("arbitrary", "arbitrary"),
        name="final_norm",
    )(h, g.reshape(1, D))


def _pad_heads(w, per_head):
    K = w.shape[0]
    w3 = w.reshape(K, N_HEADS, per_head)
    w3 = jnp.pad(w3, ((0, 0), (0, 0), (0, LANES - per_head)))
    return w3.reshape(K, N_HEADS * LANES)


def _mla_layer(h, g, w_down, q_norm, kv_norm, w_uq, w_ukv, w_o):
    B, Lp, D = h.shape
    M = B * Lp
    n_lat = MLA_Q_RANK + MLA_KV_RANK
    wd_rope = jnp.pad(w_down[:, n_lat:], ((0, 0), (MLA_NOPE, LANES - MLA_NOPE - MLA_ROPE)))
    wd_p = jnp.concatenate([w_down[:, :n_lat], wd_rope], axis=1).astype(BF16)
    wq_p = _pad_heads(w_uq, MLA_NOPE + MLA_ROPE).astype(BF16)
    wkv3 = w_ukv.reshape(MLA_KV_RANK, N_HEADS, MLA_NOPE + MLA_V)
    wk_p = _pad_heads(wkv3[:, :, :MLA_NOPE].reshape(MLA_KV_RANK, -1), MLA_NOPE).astype(BF16)
    wvt_p = wkv3[:, :, MLA_NOPE:].reshape(MLA_KV_RANK, N_HEADS * MLA_V).T.astype(BF16)
    inv = ROPE_THETA ** (-jnp.arange(0, MLA_ROPE, 2, dtype=F32) / MLA_ROPE)
    inv_lane = jnp.concatenate([jnp.zeros((MLA_NOPE,), F32), inv, inv,
                                jnp.zeros((LANES - MLA_NOPE - MLA_ROPE,), F32)]).reshape(1, LANES)
    q, k, vt = _mla_proj(h.reshape(M, D), g, wd_p, q_norm, kv_norm, wq_p, wk_p, wvt_p, inv_lane, Lp)
    o = _softmax_attention(q.reshape(B, Lp, -1), k.reshape(B, Lp, -1), vt,
                           width=2 * LANES, n_pairs=N_HEADS // 2)
    return o.reshape(M, -1), w_o.astype(BF16)


def kernel(x, meta, norm_mix, norm_ffn, pool_w, pool_scale, sb_w_qkv, sb_w_o, mla_w_down, mla_q_norm,
           mla_kv_norm, mla_w_uq, mla_w_ukv, mla_w_o, fox_w_qkvf, fox_b_f, fox_w_o, ffn_w_gate, ffn_w_up,
           ffn_w_down, final_norm):
    B, S, D = x.shape
    assert D == D_MODEL and S % ATT_TILE == 0 and PAD_FRONT % ATT_TILE == 0 and PAD_FRONT % SB_TILE == 0
    Lp = S + PAD_FRONT
    M = B * Lp
    HD = N_HEADS * HEAD_DIM
    n_pairs = N_HEADS // 2

    def sb_mixer(h, i, j):
        qk, vt = _qk_vt_proj(h.reshape(M, D), norm_mix[i], sb_w_qkv[j][:, :2 * HD].astype(BF16),
                             sb_w_qkv[j][:, 2 * HD:].T.astype(BF16), q_scale=HEAD_DIM ** -0.5, tn=HD)
        o = _sb_attention(qk.reshape(B, Lp, 2 * HD), vt, n_pairs=n_pairs)
        return o.reshape(M, HD), sb_w_o[j].astype(BF16)

    def mla_mixer(h, i, j):
        return _mla_layer(h, norm_mix[i], mla_w_down[j], mla_q_norm[j], mla_kv_norm[j], mla_w_uq[j],
                          mla_w_ukv[j], mla_w_o[j])

    def fox_mixer(h, i, j):
        wf = jnp.pad(fox_w_qkvf[j][:, 3 * HD:], ((0, 0), (0, LANES - N_HEADS))).astype(BF16)
        qk, vt, flog = _qk_vt_proj(h.reshape(M, D), norm_mix[i], fox_w_qkvf[j][:, :2 * HD].astype(BF16),
                                   fox_w_qkvf[j][:, 2 * HD:3 * HD].T.astype(BF16), wf,
                                   q_scale=HEAD_DIM ** -0.5 * LOG2E, tn=HD)
        b_lane = jnp.pad(fox_b_f[j].astype(F32), (0, LANES - N_HEADS)).reshape(1, LANES)
        kf, frow = _forget_cumsum(flog.reshape(B, Lp, LANES), b_lane)
        qk = qk.reshape(B, Lp, 2 * HD)
        o = _softmax_attention(qk, qk, vt, width=LANES, n_pairs=n_pairs, offs=(0, n_pairs), kf=kf, frow=frow)
        return o.reshape(M, HD), fox_w_o[j].astype(BF16)

    mixers = (None, sb_mixer, mla_mixer, fox_mixer)
    h = None
    for i in range(norm_mix.shape[0]):
        m, j = i % len(mixers), i // len(mixers)
        ffn_w = (norm_ffn[i], ffn_w_gate[i].astype(BF16), ffn_w_up[i].astype(BF16), ffn_w_down[i].astype(BF16))
        if m == 0:
            assert i == 0, "the pooling mixer doubles as the layout stage and must come first"
            h = _pool_layer(x, meta, norm_mix[i], pool_w[j], pool_scale[j])
            h = _ffn(h.reshape(M, D), *ffn_w).reshape(B, Lp, D)
        else:
            o2, wo = mixers[m](h, i, j)
            h = _ffn(h.reshape(M, D), *ffn_w, o2=o2, wo=wo).reshape(B, Lp, D)
    return _final_norm(h, final_norm, S)
```

```python
import functools

import jax
import jax.numpy as jnp
from jax import lax
from jax.experimental import pallas as pl
from jax.experimental.pallas import tpu as pltpu

F32 = jnp.float32
BF16 = jnp.bfloat16

D_MODEL = 1024
N_META = 16
EPS = 1e-6
POOL_WINDOWS = (2, 4, 8, 16)
POOL_GROUP = D_MODEL // len(POOL_WINDOWS)
N_HEADS = 16
HEAD_DIM = 64
MLA_Q_RANK = 384
MLA_KV_RANK = 256
MLA_NOPE = 64
MLA_ROPE = 32
MLA_V = 64
ROPE_THETA = 10000.0

LANES = 128
PAD_FRONT = 256
ROW0 = PAD_FRONT - N_META
HALO = max(POOL_WINDOWS)
MASK_VALUE = -1e30
LOG2E = 1.4426950408889634
DECAY_PIECES = 3
ACC_ROWS = LANES + 16
DEAD_LOG_WEIGHT = -104.0
VMEM_LIMIT = 56 * 1024 * 1024

ROW_TILE = 1024
ATT_TILE = 256
ATT_GROUP = 4
SB_TILE = 128
SB_LEAD = 3
SB_GROUP = 4
PROJ_TILE = 512
FF_TILE = 256


def _divisor_tile(n, target):
    best = None
    for t in range(8, min(n, target) + 1, 8):
        if n % t == 0:
            best = t
    assert best is not None, (n, target)
    return best


def _params(*sem):
    return pltpu.CompilerParams(dimension_semantics=sem, vmem_limit_bytes=VMEM_LIMIT)


def _rmsnorm(x, g):
    return x * lax.rsqrt(jnp.mean(x * x, axis=-1, keepdims=True) + EPS) * g


def _log_sigmoid_neg(z):
    return -(jnp.maximum(z, 0.0) + jnp.log(1.0 + jnp.exp(-jnp.abs(z))))


def _dot(a, b):
    return jnp.dot(a, b, preferred_element_type=F32)


def _dot_nt(a, b):
    return lax.dot_general(a, b, (((1,), (1,)), ((), ())), preferred_element_type=F32)


def _pool_kernel(x_ref, meta_ref, g_ref, w_ref, sc_ref, o_ref, abuf, *, tile):
    t = pl.program_id(1)
    front = jnp.concatenate([jnp.zeros((ROW0, D_MODEL), F32), meta_ref[...]], axis=0)
    h = jnp.where(t == 0, front, x_ref[0])
    row = t * tile + lax.broadcasted_iota(jnp.int32, (tile, 1), 0)
    a = jnp.where(row >= ROW0, _rmsnorm(h, g_ref[...]), 0.0)

    @pl.when(t == 0)
    def _():
        abuf[0:HALO, :] = jnp.zeros((HALO, D_MODEL), F32)

    @pl.when(t > 0)
    def _():
        abuf[0:HALO, :] = abuf[tile:tile + HALO, :]

    abuf[HALO:HALO + tile, :] = a
    pos1 = row - ROW0 + 1
    outs = []
    for g, win in enumerate(POOL_WINDOWS):
        cols = slice(g * POOL_GROUP, (g + 1) * POOL_GROUP)
        xg = a[:, cols]
        acc = xg
        for j in range(1, win):
            acc = acc + abuf[HALO - j:HALO - j + tile, cols]
        cnt = jnp.clip(pos1, 1, win).astype(F32)
        pooled = acc / cnt - xg
        outs.append(_dot(pooled.astype(BF16), w_ref[g]))
    mixed = jnp.concatenate(outs, axis=1) * sc_ref[...]
    o_ref[0] = h + mixed


def _pool_layer(x, meta, g, w, sc):
    B, S, D = x.shape
    tile = PAD_FRONT
    Lp = S + PAD_FRONT
    kern = functools.partial(_pool_kernel, tile=tile)
    return pl.pallas_call(
        kern,
        grid=(B, Lp // tile),
        in_specs=[
            pl.BlockSpec((1, tile, D), lambda b, t: (b, jnp.maximum(t - 1, 0), 0)),
            pl.BlockSpec((N_META, D), lambda b, t: (0, 0)),
            pl.BlockSpec((1, D), lambda b, t: (0, 0)),
            pl.BlockSpec((len(POOL_WINDOWS), POOL_GROUP, POOL_GROUP), lambda b, t: (0, 0, 0)),
            pl.BlockSpec((1, D), lambda b, t: (0, 0)),
        ],
        out_specs=pl.BlockSpec((1, tile, D), lambda b, t: (b, t, 0)),
        out_shape=jax.ShapeDtypeStruct((B, Lp, D), F32),
        scratch_shapes=[pltpu.VMEM((tile + HALO, D), F32)],
        compiler_params=_params("arbitrary", "arbitrary"),
        name="pool_layer",
    )(x, meta.astype(x.dtype), g.reshape(1, D), w.astype(BF16), sc.reshape(1, D))


def _qk_vt_proj_kernel(*refs, q_scale, q_cols, tn, has_gate):
    if has_gate:
        h_ref, g_ref, w_ref, wvt_ref, wf_ref, o_ref, vt_ref, f_ref = refs
    else:
        h_ref, g_ref, w_ref, wvt_ref, o_ref, vt_ref = refs
    a = _rmsnorm(h_ref[...], g_ref[...]).astype(BF16)
    for c in range(w_ref.shape[1] // tn):
        cols = slice(c * tn, (c + 1) * tn)
        part = _dot(a, w_ref[:, cols])
        if (c + 1) * tn <= q_cols:
            part = part * q_scale
        o_ref[:, cols] = part.astype(o_ref.dtype)
    vt_ref[...] = _dot_nt(wvt_ref[...], a).astype(vt_ref.dtype)
    if has_gate:
        f_ref[...] = _dot(a, wf_ref[...])


def _qk_vt_proj(h2, g, w_qk, w_vt, wf=None, *, q_scale, q_cols):
    M, D = h2.shape
    N = w_qk.shape[1]
    Dv = w_vt.shape[0]
    tn = PROJ_TILE
    assert N % tn == 0 and q_cols % tn == 0
    tm = _divisor_tile(M, ROW_TILE)
    has_gate = wf is not None
    resident = lambda shape: pl.BlockSpec(shape, lambda i: (0, 0), pipeline_mode=pl.Buffered(1))
    in_specs = [pl.BlockSpec((tm, D), lambda i: (i, 0)), resident((1, D)), resident((D, N)), resident((Dv, D))]
    out_specs = [pl.BlockSpec((tm, N), lambda i: (i, 0)), pl.BlockSpec((Dv, tm), lambda i: (0, i))]
    out_shape = [jax.ShapeDtypeStruct((M, N), BF16), jax.ShapeDtypeStruct((Dv, M), BF16)]
    args = [h2, g.reshape(1, D), w_qk, w_vt]
    if has_gate:
        in_specs.append(resident((D, LANES)))
        out_specs.append(pl.BlockSpec((tm, LANES), lambda i: (i, 0)))
        out_shape.append(jax.ShapeDtypeStruct((M, LANES), F32))
        args.append(wf)
    return pl.pallas_call(
        functools.partial(_qk_vt_proj_kernel, q_scale=q_scale, q_cols=q_cols, tn=tn, has_gate=has_gate),
        grid=(M // tm,),
        in_specs=in_specs,
        out_specs=out_specs,
        out_shape=out_shape,
        compiler_params=_params("arbitrary"),
        name="qk_vt_proj",
    )(*args)


def _ffn_kernel(*refs, tf, has_mix):
    if has_mix:
        h_ref, o_ref, wo_ref, g_ref, wg_ref, wu_ref, wd_ref, out_ref, acc_scr = refs
        h = h_ref[...] + _dot(o_ref[...], wo_ref[...])
    else:
        h_ref, g_ref, wg_ref, wu_ref, wd_ref, out_ref, acc_scr = refs
        h = h_ref[...]
    a = _rmsnorm(h, g_ref[...]).astype(BF16)
    for c in range(wg_ref.shape[1] // tf):
        cols = slice(c * tf, (c + 1) * tf)
        gate = _dot(a, wg_ref[:, cols])
        up = _dot(a, wu_ref[:, cols])
        act = (gate * jax.nn.sigmoid(gate) * up).astype(BF16)
        part = _dot(act, wd_ref[cols, :])
        if c == 0:
            acc_scr[...] = part
        else:
            acc_scr[...] += part
    out_ref[...] = h + acc_scr[...]


def _ffn(h2, g, wg, wu, wd, o2=None, wo=None):
    M, D = h2.shape
    F = wg.shape[1]
    assert F % FF_TILE == 0
    tm = _divisor_tile(M, ROW_TILE)
    has_mix = o2 is not None
    resident = lambda shape: pl.BlockSpec(shape, lambda i: (0, 0), pipeline_mode=pl.Buffered(1))
    rows = lambda width: pl.BlockSpec((tm, width), lambda i: (i, 0))
    in_specs, args = [rows(D)], [h2]
    if has_mix:
        in_specs += [rows(o2.shape[1]), resident(wo.shape)]
        args += [o2, wo]
    in_specs += [resident((1, D)), resident((D, F)), resident((D, F)), resident((F, D))]
    args += [g.reshape(1, D), wg, wu, wd]
    return pl.pallas_call(
        functools.partial(_ffn_kernel, tf=FF_TILE, has_mix=has_mix),
        grid=(M // tm,),
        in_specs=in_specs,
        out_specs=rows(D),
        out_shape=jax.ShapeDtypeStruct((M, D), F32),
        scratch_shapes=[pltpu.VMEM((tm, D), F32)],
        compiler_params=_params("arbitrary"),
        name="ffn",
    )(*args)


def _split_heads(x, width):
    if width == 2 * LANES:
        return x[:, :LANES], x[:, LANES:]
    lane = lax.broadcasted_iota(jnp.int32, (1, LANES), 1)
    zero = jnp.zeros_like(x)
    return jnp.where(lane < HEAD_DIM, x, zero), jnp.where(lane >= HEAD_DIM, x, zero)


def _split_keys(x, width):
    if width == 2 * LANES:
        return x[:, :LANES], x[:, LANES:]
    return x, x


def _pair_values_t(vt):
    sub = lax.broadcasted_iota(jnp.int32, (LANES, 1), 0)
    zero = jnp.zeros_like(vt)
    return jnp.concatenate(
        [jnp.where(sub < HEAD_DIM, vt, zero), jnp.where(sub >= HEAD_DIM, vt, zero)], axis=1)


def _pair_rows(x0, x1):
    sub = lax.broadcasted_iota(jnp.int32, (ACC_ROWS, 1), 0)
    return jnp.where((sub < HEAD_DIM) | (sub == LANES), x0, x1)


def _sum_rows(tile):
    sub = lax.broadcasted_iota(jnp.int32, (ACC_ROWS - LANES, 2 * tile), 0)
    col = lax.broadcasted_iota(jnp.int32, (ACC_ROWS - LANES, 2 * tile), 1)
    return jnp.where(sub == col // tile, 1.0, 0.0).astype(BF16)


def _softmax_attn_kernel(*refs, tile, width, decay, group):
    if decay:
        q_ref, k_ref, vt_ref, kf_ref, frow_ref, o_ref = refs[:6]
    else:
        q_ref, k_ref, vt_ref, o_ref = refs[:4]
    m_scr, acc_scr, raw0, raw1, p0, p1, al0, al1 = refs[-8:]
    raw_s, p_s, al_s = (raw0, raw1), (p0, p1), (al0, al1)
    hg = pl.program_id(1)
    qi = pl.program_id(2)
    n_heads = 2 * group
    q = q_ref[0]
    qs = []
    for g in range(group):
        qs += list(_split_heads(q[:, g * width:(g + 1) * width], width))
    qcol = qi * tile + lax.broadcasted_iota(jnp.int32, (1, tile), 1)
    if decay:
        lane = lax.broadcasted_iota(jnp.int32, (1, LANES), 1)
        for h in range(n_heads):
            e = h % 2
            ones = jnp.where((lane >= e * DECAY_PIECES) & (lane < (e + 1) * DECAY_PIECES), 1.0, 0.0)
            qs[h] = jnp.concatenate([qs[h], jnp.broadcast_to(ones.astype(BF16), (tile, LANES))], axis=1)
        head0 = n_heads * hg
        fq = tuple(frow_ref[0, pl.ds(head0 + h, 1), pl.ds(pl.multiple_of(qi * tile, tile), tile)] * LOG2E
                   for h in range(n_heads))

    sum_rows = _sum_rows(tile)

    def meta_keys():
        rows = slice(ROW0, PAD_FRONT)
        k = k_ref[0, rows, :]
        key = ROW0 + lax.broadcasted_iota(jnp.int32, (N_META, 1), 0)
        valid = key <= qcol
        ps = []
        for h in range(n_heads):
            g, e = divmod(h, 2)
            kk = _split_keys(k[:, g * width:(g + 1) * width], width)[e]
            if decay:
                kk = jnp.concatenate([kk, kf_ref[0, rows, g * LANES:(g + 1) * LANES]], axis=1)
            raw = jnp.where(valid, _dot_nt(kk, qs[h]), MASK_VALUE)
            m_new = jnp.max(raw, axis=0, keepdims=True)
            if decay:
                m_new = m_new + fq[h]
                ps.append(jnp.exp2(raw - (m_new - fq[h])).astype(BF16))
            else:
                ps.append(jnp.exp2(raw - m_new).astype(BF16))
            m_scr[h] = m_new
        vt = vt_ref[:, rows]
        meta_sums = _sum_rows(N_META)
        for g in range(group):
            lhs = jnp.concatenate([_pair_values_t(vt[g * LANES:(g + 1) * LANES]), meta_sums], axis=0)
            acc_scr[g * ACC_ROWS:(g + 1) * ACC_ROWS, :] = _dot(lhs, jnp.concatenate(ps[2 * g:2 * g + 2], axis=0))

    def scores(t, s):
        start = pl.multiple_of(t * tile, tile)
        k = k_ref[0, pl.ds(start, tile), :]
        if decay:
            kf = kf_ref[0, pl.ds(start, tile), :]
        for g in range(group):
            ks = _split_keys(k[:, g * width:(g + 1) * width], width)
            if decay:
                ks = tuple(jnp.concatenate([kk, kf[:, g * LANES:(g + 1) * LANES]], axis=1) for kk in ks)
            for e in range(2):
                raw_s[s][2 * g + e] = _dot_nt(ks[e], qs[2 * g + e])

    def weights(t, s, masked):
        start = pl.multiple_of(t * tile, tile)
        if masked:
            key = start + lax.broadcasted_iota(jnp.int32, (tile, 1), 0)
            valid = key <= qcol
        for h in range(n_heads):
            raw = raw_s[s][h]
            m_old = m_scr[h]
            if masked:
                raw = jnp.where(valid, raw, MASK_VALUE)
            if decay:
                m_new = jnp.maximum(m_old, jnp.max(raw, axis=0, keepdims=True) + fq[h])
                p = jnp.exp2(raw - (m_new - fq[h]))
            else:
                m_new = jnp.maximum(m_old, jnp.max(raw, axis=0, keepdims=True))
                p = jnp.exp2(raw - m_new)
            m_scr[h] = m_new
            p_s[s][h * tile:(h + 1) * tile, :] = p.astype(BF16)
            al_s[s][h] = jnp.exp2(m_old - m_new)

    def accumulate(t, s):
        vt = vt_ref[:, pl.ds(pl.multiple_of(t * tile, tile), tile)]
        for g in range(group):
            lhs = jnp.concatenate([_pair_values_t(vt[g * LANES:(g + 1) * LANES]), sum_rows], axis=0)
            pv = _dot(lhs, p_s[s][2 * g * tile:(2 * g + 2) * tile, :])
            rows = slice(g * ACC_ROWS, (g + 1) * ACC_ROWS)
            acc_scr[rows, :] = _pair_rows(al_s[s][2 * g], al_s[s][2 * g + 1]) * acc_scr[rows, :] + pv

    meta_keys()

    @pl.when(qi > 0)
    def _():
        p_s[0][...] = jnp.zeros_like(p_s[0])
        al_s[0][...] = jnp.ones_like(al_s[0])
        scores(1, 1)
        n_steady = qi - 1

        def pair(i, c):
            t = 2 * i + 1
            scores(t + 1, 0)
            weights(t, 1, False)
            accumulate(t - 1, 0)
            scores(t + 2, 1)
            weights(t + 1, 0, False)
            accumulate(t, 1)
            return c

        lax.fori_loop(0, n_steady // 2, pair, 0)

        @pl.when(n_steady % 2 == 1)
        def _():
            t = qi - 1
            scores(qi, 0)
            weights(t, 1, False)
            accumulate(t - 1, 0)
            weights(qi, 0, True)
            accumulate(t, 1)
            accumulate(qi, 0)

        @pl.when(n_steady % 2 == 0)
        def _():
            weights(qi, 1, True)
            accumulate(qi - 1, 0)
            accumulate(qi, 1)

    sub = lax.broadcasted_iota(jnp.int32, (LANES, 1), 0)
    for g in range(group):
        acc = acc_scr[g * ACC_ROWS:(g + 1) * ACC_ROWS, :]
        total = jnp.where(sub < HEAD_DIM, acc[LANES:LANES + 1], acc[LANES + 1:LANES + 2])
        o_ref[0, :, g * LANES:(g + 1) * LANES] = (acc[:LANES] / total).T.astype(o_ref.dtype)


def _softmax_attention(q, k, vt, *, width, n_pairs, offs=(0, 0), kf=None, frow=None):
    B, Lp, _ = q.shape
    qo, ko = offs
    tile = ATT_TILE
    group = ATT_GROUP
    assert n_pairs % group == 0 and qo % group == 0 and ko % group == 0
    n_heads = 2 * group
    decay = kf is not None
    kern = functools.partial(_softmax_attn_kernel, tile=tile, width=width, decay=decay, group=group)
    in_specs = [
        pl.BlockSpec((1, tile, group * width), lambda b, h, i: (b, i, qo // group + h)),
        pl.BlockSpec((1, Lp, group * width), lambda b, h, i: (b, 0, ko // group + h), pipeline_mode=pl.Buffered(1)),
        pl.BlockSpec((group * LANES, Lp), lambda b, h, i: (h, b), pipeline_mode=pl.Buffered(1)),
    ]
    args = [q, k, vt]
    if decay:
        in_specs += [
            pl.BlockSpec((1, Lp, group * LANES), lambda b, h, i: (b, 0, h), pipeline_mode=pl.Buffered(1)),
            pl.BlockSpec((1, N_HEADS, Lp), lambda b, h, i: (b, 0, 0)),
        ]
        args += [kf, frow]
    return pl.pallas_call(
        kern,
        grid=(B, n_pairs // group, Lp // tile),
        in_specs=in_specs,
        out_specs=pl.BlockSpec((1, tile, group * LANES), lambda b, h, i: (b, i, h)),
        out_shape=jax.ShapeDtypeStruct((B, Lp, n_pairs * LANES), BF16),
        scratch_shapes=[
            pltpu.VMEM((n_heads, 1, tile), F32),
            pltpu.VMEM((group * ACC_ROWS, tile), F32),
            pltpu.VMEM((n_heads, tile, tile), F32),
            pltpu.VMEM((n_heads, tile, tile), F32),
            pltpu.VMEM((n_heads * tile, tile), BF16),
            pltpu.VMEM((n_heads * tile, tile), BF16),
            pltpu.VMEM((n_heads, 1, tile), F32),
            pltpu.VMEM((n_heads, 1, tile), F32),
        ],
        compiler_params=_params("arbitrary", "arbitrary", "arbitrary"),
        name="fox_attention" if decay else "mla_attention",
    )(*args)


def _sb_attn_kernel(q_ref, k_ref, vt_ref, tri_ref, o_ref, r_scr, acc_scr, *, tile, group):
    qi = pl.program_id(2)
    n_heads = 2 * group
    first = ROW0 // tile
    q = q_ref[0]
    qs = []
    for g in range(group):
        qs += list(_split_heads(q[:, g * LANES:(g + 1) * LANES], LANES))
    qcol = qi * tile + lax.broadcasted_iota(jnp.int32, (1, tile), 1)
    r_scr[...] = jnp.zeros_like(r_scr)
    acc_scr[...] = jnp.zeros_like(acc_scr)
    tri2 = tri_ref[...]

    def walk(tiles):
        ks, vts, valids = [], [], []
        for t, masked in tiles:
            start = pl.multiple_of(t * tile, tile)
            ks.append(k_ref[0, pl.ds(start, tile), :])
            vts.append(vt_ref[:, pl.ds(start, tile)])
            key = start + lax.broadcasted_iota(jnp.int32, (tile, 1), 0)
            valids.append((key >= ROW0) & (key < qcol) if masked else None)
        n = len(tiles)
        zs = [[_dot_nt(ks[i][:, (h // 2) * LANES:(h // 2 + 1) * LANES], qs[h]) for h in range(n_heads)]
              for i in range(n)]
        lks = [[None] * n_heads for _ in range(n)]
        for i in range(n):
            for h in range(n_heads):
                lk = _log_sigmoid_neg(zs[i][h])
                lks[i][h] = lk if valids[i] is None else jnp.where(valids[i], lk, 0.0)
        laters = [[None] * n_heads for _ in range(n)]
        for i in range(n):
            for h in range(n_heads):
                lk_hi = lks[i][h].astype(BF16)
                lk_lo = (lks[i][h] - lk_hi.astype(F32)).astype(BF16)
                laters[i][h] = _dot(tri2, jnp.concatenate([lk_hi, lk_lo], axis=0))
        ws = [[None] * n_heads for _ in range(n)]
        for h in range(n_heads):
            r = r_scr[h]
            for i in range(n):
                w = jnp.exp(lks[i][h] + zs[i][h] + (laters[i][h] + r))
                ws[i][h] = (w if valids[i] is None else jnp.where(valids[i], w, 0.0)).astype(BF16)
                r = r + jnp.sum(lks[i][h], axis=0, keepdims=True)
            r_scr[h] = r
        for g in range(group):
            rows = slice(g * LANES, (g + 1) * LANES)
            lhs = jnp.concatenate([_pair_values_t(vts[i][rows]) for i in range(n)], axis=1)
            rhs = jnp.concatenate([ws[i][h] for i in range(n) for h in (2 * g, 2 * g + 1)], axis=0)
            acc_scr[rows, :] += _dot(lhs, rhs)

    def live():
        return (jnp.max(r_scr[...]) > DEAD_LOG_WEIGHT).astype(jnp.int32)

    lead = qi - (SB_LEAD - 1) > first

    @pl.when(lead)
    def _():
        walk(tuple((qi - i, i == 0) for i in range(SB_LEAD)))

    @pl.when(jnp.logical_not(lead))
    def _():
        walk(((qi, True),))

    def body(c):
        walk(((c[0], False),))
        return c[0] - 1, live()

    t0 = jnp.where(lead, qi - SB_LEAD, qi - 1)
    _, alive = lax.while_loop(lambda c: (c[0] > first) & (c[1] > 0), body, (t0, live()))

    @pl.when((qi > first) & (alive > 0))
    def _():
        walk(((first, True),))

    for g in range(group):
        rows = slice(g * LANES, (g + 1) * LANES)
        o_ref[0, :, rows] = acc_scr[rows, :].T.astype(o_ref.dtype)


def _sb_attention(qk, vt, *, n_pairs):
    B, Lp, _ = qk.shape
    tile, group = SB_TILE, SB_GROUP
    assert n_pairs % group == 0
    n_groups = n_pairs // group
    width = group * LANES
    r = lax.broadcasted_iota(jnp.int32, (tile, tile), 0)
    c = lax.broadcasted_iota(jnp.int32, (tile, tile), 1)
    tri = (c > r).astype(BF16)
    tri2 = jnp.concatenate([tri, tri], axis=1)
    kern = functools.partial(_sb_attn_kernel, tile=tile, group=group)
    return pl.pallas_call(
        kern,
        grid=(B, n_groups, Lp // tile),
        in_specs=[
            pl.BlockSpec((1, tile, width), lambda b, h, i: (b, i, h)),
            pl.BlockSpec((1, Lp, width), lambda b, h, i: (b, 0, n_groups + h), pipeline_mode=pl.Buffered(1)),
            pl.BlockSpec((width, Lp), lambda b, h, i: (h, b), pipeline_mode=pl.Buffered(1)),
            pl.BlockSpec((tile, 2 * tile), lambda b, h, i: (0, 0)),
        ],
        out_specs=pl.BlockSpec((1, tile, width), lambda b, h, i: (b, i, h)),
        out_shape=jax.ShapeDtypeStruct((B, Lp, n_pairs * LANES), BF16),
        scratch_shapes=[
            pltpu.VMEM((2 * group, 1, tile), F32),
            pltpu.VMEM((width, tile), F32),
        ],
        compiler_params=_params("arbitrary", "arbitrary", "arbitrary"),
        name="sb_attention",
    )(qk, qk, vt, tri2)


def _rope_table_kernel(inv_ref, o_ref, *, tile):
    row = pl.program_id(0) * tile + lax.broadcasted_iota(jnp.int32, (tile, 1), 0)
    ang = (row - ROW0).astype(F32) * inv_ref[...]
    lane = lax.broadcasted_iota(jnp.int32, (1, LANES), 1)
    half = MLA_ROPE // 2
    in_lo = (lane >= MLA_NOPE) & (lane < MLA_NOPE + half)
    in_hi = (lane >= MLA_NOPE + half) & (lane < MLA_NOPE + MLA_ROPE)
    cos = jnp.cos(ang)
    sin = jnp.sin(ang)
    o_ref[0] = jnp.where(lane < MLA_NOPE, 1.0, jnp.where(in_lo | in_hi, cos, 0.0))
    o_ref[1] = jnp.where(in_lo, -sin, 0.0)
    o_ref[2] = jnp.where(in_hi, sin, 0.0)


def _rope_tables(rows, inv_lane):
    tile = _divisor_tile(rows, 512)
    return pl.pallas_call(
        functools.partial(_rope_table_kernel, tile=tile),
        grid=(rows // tile,),
        in_specs=[pl.BlockSpec((1, LANES), lambda i: (0, 0))],
        out_specs=pl.BlockSpec((3, tile, LANES), lambda i: (0, i, 0)),
        out_shape=jax.ShapeDtypeStruct((3, rows, LANES), F32),
        compiler_params=_params("arbitrary"),
        name="rope_tables",
    )(inv_lane)


def _mla_proj_kernel(h_ref, g_ref, wd_ref, qn_ref, kvn_ref, wq_ref, wk_ref, wvt_ref, tab_ref,
                     q_ref, k_ref, vt_ref, *, q_scale):
    a = _rmsnorm(h_ref[...], g_ref[...]).astype(BF16)
    down = _dot(a, wd_ref[...])
    cq = _rmsnorm(down[:, :MLA_Q_RANK], qn_ref[...]).astype(BF16)
    ckv = _rmsnorm(down[:, MLA_Q_RANK:MLA_Q_RANK + MLA_KV_RANK], kvn_ref[...]).astype(BF16)
    kr = down[:, MLA_Q_RANK + MLA_KV_RANK:]
    c_tab, s_lo, s_hi = tab_ref[0], tab_ref[1], tab_ref[2]
    half = MLA_ROPE // 2

    def rope(x):
        return (x * c_tab + pltpu.roll(x, LANES - half, 1) * s_lo + pltpu.roll(x, half, 1) * s_hi)

    q = _dot(cq, wq_ref[...])
    kn = _dot(ckv, wk_ref[...])
    vt_ref[...] = _dot_nt(wvt_ref[...], ckv).astype(BF16)
    kr_rot = rope(kr)
    for hd in range(N_HEADS):
        cols = slice(hd * LANES, (hd + 1) * LANES)
        q_ref[:, cols] = (rope(q[:, cols]) * q_scale).astype(BF16)
        k_ref[:, cols] = (kn[:, cols] + kr_rot).astype(BF16)


def _mla_proj(h2, g, wd, qn, kvn, wq, wk, wvt, inv_lane, rows_per_batch):
    M, D = h2.shape
    tm = _divisor_tile(rows_per_batch, 512)
    tiles_per_batch = rows_per_batch // tm
    tables = _rope_tables(rows_per_batch, inv_lane)
    kern = functools.partial(_mla_proj_kernel, q_scale=(MLA_NOPE + MLA_ROPE) ** -0.5 * LOG2E)
    full = lambda a: pl.BlockSpec(a.shape, lambda i: (0,) * a.ndim)
    g2, qn2, kvn2 = g.reshape(1, D), qn.reshape(1, -1), kvn.reshape(1, -1)
    return pl.pallas_call(
        kern,
        grid=(M // tm,),
        in_specs=[pl.BlockSpec((tm, D), lambda i: (i, 0)), full(g2), full(wd), full(qn2), full(kvn2),
                  full(wq), full(wk), full(wvt),
                  pl.BlockSpec((3, tm, LANES), lambda i: (0, i % tiles_per_batch, 0))],
        out_specs=[
            pl.BlockSpec((tm, N_HEADS * LANES), lambda i: (i, 0)),
            pl.BlockSpec((tm, N_HEADS * LANES), lambda i: (i, 0)),
            pl.BlockSpec((N_HEADS * MLA_V, tm), lambda i: (0, i)),
        ],
        out_shape=[
            jax.ShapeDtypeStruct((M, N_HEADS * LANES), BF16),
            jax.ShapeDtypeStruct((M, N_HEADS * LANES), BF16),
            jax.ShapeDtypeStruct((N_HEADS * MLA_V, M), BF16),
        ],
        compiler_params=_params("arbitrary"),
        name="mla_proj",
    )(h2, g2, wd, qn2, kvn2, wq, wk, wvt, tables)


def _forget_cumsum_kernel(f_ref, b_ref, sel_ref, kf_ref, frow_ref, carry, *, tile):
    t = pl.program_id(1)

    @pl.when(t == 0)
    def _():
        carry[...] = jnp.zeros_like(carry)

    row_in = lax.broadcasted_iota(jnp.int32, (tile, 1), 0)
    row = t * tile + row_in
    x = jnp.where(row >= ROW0, _log_sigmoid_neg(-(f_ref[0] + b_ref[...])), 0.0)
    shift = 1
    while shift < tile:
        x = x + jnp.where(row_in >= shift, pltpu.roll(x, shift, 0), 0.0)
        shift *= 2
    x = x + carry[...]
    carry[...] = x[tile - 1:tile, :]
    frow_ref[0] = x.T[:N_HEADS, :]
    rest = -x * LOG2E
    kf = jnp.zeros(kf_ref.shape[1:], F32)
    for piece in range(DECAY_PIECES):
        part = rest.astype(BF16)
        rest = rest - part.astype(F32)
        kf = kf + _dot(part, sel_ref[piece])
    kf_ref[0] = kf.astype(BF16)


def _forget_cumsum(flog, b_lane):
    B, Lp, _ = flog.shape
    tile = ATT_TILE
    n_pairs = N_HEADS // 2
    head = lax.broadcasted_iota(jnp.int32, (DECAY_PIECES, LANES, n_pairs * LANES), 1)
    col = lax.broadcasted_iota(jnp.int32, (DECAY_PIECES, LANES, n_pairs * LANES), 2)
    piece = lax.broadcasted_iota(jnp.int32, (DECAY_PIECES, LANES, n_pairs * LANES), 0)
    sel = ((head < N_HEADS) & (col == (head // 2) * LANES + (head % 2) * DECAY_PIECES + piece)).astype(BF16)
    kern = functools.partial(_forget_cumsum_kernel, tile=tile)
    return pl.pallas_call(
        kern,
        grid=(B, Lp // tile),
        in_specs=[
            pl.BlockSpec((1, tile, LANES), lambda b, t: (b, t, 0)),
            pl.BlockSpec((1, LANES), lambda b, t: (0, 0)),
            pl.BlockSpec(sel.shape, lambda b, t: (0, 0, 0)),
        ],
        out_specs=[
            pl.BlockSpec((1, tile, n_pairs * LANES), lambda b, t: (b, t, 0)),
            pl.BlockSpec((1, N_HEADS, tile), lambda b, t: (b, 0, t)),
        ],
        out_shape=[
            jax.ShapeDtypeStruct((B, Lp, n_pairs * LANES), BF16),
            jax.ShapeDtypeStruct((B, N_HEADS, Lp), F32),
        ],
        scratch_shapes=[pltpu.VMEM((1, LANES), F32)],
        compiler_params=_params("arbitrary", "arbitrary"),
        name="forget_cumsum",
    )(flog, b_lane, sel)


def _final_norm_kernel(h_ref, g_ref, o_ref):
    o_ref[0] = _rmsnorm(h_ref[0], g_ref[...])


def _final_norm(h, g, seq):
    B, Lp, D = h.shape
    tile = PAD_FRONT
    skip = PAD_FRONT // tile
    return pl.pallas_call(
        _final_norm_kernel,
        grid=(B, seq // tile),
        in_specs=[
            pl.BlockSpec((1, tile, D), lambda b, t: (b, t + skip, 0)),
            pl.BlockSpec((1, D), lambda b, t: (0, 0)),
        ],
        out_specs=pl.BlockSpec((1, tile, D), lambda b, t: (b, t, 0)),
        out_shape=jax.ShapeDtypeStruct((B, seq, D), F32),
        compiler_params=_params("arbitrary", "arbitrary"),
        name="final_norm",
    )(h, g.reshape(1, D))


def _pad_heads(w, per_head):
    K = w.shape[0]
    w3 = w.reshape(K, N_HEADS, per_head)
    w3 = jnp.pad(w3, ((0, 0), (0, 0), (0, LANES - per_head)))
    return w3.reshape(K, N_HEADS * LANES)


def _mla_layer(h, g, w_down, q_norm, kv_norm, w_uq, w_ukv, w_o):
    B, Lp, D = h.shape
    M = B * Lp
    n_lat = MLA_Q_RANK + MLA_KV_RANK
    wd_rope = jnp.pad(w_down[:, n_lat:], ((0, 0), (MLA_NOPE, LANES - MLA_NOPE - MLA_ROPE)))
    wd_p = jnp.concatenate([w_down[:, :n_lat], wd_rope], axis=1).astype(BF16)
    wq_p = _pad_heads(w_uq, MLA_NOPE + MLA_ROPE).astype(BF16)
    wkv3 = w_ukv.reshape(MLA_KV_RANK, N_HEADS, MLA_NOPE + MLA_V)
    wk_p = _pad_heads(wkv3[:, :, :MLA_NOPE].reshape(MLA_KV_RANK, -1), MLA_NOPE).astype(BF16)
    wvt_p = wkv3[:, :, MLA_NOPE:].reshape(MLA_KV_RANK, N_HEADS * MLA_V).T.astype(BF16)
    inv = ROPE_THETA ** (-jnp.arange(0, MLA_ROPE, 2, dtype=F32) / MLA_ROPE)
    inv_lane = jnp.concatenate([jnp.zeros((MLA_NOPE,), F32), inv, inv,
                                jnp.zeros((LANES - MLA_NOPE - MLA_ROPE,), F32)]).reshape(1, LANES)
    q, k, vt = _mla_proj(h.reshape(M, D), g, wd_p, q_norm, kv_norm, wq_p, wk_p, wvt_p, inv_lane, Lp)
    o = _softmax_attention(q.reshape(B, Lp, -1), k.reshape(B, Lp, -1), vt,
                           width=2 * LANES, n_pairs=N_HEADS // 2)
    return o.reshape(M, -1), w_o.astype(BF16)


def kernel(x, meta, norm_mix, norm_ffn, pool_w, pool_scale, sb_w_qkv, sb_w_o, mla_w_down, mla_q_norm,
           mla_kv_norm, mla_w_uq, mla_w_ukv, mla_w_o, fox_w_qkvf, fox_b_f, fox_w_o, ffn_w_gate, ffn_w_up,
           ffn_w_down, final_norm):
    B, S, D = x.shape
    assert D == D_MODEL and S % ATT_TILE == 0 and PAD_FRONT % ATT_TILE == 0 and PAD_FRONT % SB_TILE == 0
    Lp = S + PAD_FRONT
    M = B * Lp
    HD = N_HEADS * HEAD_DIM
    n_pairs = N_HEADS // 2

    def sb_mixer(h, i, j):
        qk, vt = _qk_vt_proj(h.reshape(M, D), norm_mix[i], sb_w_qkv[j][:, :2 * HD].astype(BF16),
                             sb_w_qkv[j][:, 2 * HD:].T.astype(BF16), q_scale=HEAD_DIM ** -0.5, q_cols=HD)
        o = _sb_attention(qk.reshape(B, Lp, 2 * HD), vt, n_pairs=n_pairs)
        return o.reshape(M, HD), sb_w_o[j].astype(BF16)

    def mla_mixer(h, i, j):
        return _mla_layer(h, norm_mix[i], mla_w_down[j], mla_q_norm[j], mla_kv_norm[j], mla_w_uq[j],
                          mla_w_ukv[j], mla_w_o[j])

    def fox_mixer(h, i, j):
        wf = jnp.pad(fox_w_qkvf[j][:, 3 * HD:], ((0, 0), (0, LANES - N_HEADS))).astype(BF16)
        qk, vt, flog = _qk_vt_proj(h.reshape(M, D), norm_mix[i], fox_w_qkvf[j][:, :2 * HD].astype(BF16),
                                   fox_w_qkvf[j][:, 2 * HD:3 * HD].T.astype(BF16), wf,
                                   q_scale=HEAD_DIM ** -0.5 * LOG2E, q_cols=HD)
        b_lane = jnp.pad(fox_b_f[j].astype(F32), (0, LANES - N_HEADS)).reshape(1, LANES)
        kf, frow = _forget_cumsum(flog.reshape(B, Lp, LANES), b_lane)
        qk = qk.reshape(B, Lp, 2 * HD)
        o = _softmax_attention(qk, qk, vt, width=LANES, n_pairs=n_pairs, offs=(0, n_pairs), kf=kf, frow=frow)
        return o.reshape(M, HD), fox_w_o[j].astype(BF16)

    mixers = (None, sb_mixer, mla_mixer, fox_mixer)
    h = None
    for i in range(norm_mix.shape[0]):
        m, j = i % len(mixers), i // len(mixers)
        ffn_w = (norm_ffn[i], ffn_w_gate[i].astype(BF16), ffn_w_up[i].astype(BF16), ffn_w_down[i].astype(BF16))
        if m == 0:
            assert i == 0, "the pooling mixer doubles as the layout stage and must come first"
            h = _pool_layer(x, meta, norm_mix[i], pool_w[j], pool_scale[j])
            h = _ffn(h.reshape(M, D), *ffn_w).reshape(B, Lp, D)
        else:
            o2, wo = mixers[m](h, i, j)
            h = _ffn(h.reshape(M, D), *ffn_w, o2=o2, wo=wo).reshape(B, Lp, D)
    return _final_norm(h, final_norm, S)
```

```python
import functools

import jax
import jax.numpy as jnp
from jax import lax
from jax.experimental import pallas as pl
from jax.experimental.pallas import tpu as pltpu

F32 = jnp.float32
BF16 = jnp.bfloat16

D_MODEL = 1024
N_META = 16
EPS = 1e-6
POOL_WINDOWS = (2, 4, 8, 16)
POOL_GROUP = D_MODEL // len(POOL_WINDOWS)
N_HEADS = 16
HEAD_DIM = 64
MLA_Q_RANK = 384
MLA_KV_RANK = 256
MLA_NOPE = 64
MLA_ROPE = 32
MLA_V = 64
ROPE_THETA = 10000.0

LANES = 128
PAD_FRONT = 256
ROW0 = PAD_FRONT - N_META
HALO = max(POOL_WINDOWS)
MASK_VALUE = -1e30
LOG2E = 1.4426950408889634
DECAY_PIECES = 3
ACC_ROWS = LANES + 16
DEAD_LOG_WEIGHT = -104.0
VMEM_LIMIT = 56 * 1024 * 1024

ROW_TILE = 1024
ATT_TILE = 256
ATT_GROUP = 4
SB_TILE = 128
SB_LEAD = 3
SB_GROUP = 8
PROJ_TILE = 512
FF_TILE = 256


def _divisor_tile(n, target):
    best = None
    for t in range(8, min(n, target) + 1, 8):
        if n % t == 0:
            best = t
    assert best is not None, (n, target)
    return best


def _params(*sem):
    return pltpu.CompilerParams(dimension_semantics=sem, vmem_limit_bytes=VMEM_LIMIT)


def _rmsnorm(x, g):
    return x * lax.rsqrt(jnp.mean(x * x, axis=-1, keepdims=True) + EPS) * g


def _log_sigmoid_neg(z):
    return -(jnp.maximum(z, 0.0) + jnp.log(1.0 + jnp.exp(-jnp.abs(z))))


def _dot(a, b):
    return jnp.dot(a, b, preferred_element_type=F32)


def _dot_nt(a, b):
    return lax.dot_general(a, b, (((1,), (1,)), ((), ())), preferred_element_type=F32)


def _pool_kernel(x_ref, meta_ref, g_ref, w_ref, sc_ref, o_ref, abuf, *, tile):
    t = pl.program_id(1)
    front = jnp.concatenate([jnp.zeros((ROW0, D_MODEL), F32), meta_ref[...]], axis=0)
    h = jnp.where(t == 0, front, x_ref[0])
    row = t * tile + lax.broadcasted_iota(jnp.int32, (tile, 1), 0)
    a = jnp.where(row >= ROW0, _rmsnorm(h, g_ref[...]), 0.0)

    @pl.when(t == 0)
    def _():
        abuf[0:HALO, :] = jnp.zeros((HALO, D_MODEL), F32)

    @pl.when(t > 0)
    def _():
        abuf[0:HALO, :] = abuf[tile:tile + HALO, :]

    abuf[HALO:HALO + tile, :] = a
    pos1 = row - ROW0 + 1
    outs = []
    for g, win in enumerate(POOL_WINDOWS):
        cols = slice(g * POOL_GROUP, (g + 1) * POOL_GROUP)
        acc = abuf[:, cols]
        shift = 1
        while shift < win:
            acc = acc + pltpu.roll(acc, shift, 0)
            shift *= 2
        xg = a[:, cols]
        cnt = jnp.clip(pos1, 1, win).astype(F32)
        pooled = acc[HALO:] / cnt - xg
        outs.append(_dot(pooled.astype(BF16), w_ref[g]))
    mixed = jnp.concatenate(outs, axis=1) * sc_ref[...]
    o_ref[0] = h + mixed


def _pool_layer(x, meta, g, w, sc):
    B, S, D = x.shape
    tile = PAD_FRONT
    Lp = S + PAD_FRONT
    kern = functools.partial(_pool_kernel, tile=tile)
    return pl.pallas_call(
        kern,
        grid=(B, Lp // tile),
        in_specs=[
            pl.BlockSpec((1, tile, D), lambda b, t: (b, jnp.maximum(t - 1, 0), 0)),
            pl.BlockSpec((N_META, D), lambda b, t: (0, 0)),
            pl.BlockSpec((1, D), lambda b, t: (0, 0)),
            pl.BlockSpec((len(POOL_WINDOWS), POOL_GROUP, POOL_GROUP), lambda b, t: (0, 0, 0)),
            pl.BlockSpec((1, D), lambda b, t: (0, 0)),
        ],
        out_specs=pl.BlockSpec((1, tile, D), lambda b, t: (b, t, 0)),
        out_shape=jax.ShapeDtypeStruct((B, Lp, D), F32),
        scratch_shapes=[pltpu.VMEM((tile + HALO, D), F32)],
        compiler_params=_params("arbitrary", "arbitrary"),
        name="pool_layer",
    )(x, meta.astype(x.dtype), g.reshape(1, D), w.astype(BF16), sc.reshape(1, D))


def _qk_vt_proj_kernel(*refs, q_scale, q_cols, tn, has_gate):
    if has_gate:
        h_ref, g_ref, w_ref, wvt_ref, wf_ref, o_ref, vt_ref, f_ref = refs
    else:
        h_ref, g_ref, w_ref, wvt_ref, o_ref, vt_ref = refs
    a = _rmsnorm(h_ref[...], g_ref[...]).astype(BF16)
    for c in range(w_ref.shape[1] // tn):
        cols = slice(c * tn, (c + 1) * tn)
        part = _dot(a, w_ref[:, cols])
        if (c + 1) * tn <= q_cols:
            part = part * q_scale
        o_ref[:, cols] = part.astype(o_ref.dtype)
    vt_ref[...] = _dot_nt(wvt_ref[...], a).astype(vt_ref.dtype)
    if has_gate:
        f_ref[...] = _dot(a, wf_ref[...])


def _qk_vt_proj(h2, g, w_qk, w_vt, wf=None, *, q_scale, q_cols):
    M, D = h2.shape
    N = w_qk.shape[1]
    Dv = w_vt.shape[0]
    tn = PROJ_TILE
    assert N % tn == 0 and q_cols % tn == 0
    tm = _divisor_tile(M, ROW_TILE)
    has_gate = wf is not None
    resident = lambda shape: pl.BlockSpec(shape, lambda i: (0, 0), pipeline_mode=pl.Buffered(1))
    in_specs = [pl.BlockSpec((tm, D), lambda i: (i, 0)), resident((1, D)), resident((D, N)), resident((Dv, D))]
    out_specs = [pl.BlockSpec((tm, N), lambda i: (i, 0)), pl.BlockSpec((Dv, tm), lambda i: (0, i))]
    out_shape = [jax.ShapeDtypeStruct((M, N), BF16), jax.ShapeDtypeStruct((Dv, M), BF16)]
    args = [h2, g.reshape(1, D), w_qk, w_vt]
    if has_gate:
        in_specs.append(resident((D, LANES)))
        out_specs.append(pl.BlockSpec((tm, LANES), lambda i: (i, 0)))
        out_shape.append(jax.ShapeDtypeStruct((M, LANES), F32))
        args.append(wf)
    return pl.pallas_call(
        functools.partial(_qk_vt_proj_kernel, q_scale=q_scale, q_cols=q_cols, tn=tn, has_gate=has_gate),
        grid=(M // tm,),
        in_specs=in_specs,
        out_specs=out_specs,
        out_shape=out_shape,
        compiler_params=_params("arbitrary"),
        name="qk_vt_proj",
    )(*args)


def _ffn_kernel(*refs, tf, has_mix):
    if has_mix:
        h_ref, o_ref, wo_ref, g_ref, wg_ref, wu_ref, wd_ref, out_ref, acc_scr = refs
        h = h_ref[...] + _dot(o_ref[...], wo_ref[...])
    else:
        h_ref, g_ref, wg_ref, wu_ref, wd_ref, out_ref, acc_scr = refs
        h = h_ref[...]
    a = _rmsnorm(h, g_ref[...]).astype(BF16)
    for c in range(wg_ref.shape[1] // tf):
        cols = slice(c * tf, (c + 1) * tf)
        gate = _dot(a, wg_ref[:, cols])
        up = _dot(a, wu_ref[:, cols])
        act = (gate * jax.nn.sigmoid(gate) * up).astype(BF16)
        part = _dot(act, wd_ref[cols, :])
        if c == 0:
            acc_scr[...] = part
        else:
            acc_scr[...] += part
    out_ref[...] = h + acc_scr[...]


def _ffn(h2, g, wg, wu, wd, o2=None, wo=None):
    M, D = h2.shape
    F = wg.shape[1]
    assert F % FF_TILE == 0
    tm = _divisor_tile(M, ROW_TILE)
    has_mix = o2 is not None
    resident = lambda shape: pl.BlockSpec(shape, lambda i: (0, 0), pipeline_mode=pl.Buffered(1))
    rows = lambda width: pl.BlockSpec((tm, width), lambda i: (i, 0))
    in_specs, args = [rows(D)], [h2]
    if has_mix:
        in_specs += [rows(o2.shape[1]), resident(wo.shape)]
        args += [o2, wo]
    in_specs += [resident((1, D)), resident((D, F)), resident((D, F)), resident((F, D))]
    args += [g.reshape(1, D), wg, wu, wd]
    return pl.pallas_call(
        functools.partial(_ffn_kernel, tf=FF_TILE, has_mix=has_mix),
        grid=(M // tm,),
        in_specs=in_specs,
        out_specs=rows(D),
        out_shape=jax.ShapeDtypeStruct((M, D), F32),
        scratch_shapes=[pltpu.VMEM((tm, D), F32)],
        compiler_params=_params("arbitrary"),
        name="ffn",
    )(*args)


def _split_heads(x, width):
    if width == 2 * LANES:
        return x[:, :LANES], x[:, LANES:]
    lane = lax.broadcasted_iota(jnp.int32, (1, LANES), 1)
    zero = jnp.zeros_like(x)
    return jnp.where(lane < HEAD_DIM, x, zero), jnp.where(lane >= HEAD_DIM, x, zero)


def _split_keys(x, width):
    if width == 2 * LANES:
        return x[:, :LANES], x[:, LANES:]
    return x, x


def _pair_values_t(vt):
    sub = lax.broadcasted_iota(jnp.int32, (LANES, 1), 0)
    zero = jnp.zeros_like(vt)
    return jnp.concatenate(
        [jnp.where(sub < HEAD_DIM, vt, zero), jnp.where(sub >= HEAD_DIM, vt, zero)], axis=1)


def _pair_rows(x0, x1):
    sub = lax.broadcasted_iota(jnp.int32, (ACC_ROWS, 1), 0)
    return jnp.where((sub < HEAD_DIM) | (sub == LANES), x0, x1)


def _sum_rows(tile):
    sub = lax.broadcasted_iota(jnp.int32, (ACC_ROWS - LANES, 2 * tile), 0)
    col = lax.broadcasted_iota(jnp.int32, (ACC_ROWS - LANES, 2 * tile), 1)
    return jnp.where(sub == col // tile, 1.0, 0.0).astype(BF16)


def _softmax_attn_kernel(*refs, tile, width, decay, group):
    if decay:
        q_ref, k_ref, vt_ref, kf_ref, frow_ref, o_ref = refs[:6]
    else:
        q_ref, k_ref, vt_ref, o_ref = refs[:4]
    m_scr, acc_scr, raw0, raw1, p0, p1, al0, al1 = refs[-8:]
    raw_s, p_s, al_s = (raw0, raw1), (p0, p1), (al0, al1)
    hg = pl.program_id(1)
    qi = pl.program_id(2)
    n_heads = 2 * group
    q = q_ref[0]
    qs = []
    for g in range(group):
        qs += list(_split_heads(q[:, g * width:(g + 1) * width], width))
    qcol = qi * tile + lax.broadcasted_iota(jnp.int32, (1, tile), 1)
    if decay:
        lane = lax.broadcasted_iota(jnp.int32, (1, LANES), 1)
        for h in range(n_heads):
            e = h % 2
            ones = jnp.where((lane >= e * DECAY_PIECES) & (lane < (e + 1) * DECAY_PIECES), 1.0, 0.0)
            qs[h] = jnp.concatenate([qs[h], jnp.broadcast_to(ones.astype(BF16), (tile, LANES))], axis=1)
        head0 = n_heads * hg
        fq = tuple(frow_ref[0, pl.ds(head0 + h, 1), pl.ds(pl.multiple_of(qi * tile, tile), tile)] * LOG2E
                   for h in range(n_heads))

    sum_rows = _sum_rows(tile)

    def meta_keys():
        rows = slice(ROW0, PAD_FRONT)
        k = k_ref[0, rows, :]
        key = ROW0 + lax.broadcasted_iota(jnp.int32, (N_META, 1), 0)
        valid = key <= qcol
        ps = []
        for h in range(n_heads):
            g, e = divmod(h, 2)
            kk = _split_keys(k[:, g * width:(g + 1) * width], width)[e]
            if decay:
                kk = jnp.concatenate([kk, kf_ref[0, rows, g * LANES:(g + 1) * LANES]], axis=1)
            raw = jnp.where(valid, _dot_nt(kk, qs[h]), MASK_VALUE)
            m_new = jnp.max(raw, axis=0, keepdims=True)
            if decay:
                m_new = m_new + fq[h]
                ps.append(jnp.exp2(raw - (m_new - fq[h])).astype(BF16))
            else:
                ps.append(jnp.exp2(raw - m_new).astype(BF16))
            m_scr[h] = m_new
        vt = vt_ref[:, rows]
        meta_sums = _sum_rows(N_META)
        for g in range(group):
            lhs = jnp.concatenate([_pair_values_t(vt[g * LANES:(g + 1) * LANES]), meta_sums], axis=0)
            acc_scr[g * ACC_ROWS:(g + 1) * ACC_ROWS, :] = _dot(lhs, jnp.concatenate(ps[2 * g:2 * g + 2], axis=0))

    def scores(t, s):
        start = pl.multiple_of(t * tile, tile)
        k = k_ref[0, pl.ds(start, tile), :]
        if decay:
            kf = kf_ref[0, pl.ds(start, tile), :]
        for g in range(group):
            ks = _split_keys(k[:, g * width:(g + 1) * width], width)
            if decay:
                ks = tuple(jnp.concatenate([kk, kf[:, g * LANES:(g + 1) * LANES]], axis=1) for kk in ks)
            for e in range(2):
                raw_s[s][2 * g + e] = _dot_nt(ks[e], qs[2 * g + e])

    def weights(t, s, masked):
        start = pl.multiple_of(t * tile, tile)
        if masked:
            key = start + lax.broadcasted_iota(jnp.int32, (tile, 1), 0)
            valid = key <= qcol
        for h in range(n_heads):
            raw = raw_s[s][h]
            m_old = m_scr[h]
            if masked:
                raw = jnp.where(valid, raw, MASK_VALUE)
            if decay:
                m_new = jnp.maximum(m_old, jnp.max(raw, axis=0, keepdims=True) + fq[h])
                p = jnp.exp2(raw - (m_new - fq[h]))
            else:
                m_new = jnp.maximum(m_old, jnp.max(raw, axis=0, keepdims=True))
                p = jnp.exp2(raw - m_new)
            m_scr[h] = m_new
            p_s[s][h * tile:(h + 1) * tile, :] = p.astype(BF16)
            al_s[s][h] = jnp.exp2(m_old - m_new)

    def accumulate(t, s):
        vt = vt_ref[:, pl.ds(pl.multiple_of(t * tile, tile), tile)]
        for g in range(group):
            lhs = jnp.concatenate([_pair_values_t(vt[g * LANES:(g + 1) * LANES]), sum_rows], axis=0)
            pv = _dot(lhs, p_s[s][2 * g * tile:(2 * g + 2) * tile, :])
            rows = slice(g * ACC_ROWS, (g + 1) * ACC_ROWS)
            acc_scr[rows, :] = _pair_rows(al_s[s][2 * g], al_s[s][2 * g + 1]) * acc_scr[rows, :] + pv

    meta_keys()

    @pl.when(qi > 0)
    def _():
        p_s[0][...] = jnp.zeros_like(p_s[0])
        al_s[0][...] = jnp.ones_like(al_s[0])
        scores(1, 1)
        n_steady = qi - 1

        def pair(i, c):
            t = 2 * i + 1
            scores(t + 1, 0)
            weights(t, 1, False)
            accumulate(t - 1, 0)
            scores(t + 2, 1)
            weights(t + 1, 0, False)
            accumulate(t, 1)
            return c

        lax.fori_loop(0, n_steady // 2, pair, 0)

        @pl.when(n_steady % 2 == 1)
        def _():
            t = qi - 1
            scores(qi, 0)
            weights(t, 1, False)
            accumulate(t - 1, 0)
            weights(qi, 0, True)
            accumulate(t, 1)
            accumulate(qi, 0)

        @pl.when(n_steady % 2 == 0)
        def _():
            weights(qi, 1, True)
            accumulate(qi - 1, 0)
            accumulate(qi, 1)

    sub = lax.broadcasted_iota(jnp.int32, (LANES, 1), 0)
    for g in range(group):
        acc = acc_scr[g * ACC_ROWS:(g + 1) * ACC_ROWS, :]
        total = jnp.where(sub < HEAD_DIM, acc[LANES:LANES + 1], acc[LANES + 1:LANES + 2])
        o_ref[0, :, g * LANES:(g + 1) * LANES] = (acc[:LANES] / total).T.astype(o_ref.dtype)


def _softmax_attention(q, k, vt, *, width, n_pairs, offs=(0, 0), kf=None, frow=None):
    B, Lp, _ = q.shape
    qo, ko = offs
    tile = ATT_TILE
    group = ATT_GROUP
    assert n_pairs % group == 0 and qo % group == 0 and ko % group == 0
    n_heads = 2 * group
    decay = kf is not None
    kern = functools.partial(_softmax_attn_kernel, tile=tile, width=width, decay=decay, group=group)
    in_specs = [
        pl.BlockSpec((1, tile, group * width), lambda b, h, i: (b, i, qo // group + h)),
        pl.BlockSpec((1, Lp, group * width), lambda b, h, i: (b, 0, ko // group + h), pipeline_mode=pl.Buffered(1)),
        pl.BlockSpec((group * LANES, Lp), lambda b, h, i: (h, b), pipeline_mode=pl.Buffered(1)),
    ]
    args = [q, k, vt]
    if decay:
        in_specs += [
            pl.BlockSpec((1, Lp, group * LANES), lambda b, h, i: (b, 0, h), pipeline_mode=pl.Buffered(1)),
            pl.BlockSpec((1, N_HEADS, Lp), lambda b, h, i: (b, 0, 0)),
        ]
        args += [kf, frow]
    return pl.pallas_call(
        kern,
        grid=(B, n_pairs // group, Lp // tile),
        in_specs=in_specs,
        out_specs=pl.BlockSpec((1, tile, group * LANES), lambda b, h, i: (b, i, h)),
        out_shape=jax.ShapeDtypeStruct((B, Lp, n_pairs * LANES), BF16),
        scratch_shapes=[
            pltpu.VMEM((n_heads, 1, tile), F32),
            pltpu.VMEM((group * ACC_ROWS, tile), F32),
            pltpu.VMEM((n_heads, tile, tile), F32),
            pltpu.VMEM((n_heads, tile, tile), F32),
            pltpu.VMEM((n_heads * tile, tile), BF16),
            pltpu.VMEM((n_heads * tile, tile), BF16),
            pltpu.VMEM((n_heads, 1, tile), F32),
            pltpu.VMEM((n_heads, 1, tile), F32),
        ],
        compiler_params=_params("arbitrary", "arbitrary", "arbitrary"),
        name="fox_attention" if decay else "mla_attention",
    )(*args)


def _sb_attn_kernel(q_ref, k_ref, vt_ref, tri_ref, o_ref, r_scr, acc_scr, *, tile, group):
    qi = pl.program_id(2)
    n_heads = 2 * group
    first = ROW0 // tile
    q = q_ref[0]
    qs = []
    for g in range(group):
        qs += list(_split_heads(q[:, g * LANES:(g + 1) * LANES], LANES))
    qcol = qi * tile + lax.broadcasted_iota(jnp.int32, (1, tile), 1)
    r_scr[...] = jnp.zeros_like(r_scr)
    acc_scr[...] = jnp.zeros_like(acc_scr)
    tri2 = tri_ref[...]

    def walk(tiles):
        ks, vts, valids = [], [], []
        for t, masked in tiles:
            start = pl.multiple_of(t * tile, tile)
            ks.append(k_ref[0, pl.ds(start, tile), :])
            vts.append(vt_ref[:, pl.ds(start, tile)])
            key = start + lax.broadcasted_iota(jnp.int32, (tile, 1), 0)
            valids.append((key >= ROW0) & (key < qcol) if masked else None)
        n = len(tiles)
        zs = [[_dot_nt(ks[i][:, (h // 2) * LANES:(h // 2 + 1) * LANES], qs[h]) for h in range(n_heads)]
              for i in range(n)]
        lks = [[None] * n_heads for _ in range(n)]
        for i in range(n):
            for h in range(n_heads):
                lk = _log_sigmoid_neg(zs[i][h])
                lks[i][h] = lk if valids[i] is None else jnp.where(valids[i], lk, 0.0)
        laters = [[None] * n_heads for _ in range(n)]
        for i in range(n):
            for h in range(n_heads):
                lk_hi = lks[i][h].astype(BF16)
                lk_lo = (lks[i][h] - lk_hi.astype(F32)).astype(BF16)
                laters[i][h] = _dot(tri2, jnp.concatenate([lk_hi, lk_lo], axis=0))
        ws = [[None] * n_heads for _ in range(n)]
        for h in range(n_heads):
            r = r_scr[h]
            for i in range(n):
                w = jnp.exp(lks[i][h] + zs[i][h] + (laters[i][h] + r))
                ws[i][h] = (w if valids[i] is None else jnp.where(valids[i], w, 0.0)).astype(BF16)
                r = r + jnp.sum(lks[i][h], axis=0, keepdims=True)
            r_scr[h] = r
        for g in range(group):
            rows = slice(g * LANES, (g + 1) * LANES)
            lhs = jnp.concatenate([_pair_values_t(vts[i][rows]) for i in range(n)], axis=1)
            rhs = jnp.concatenate([ws[i][h] for i in range(n) for h in (2 * g, 2 * g + 1)], axis=0)
            acc_scr[rows, :] += _dot(lhs, rhs)

    def live():
        return (jnp.max(r_scr[...]) > DEAD_LOG_WEIGHT).astype(jnp.int32)

    lead = qi - (SB_LEAD - 1) > first

    @pl.when(lead)
    def _():
        walk(tuple((qi - i, i == 0) for i in range(SB_LEAD)))

    @pl.when(jnp.logical_not(lead))
    def _():
        walk(((qi, True),))

    def body(c):
        walk(((c[0], False),))
        return c[0] - 1, live()

    t0 = jnp.where(lead, qi - SB_LEAD, qi - 1)
    _, alive = lax.while_loop(lambda c: (c[0] > first) & (c[1] > 0), body, (t0, live()))

    @pl.when((qi > first) & (alive > 0))
    def _():
        walk(((first, True),))

    for g in range(group):
        rows = slice(g * LANES, (g + 1) * LANES)
        o_ref[0, :, rows] = acc_scr[rows, :].T.astype(o_ref.dtype)


def _sb_attention(qk, vt, *, n_pairs):
    B, Lp, _ = qk.shape
    tile, group = SB_TILE, SB_GROUP
    assert n_pairs % group == 0
    n_groups = n_pairs // group
    width = group * LANES
    r = lax.broadcasted_iota(jnp.int32, (tile, tile), 0)
    c = lax.broadcasted_iota(jnp.int32, (tile, tile), 1)
    tri = (c > r).astype(BF16)
    tri2 = jnp.concatenate([tri, tri], axis=1)
    kern = functools.partial(_sb_attn_kernel, tile=tile, group=group)
    return pl.pallas_call(
        kern,
        grid=(B, n_groups, Lp // tile),
        in_specs=[
            pl.BlockSpec((1, tile, width), lambda b, h, i: (b, i, h)),
            pl.BlockSpec((1, Lp, width), lambda b, h, i: (b, 0, n_groups + h), pipeline_mode=pl.Buffered(1)),
            pl.BlockSpec((width, Lp), lambda b, h, i: (h, b), pipeline_mode=pl.Buffered(1)),
            pl.BlockSpec((tile, 2 * tile), lambda b, h, i: (0, 0)),
        ],
        out_specs=pl.BlockSpec((1, tile, width), lambda b, h, i: (b, i, h)),
        out_shape=jax.ShapeDtypeStruct((B, Lp, n_pairs * LANES), BF16),
        scratch_shapes=[
            pltpu.VMEM((2 * group, 1, tile), F32),
            pltpu.VMEM((width, tile), F32),
        ],
        compiler_params=_params("arbitrary", "arbitrary", "arbitrary"),
        name="sb_attention",
    )(qk, qk, vt, tri2)


def _rope_table_kernel(inv_ref, o_ref, *, tile):
    row = pl.program_id(0) * tile + lax.broadcasted_iota(jnp.int32, (tile, 1), 0)
    ang = (row - ROW0).astype(F32) * inv_ref[...]
    lane = lax.broadcasted_iota(jnp.int32, (1, LANES), 1)
    half = MLA_ROPE // 2
    in_lo = (lane >= MLA_NOPE) & (lane < MLA_NOPE + half)
    in_hi = (lane >= MLA_NOPE + half) & (lane < MLA_NOPE + MLA_ROPE)
    cos = jnp.cos(ang)
    sin = jnp.sin(ang)
    o_ref[0] = jnp.where(lane < MLA_NOPE, 1.0, jnp.where(in_lo | in_hi, cos, 0.0))
    o_ref[1] = jnp.where(in_lo, -sin, 0.0)
    o_ref[2] = jnp.where(in_hi, sin, 0.0)


def _rope_tables(rows, inv_lane):
    tile = _divisor_tile(rows, 512)
    return pl.pallas_call(
        functools.partial(_rope_table_kernel, tile=tile),
        grid=(rows // tile,),
        in_specs=[pl.BlockSpec((1, LANES), lambda i: (0, 0))],
        out_specs=pl.BlockSpec((3, tile, LANES), lambda i: (0, i, 0)),
        out_shape=jax.ShapeDtypeStruct((3, rows, LANES), F32),
        compiler_params=_params("arbitrary"),
        name="rope_tables",
    )(inv_lane)


def _mla_proj_kernel(h_ref, g_ref, wd_ref, qn_ref, kvn_ref, wq_ref, wk_ref, wvt_ref, tab_ref,
                     q_ref, k_ref, vt_ref, *, q_scale):
    a = _rmsnorm(h_ref[...], g_ref[...]).astype(BF16)
    down = _dot(a, wd_ref[...])
    cq = _rmsnorm(down[:, :MLA_Q_RANK], qn_ref[...]).astype(BF16)
    ckv = _rmsnorm(down[:, MLA_Q_RANK:MLA_Q_RANK + MLA_KV_RANK], kvn_ref[...]).astype(BF16)
    kr = down[:, MLA_Q_RANK + MLA_KV_RANK:]
    c_tab, s_lo, s_hi = tab_ref[0], tab_ref[1], tab_ref[2]
    half = MLA_ROPE // 2

    def rope(x):
        return (x * c_tab + pltpu.roll(x, LANES - half, 1) * s_lo + pltpu.roll(x, half, 1) * s_hi)

    q = _dot(cq, wq_ref[...])
    kn = _dot(ckv, wk_ref[...])
    vt_ref[...] = _dot_nt(wvt_ref[...], ckv).astype(BF16)
    kr_rot = rope(kr)
    for hd in range(N_HEADS):
        cols = slice(hd * LANES, (hd + 1) * LANES)
        q_ref[:, cols] = (rope(q[:, cols]) * q_scale).astype(BF16)
        k_ref[:, cols] = (kn[:, cols] + kr_rot).astype(BF16)


def _mla_proj(h2, g, wd, qn, kvn, wq, wk, wvt, inv_lane, rows_per_batch):
    M, D = h2.shape
    tm = _divisor_tile(rows_per_batch, 512)
    tiles_per_batch = rows_per_batch // tm
    tables = _rope_tables(rows_per_batch, inv_lane)
    kern = functools.partial(_mla_proj_kernel, q_scale=(MLA_NOPE + MLA_ROPE) ** -0.5 * LOG2E)
    full = lambda a: pl.BlockSpec(a.shape, lambda i: (0,) * a.ndim)
    g2, qn2, kvn2 = g.reshape(1, D), qn.reshape(1, -1), kvn.reshape(1, -1)
    return pl.pallas_call(
        kern,
        grid=(M // tm,),
        in_specs=[pl.BlockSpec((tm, D), lambda i: (i, 0)), full(g2), full(wd), full(qn2), full(kvn2),
                  full(wq), full(wk), full(wvt),
                  pl.BlockSpec((3, tm, LANES), lambda i: (0, i % tiles_per_batch, 0))],
        out_specs=[
            pl.BlockSpec((tm, N_HEADS * LANES), lambda i: (i, 0)),
            pl.BlockSpec((tm, N_HEADS * LANES), lambda i: (i, 0)),
            pl.BlockSpec((N_HEADS * MLA_V, tm), lambda i: (0, i)),
        ],
        out_shape=[
            jax.ShapeDtypeStruct((M, N_HEADS * LANES), BF16),
            jax.ShapeDtypeStruct((M, N_HEADS * LANES), BF16),
            jax.ShapeDtypeStruct((N_HEADS * MLA_V, M), BF16),
        ],
        compiler_params=_params("arbitrary"),
        name="mla_proj",
    )(h2, g2, wd, qn2, kvn2, wq, wk, wvt, tables)


def _forget_cumsum_kernel(f_ref, b_ref, sel_ref, kf_ref, frow_ref, carry, *, tile):
    t = pl.program_id(1)

    @pl.when(t == 0)
    def _():
        carry[...] = jnp.zeros_like(carry)

    row_in = lax.broadcasted_iota(jnp.int32, (tile, 1), 0)
    row = t * tile + row_in
    x = jnp.where(row >= ROW0, _log_sigmoid_neg(-(f_ref[0] + b_ref[...])), 0.0)
    shift = 1
    while shift < tile:
        x = x + jnp.where(row_in >= shift, pltpu.roll(x, shift, 0), 0.0)
        shift *= 2
    x = x + carry[...]
    carry[...] = x[tile - 1:tile, :]
    frow_ref[0] = x.T[:N_HEADS, :]
    rest = -x * LOG2E
    kf = jnp.zeros(kf_ref.shape[1:], F32)
    for piece in range(DECAY_PIECES):
        part = rest.astype(BF16)
        rest = rest - part.astype(F32)
        kf = kf + _dot(part, sel_ref[piece])
    kf_ref[0] = kf.astype(BF16)


def _forget_cumsum(flog, b_lane):
    B, Lp, _ = flog.shape
    tile = ATT_TILE
    n_pairs = N_HEADS // 2
    head = lax.broadcasted_iota(jnp.int32, (DECAY_PIECES, LANES, n_pairs * LANES), 1)
    col = lax.broadcasted_iota(jnp.int32, (DECAY_PIECES, LANES, n_pairs * LANES), 2)
    piece = lax.broadcasted_iota(jnp.int32, (DECAY_PIECES, LANES, n_pairs * LANES), 0)
    sel = ((head < N_HEADS) & (col == (head // 2) * LANES + (head % 2) * DECAY_PIECES + piece)).astype(BF16)
    kern = functools.partial(_forget_cumsum_kernel, tile=tile)
    return pl.pallas_call(
        kern,
        grid=(B, Lp // tile),
        in_specs=[
            pl.BlockSpec((1, tile, LANES), lambda b, t: (b, t, 0)),
            pl.BlockSpec((1, LANES), lambda b, t: (0, 0)),
            pl.BlockSpec(sel.shape, lambda b, t: (0, 0, 0)),
        ],
        out_specs=[
            pl.BlockSpec((1, tile, n_pairs * LANES), lambda b, t: (b, t, 0)),
            pl.BlockSpec((1, N_HEADS, tile), lambda b, t: (b, 0, t)),
        ],
        out_shape=[
            jax.ShapeDtypeStruct((B, Lp, n_pairs * LANES), BF16),
            jax.ShapeDtypeStruct((B, N_HEADS, Lp), F32),
        ],
        scratch_shapes=[pltpu.VMEM((1, LANES), F32)],
        compiler_params=_params("arbitrary", "arbitrary"),
        name="forget_cumsum",
    )(flog, b_lane, sel)


def _final_norm_kernel(h_ref, g_ref, o_ref):
    o_ref[0] = _rmsnorm(h_ref[0], g_ref[...])


def _final_norm(h, g, seq):
    B, Lp, D = h.shape
    tile = PAD_FRONT
    skip = PAD_FRONT // tile
    return pl.pallas_call(
        _final_norm_kernel,
        grid=(B, seq // tile),
        in_specs=[
            pl.BlockSpec((1, tile, D), lambda b, t: (b, t + skip, 0)),
            pl.BlockSpec((1, D), lambda b, t: (0, 0)),
        ],
        out_specs=pl.BlockSpec((1, tile, D), lambda b, t: (b, t, 0)),
        out_shape=jax.ShapeDtypeStruct((B, seq, D), F32),
        compiler_params=_params("arbitrary", "arbitrary"),
        name="final_norm",
    )(h, g.reshape(1, D))


def _pad_heads(w, per_head):
    K = w.shape[0]
    w3 = w.reshape(K, N_HEADS, per_head)
    w3 = jnp.pad(w3, ((0, 0), (0, 0), (0, LANES - per_head)))
    return w3.reshape(K, N_HEADS * LANES)


def _mla_layer(h, g, w_down, q_norm, kv_norm, w_uq, w_ukv, w_o):
    B, Lp, D = h.shape
    M = B * Lp
    n_lat = MLA_Q_RANK + MLA_KV_RANK
    wd_rope = jnp.pad(w_down[:, n_lat:], ((0, 0), (MLA_NOPE, LANES - MLA_NOPE - MLA_ROPE)))
    wd_p = jnp.concatenate([w_down[:, :n_lat], wd_rope], axis=1).astype(BF16)
    wq_p = _pad_heads(w_uq, MLA_NOPE + MLA_ROPE).astype(BF16)
    wkv3 = w_ukv.reshape(MLA_KV_RANK, N_HEADS, MLA_NOPE + MLA_V)
    wk_p = _pad_heads(wkv3[:, :, :MLA_NOPE].reshape(MLA_KV_RANK, -1), MLA_NOPE).astype(BF16)
    wvt_p = wkv3[:, :, MLA_NOPE:].reshape(MLA_KV_RANK, N_HEADS * MLA_V).T.astype(BF16)
    inv = ROPE_THETA ** (-jnp.arange(0, MLA_ROPE, 2, dtype=F32) / MLA_ROPE)
    inv_lane = jnp.concatenate([jnp.zeros((MLA_NOPE,), F32), inv, inv,
                                jnp.zeros((LANES - MLA_NOPE - MLA_ROPE,), F32)]).reshape(1, LANES)
    q, k, vt = _mla_proj(h.reshape(M, D), g, wd_p, q_norm, kv_norm, wq_p, wk_p, wvt_p, inv_lane, Lp)
    o = _softmax_attention(q.reshape(B, Lp, -1), k.reshape(B, Lp, -1), vt,
                           width=2 * LANES, n_pairs=N_HEADS // 2)
    return o.reshape(M, -1), w_o.astype(BF16)


def kernel(x, meta, norm_mix, norm_ffn, pool_w, pool_scale, sb_w_qkv, sb_w_o, mla_w_down, mla_q_norm,
           mla_kv_norm, mla_w_uq, mla_w_ukv, mla_w_o, fox_w_qkvf, fox_b_f, fox_w_o, ffn_w_gate, ffn_w_up,
           ffn_w_down, final_norm):
    B, S, D = x.shape
    assert D == D_MODEL and S % ATT_TILE == 0 and PAD_FRONT % ATT_TILE == 0 and PAD_FRONT % SB_TILE == 0
    Lp = S + PAD_FRONT
    M = B * Lp
    HD = N_HEADS * HEAD_DIM
    n_pairs = N_HEADS // 2

    def sb_mixer(h, i, j):
        qk, vt = _qk_vt_proj(h.reshape(M, D), norm_mix[i], sb_w_qkv[j][:, :2 * HD].astype(BF16),
                             sb_w_qkv[j][:, 2 * HD:].T.astype(BF16), q_scale=HEAD_DIM ** -0.5, q_cols=HD)
        o = _sb_attention(qk.reshape(B, Lp, 2 * HD), vt, n_pairs=n_pairs)
        return o.reshape(M, HD), sb_w_o[j].astype(BF16)

    def mla_mixer(h, i, j):
        return _mla_layer(h, norm_mix[i], mla_w_down[j], mla_q_norm[j], mla_kv_norm[j], mla_w_uq[j],
                          mla_w_ukv[j], mla_w_o[j])

    def fox_mixer(h, i, j):
        wf = jnp.pad(fox_w_qkvf[j][:, 3 * HD:], ((0, 0), (0, LANES - N_HEADS))).astype(BF16)
        qk, vt, flog = _qk_vt_proj(h.reshape(M, D), norm_mix[i], fox_w_qkvf[j][:, :2 * HD].astype(BF16),
                                   fox_w_qkvf[j][:, 2 * HD:3 * HD].T.astype(BF16), wf,
                                   q_scale=HEAD_DIM ** -0.5 * LOG2E, q_cols=HD)
        b_lane = jnp.pad(fox_b_f[j].astype(F32), (0, LANES - N_HEADS)).reshape(1, LANES)
        kf, frow = _forget_cumsum(flog.reshape(B, Lp, LANES), b_lane)
        qk = qk.reshape(B, Lp, 2 * HD)
        o = _softmax_attention(qk, qk, vt, width=LANES, n_pairs=n_pairs, offs=(0, n_pairs), kf=kf, frow=frow)
        return o.reshape(M, HD), fox_w_o[j].astype(BF16)

    mixers = (None, sb_mixer, mla_mixer, fox_mixer)
    h = None
    for i in range(norm_mix.shape[0]):
        m, j = i % len(mixers), i // len(mixers)
        ffn_w = (norm_ffn[i], ffn_w_gate[i].astype(BF16), ffn_w_up[i].astype(BF16), ffn_w_down[i].astype(BF16))
        if m == 0:
            assert i == 0, "the pooling mixer doubles as the layout stage and must come first"
            h = _pool_layer(x, meta, norm_mix[i], pool_w[j], pool_scale[j])
            h = _ffn(h.reshape(M, D), *ffn_w).reshape(B, Lp, D)
        else:
            o2, wo = mixers[m](h, i, j)
            h = _ffn(h.reshape(M, D), *ffn_w, o2=o2, wo=wo).reshape(B, Lp, D)
    return _final_norm(h, final_norm, S)
```

```python
import functools

import jax
import jax.numpy as jnp
from jax import lax
from jax.experimental import pallas as pl
from jax.experimental.pallas import tpu as pltpu

F32 = jnp.float32
BF16 = jnp.bfloat16

D_MODEL = 1024
N_META = 16
EPS = 1e-6
POOL_WINDOWS = (2, 4, 8, 16)
POOL_GROUP = D_MODEL // len(POOL_WINDOWS)
N_HEADS = 16
HEAD_DIM = 64
MLA_Q_RANK = 384
MLA_KV_RANK = 256
MLA_NOPE = 64
MLA_ROPE = 32
MLA_V = 64
ROPE_THETA = 10000.0

LANES = 128
PAD_FRONT = 256
ROW0 = PAD_FRONT - N_META
HALO = max(POOL_WINDOWS)
MASK_VALUE = -1e30
LOG2E = 1.4426950408889634
DECAY_PIECES = 3
ACC_ROWS = LANES + 16
DEAD_LOG_WEIGHT = -104.0
VMEM_LIMIT = 56 * 1024 * 1024

ROW_TILE = 1024
ATT_TILE = 256
ATT_GROUP = 4
SB_TILE = 128
SB_LEAD = 3
SB_GROUP = 8
PROJ_TILE = 512
FF_TILE = 256


def _divisor_tile(n, target):
    best = None
    for t in range(8, min(n, target) + 1, 8):
        if n % t == 0:
            best = t
    assert best is not None, (n, target)
    return best


def _params(*sem):
    return pltpu.CompilerParams(dimension_semantics=sem, vmem_limit_bytes=VMEM_LIMIT)


def _rmsnorm(x, g):
    return x * lax.rsqrt(jnp.mean(x * x, axis=-1, keepdims=True) + EPS) * g


def _log_sigmoid_neg(z):
    return -(jnp.maximum(z, 0.0) + jnp.log(1.0 + jnp.exp(-jnp.abs(z))))


def _dot(a, b):
    return jnp.dot(a, b, preferred_element_type=F32)


def _dot_nt(a, b):
    return lax.dot_general(a, b, (((1,), (1,)), ((), ())), preferred_element_type=F32)


def _pool_kernel(x_ref, meta_ref, g_ref, w_ref, sc_ref, o_ref, abuf, *, tile):
    t = pl.program_id(1)
    front = jnp.concatenate([jnp.zeros((ROW0, D_MODEL), F32), meta_ref[...]], axis=0)
    h = jnp.where(t == 0, front, x_ref[0])
    row = t * tile + lax.broadcasted_iota(jnp.int32, (tile, 1), 0)
    a = jnp.where(row >= ROW0, _rmsnorm(h, g_ref[...]), 0.0)

    @pl.when(t == 0)
    def _():
        abuf[0:HALO, :] = jnp.zeros((HALO, D_MODEL), F32)

    @pl.when(t > 0)
    def _():
        abuf[0:HALO, :] = abuf[tile:tile + HALO, :]

    abuf[HALO:HALO + tile, :] = a
    pos1 = row - ROW0 + 1
    outs = []
    for g, win in enumerate(POOL_WINDOWS):
        cols = slice(g * POOL_GROUP, (g + 1) * POOL_GROUP)
        acc = abuf[:, cols]
        shift = 1
        while shift < win:
            acc = acc + pltpu.roll(acc, shift, 0)
            shift *= 2
        xg = a[:, cols]
        cnt = jnp.clip(pos1, 1, win).astype(F32)
        pooled = acc[HALO:] / cnt - xg
        outs.append(_dot(pooled.astype(BF16), w_ref[g]))
    mixed = jnp.concatenate(outs, axis=1) * sc_ref[...]
    o_ref[0] = h + mixed


def _pool_layer(x, meta, g, w, sc):
    B, S, D = x.shape
    tile = PAD_FRONT
    Lp = S + PAD_FRONT
    kern = functools.partial(_pool_kernel, tile=tile)
    return pl.pallas_call(
        kern,
        grid=(B, Lp // tile),
        in_specs=[
            pl.BlockSpec((1, tile, D), lambda b, t: (b, jnp.maximum(t - 1, 0), 0)),
            pl.BlockSpec((N_META, D), lambda b, t: (0, 0)),
            pl.BlockSpec((1, D), lambda b, t: (0, 0)),
            pl.BlockSpec((len(POOL_WINDOWS), POOL_GROUP, POOL_GROUP), lambda b, t: (0, 0, 0)),
            pl.BlockSpec((1, D), lambda b, t: (0, 0)),
        ],
        out_specs=pl.BlockSpec((1, tile, D), lambda b, t: (b, t, 0)),
        out_shape=jax.ShapeDtypeStruct((B, Lp, D), F32),
        scratch_shapes=[pltpu.VMEM((tile + HALO, D), F32)],
        compiler_params=_params("arbitrary", "arbitrary"),
        name="pool_layer",
    )(x, meta.astype(x.dtype), g.reshape(1, D), w.astype(BF16), sc.reshape(1, D))


def _qk_vt_proj_kernel(*refs, q_scale, q_cols, tn, has_gate):
    if has_gate:
        h_ref, g_ref, w_ref, wvt_ref, wf_ref, o_ref, vt_ref, f_ref = refs
    else:
        h_ref, g_ref, w_ref, wvt_ref, o_ref, vt_ref = refs
    a = _rmsnorm(h_ref[...], g_ref[...]).astype(BF16)
    for c in range(w_ref.shape[1] // tn):
        cols = slice(c * tn, (c + 1) * tn)
        part = _dot(a, w_ref[:, cols])
        if (c + 1) * tn <= q_cols:
            part = part * q_scale
        o_ref[:, cols] = part.astype(o_ref.dtype)
    vt_ref[...] = _dot_nt(wvt_ref[...], a).astype(vt_ref.dtype)
    if has_gate:
        f_ref[...] = _dot(a, wf_ref[...])


def _qk_vt_proj(h2, g, w_qk, w_vt, wf=None, *, q_scale, q_cols):
    M, D = h2.shape
    N = w_qk.shape[1]
    Dv = w_vt.shape[0]
    tn = PROJ_TILE
    assert N % tn == 0 and q_cols % tn == 0
    tm = _divisor_tile(M, ROW_TILE)
    has_gate = wf is not None
    resident = lambda shape: pl.BlockSpec(shape, lambda i: (0, 0), pipeline_mode=pl.Buffered(1))
    in_specs = [pl.BlockSpec((tm, D), lambda i: (i, 0)), resident((1, D)), resident((D, N)), resident((Dv, D))]
    out_specs = [pl.BlockSpec((tm, N), lambda i: (i, 0)), pl.BlockSpec((Dv, tm), lambda i: (0, i))]
    out_shape = [jax.ShapeDtypeStruct((M, N), BF16), jax.ShapeDtypeStruct((Dv, M), BF16)]
    args = [h2, g.reshape(1, D), w_qk, w_vt]
    if has_gate:
        in_specs.append(resident((D, LANES)))
        out_specs.append(pl.BlockSpec((tm, LANES), lambda i: (i, 0)))
        out_shape.append(jax.ShapeDtypeStruct((M, LANES), F32))
        args.append(wf)
    return pl.pallas_call(
        functools.partial(_qk_vt_proj_kernel, q_scale=q_scale, q_cols=q_cols, tn=tn, has_gate=has_gate),
        grid=(M // tm,),
        in_specs=in_specs,
        out_specs=out_specs,
        out_shape=out_shape,
        compiler_params=_params("arbitrary"),
        name="qk_vt_proj",
    )(*args)


def _ffn_kernel(*refs, tf, has_mix):
    if has_mix:
        h_ref, o_ref, wo_ref, g_ref, wg_ref, wu_ref, wd_ref, out_ref, acc_scr = refs
        h = h_ref[...] + _dot(o_ref[...], wo_ref[...])
    else:
        h_ref, g_ref, wg_ref, wu_ref, wd_ref, out_ref, acc_scr = refs
        h = h_ref[...]
    a = _rmsnorm(h, g_ref[...]).astype(BF16)
    for c in range(wg_ref.shape[1] // tf):
        cols = slice(c * tf, (c + 1) * tf)
        gate = _dot(a, wg_ref[:, cols])
        up = _dot(a, wu_ref[:, cols])
        act = (gate * jax.nn.sigmoid(gate) * up).astype(BF16)
        part = _dot(act, wd_ref[cols, :])
        if c == 0:
            acc_scr[...] = part
        else:
            acc_scr[...] += part
    out_ref[...] = h + acc_scr[...]


def _ffn(h2, g, wg, wu, wd, o2=None, wo=None):
    M, D = h2.shape
    F = wg.shape[1]
    assert F % FF_TILE == 0
    tm = _divisor_tile(M, ROW_TILE)
    has_mix = o2 is not None
    resident = lambda shape: pl.BlockSpec(shape, lambda i: (0, 0), pipeline_mode=pl.Buffered(1))
    rows = lambda width: pl.BlockSpec((tm, width), lambda i: (i, 0))
    in_specs, args = [rows(D)], [h2]
    if has_mix:
        in_specs += [rows(o2.shape[1]), resident(wo.shape)]
        args += [o2, wo]
    in_specs += [resident((1, D)), resident((D, F)), resident((D, F)), resident((F, D))]
    args += [g.reshape(1, D), wg, wu, wd]
    return pl.pallas_call(
        functools.partial(_ffn_kernel, tf=FF_TILE, has_mix=has_mix),
        grid=(M // tm,),
        in_specs=in_specs,
        out_specs=rows(D),
        out_shape=jax.ShapeDtypeStruct((M, D), F32),
        scratch_shapes=[pltpu.VMEM((tm, D), F32)],
        compiler_params=_params("arbitrary"),
        name="ffn",
    )(*args)


def _split_heads(x, width):
    if width == 2 * LANES:
        return x[:, :LANES], x[:, LANES:]
    lane = lax.broadcasted_iota(jnp.int32, (1, LANES), 1)
    zero = jnp.zeros_like(x)
    return jnp.where(lane < HEAD_DIM, x, zero), jnp.where(lane >= HEAD_DIM, x, zero)


def _split_keys(x, width):
    if width == 2 * LANES:
        return x[:, :LANES], x[:, LANES:]
    return x, x


def _pair_values_t(vt):
    sub = lax.broadcasted_iota(jnp.int32, (LANES, 1), 0)
    zero = jnp.zeros_like(vt)
    return jnp.concatenate(
        [jnp.where(sub < HEAD_DIM, vt, zero), jnp.where(sub >= HEAD_DIM, vt, zero)], axis=1)


def _pair_rows(x0, x1):
    sub = lax.broadcasted_iota(jnp.int32, (ACC_ROWS, 1), 0)
    return jnp.where((sub < HEAD_DIM) | (sub == LANES), x0, x1)


def _sum_rows(tile):
    sub = lax.broadcasted_iota(jnp.int32, (ACC_ROWS - LANES, 2 * tile), 0)
    col = lax.broadcasted_iota(jnp.int32, (ACC_ROWS - LANES, 2 * tile), 1)
    return jnp.where(sub == col // tile, 1.0, 0.0).astype(BF16)


def _softmax_attn_kernel(*refs, tile, width, decay, group):
    if decay:
        q_ref, k_ref, vt_ref, kf_ref, frow_ref, o_ref = refs[:6]
    else:
        q_ref, k_ref, vt_ref, o_ref = refs[:4]
    m_scr, acc_scr, raw0, raw1, p0, p1, al0, al1 = refs[-8:]
    raw_s, p_s, al_s = (raw0, raw1), (p0, p1), (al0, al1)
    hg = pl.program_id(1)
    qi = pl.program_id(2)
    n_heads = 2 * group
    q = q_ref[0]
    qs = []
    for g in range(group):
        qs += list(_split_heads(q[:, g * width:(g + 1) * width], width))
    qcol = qi * tile + lax.broadcasted_iota(jnp.int32, (1, tile), 1)
    if decay:
        lane = lax.broadcasted_iota(jnp.int32, (1, LANES), 1)
        for h in range(n_heads):
            e = h % 2
            ones = jnp.where((lane >= e * DECAY_PIECES) & (lane < (e + 1) * DECAY_PIECES), 1.0, 0.0)
            qs[h] = jnp.concatenate([qs[h], jnp.broadcast_to(ones.astype(BF16), (tile, LANES))], axis=1)
        head0 = n_heads * hg
        fq = tuple(frow_ref[0, pl.ds(head0 + h, 1), pl.ds(pl.multiple_of(qi * tile, tile), tile)] * LOG2E
                   for h in range(n_heads))

    sum_rows = _sum_rows(tile)

    def meta_keys():
        rows = slice(ROW0, PAD_FRONT)
        k = k_ref[0, rows, :]
        key = ROW0 + lax.broadcasted_iota(jnp.int32, (N_META, 1), 0)
        valid = key <= qcol
        ps = []
        for h in range(n_heads):
            g, e = divmod(h, 2)
            kk = _split_keys(k[:, g * width:(g + 1) * width], width)[e]
            if decay:
                kk = jnp.concatenate([kk, kf_ref[0, rows, g * LANES:(g + 1) * LANES]], axis=1)
            raw = jnp.where(valid, _dot_nt(kk, qs[h]), MASK_VALUE)
            m_new = jnp.max(raw, axis=0, keepdims=True)
            if decay:
                m_new = m_new + fq[h]
                ps.append(jnp.exp2(raw - (m_new - fq[h])).astype(BF16))
            else:
                ps.append(jnp.exp2(raw - m_new).astype(BF16))
            m_scr[h] = m_new
        vt = vt_ref[:, rows]
        meta_sums = _sum_rows(N_META)
        for g in range(group):
            lhs = jnp.concatenate([_pair_values_t(vt[g * LANES:(g + 1) * LANES]), meta_sums], axis=0)
            acc_scr[g * ACC_ROWS:(g + 1) * ACC_ROWS, :] = _dot(lhs, jnp.concatenate(ps[2 * g:2 * g + 2], axis=0))

    def scores(t, s):
        start = pl.multiple_of(t * tile, tile)
        k = k_ref[0, pl.ds(start, tile), :]
        if decay:
            kf = kf_ref[0, pl.ds(start, tile), :]
        for g in range(group):
            ks = _split_keys(k[:, g * width:(g + 1) * width], width)
            if decay:
                ks = tuple(jnp.concatenate([kk, kf[:, g * LANES:(g + 1) * LANES]], axis=1) for kk in ks)
            for e in range(2):
                raw_s[s][2 * g + e] = _dot_nt(ks[e], qs[2 * g + e])

    def weights(t, s, masked):
        start = pl.multiple_of(t * tile, tile)
        if masked:
            key = start + lax.broadcasted_iota(jnp.int32, (tile, 1), 0)
            valid = key <= qcol
        for h in range(n_heads):
            raw = raw_s[s][h]
            m_old = m_scr[h]
            if masked:
                raw = jnp.where(valid, raw, MASK_VALUE)
            if decay:
                m_new = jnp.maximum(m_old, jnp.max(raw, axis=0, keepdims=True) + fq[h])
                p = jnp.exp2(raw - (m_new - fq[h]))
            else:
                m_new = jnp.maximum(m_old, jnp.max(raw, axis=0, keepdims=True))
                p = jnp.exp2(raw - m_new)
            m_scr[h] = m_new
            p_s[s][h * tile:(h + 1) * tile, :] = p.astype(BF16)
            al_s[s][h] = jnp.exp2(m_old - m_new)

    def accumulate(t, s):
        vt = vt_ref[:, pl.ds(pl.multiple_of(t * tile, tile), tile)]
        for g in range(group):
            lhs = jnp.concatenate([_pair_values_t(vt[g * LANES:(g + 1) * LANES]), sum_rows], axis=0)
            pv = _dot(lhs, p_s[s][2 * g * tile:(2 * g + 2) * tile, :])
            rows = slice(g * ACC_ROWS, (g + 1) * ACC_ROWS)
            acc_scr[rows, :] = _pair_rows(al_s[s][2 * g], al_s[s][2 * g + 1]) * acc_scr[rows, :] + pv

    meta_keys()

    @pl.when(qi > 0)
    def _():
        p_s[0][...] = jnp.zeros_like(p_s[0])
        al_s[0][...] = jnp.ones_like(al_s[0])
        scores(1, 1)
        n_steady = qi - 1

        def pair(i, c):
            t = 2 * i + 1
            scores(t + 1, 0)
            weights(t, 1, False)
            accumulate(t - 1, 0)
            scores(t + 2, 1)
            weights(t + 1, 0, False)
            accumulate(t, 1)
            return c

        lax.fori_loop(0, n_steady // 2, pair, 0)

        @pl.when(n_steady % 2 == 1)
        def _():
            t = qi - 1
            scores(qi, 0)
            weights(t, 1, False)
            accumulate(t - 1, 0)
            weights(qi, 0, True)
            accumulate(t, 1)
            accumulate(qi, 0)

        @pl.when(n_steady % 2 == 0)
        def _():
            weights(qi, 1, True)
            accumulate(qi - 1, 0)
            accumulate(qi, 1)

    sub = lax.broadcasted_iota(jnp.int32, (LANES, 1), 0)
    for g in range(group):
        acc = acc_scr[g * ACC_ROWS:(g + 1) * ACC_ROWS, :]
        total = jnp.where(sub < HEAD_DIM, acc[LANES:LANES + 1], acc[LANES + 1:LANES + 2])
        o_ref[0, :, g * LANES:(g + 1) * LANES] = (acc[:LANES] / total).T.astype(o_ref.dtype)


def _softmax_attention(q, k, vt, *, width, n_pairs, offs=(0, 0), kf=None, frow=None):
    B, Lp, _ = q.shape
    qo, ko = offs
    tile = ATT_TILE
    group = ATT_GROUP
    assert n_pairs % group == 0 and qo % group == 0 and ko % group == 0
    n_heads = 2 * group
    decay = kf is not None
    kern = functools.partial(_softmax_attn_kernel, tile=tile, width=width, decay=decay, group=group)
    in_specs = [
        pl.BlockSpec((1, tile, group * width), lambda b, h, i: (b, i, qo // group + h)),
        pl.BlockSpec((1, Lp, group * width), lambda b, h, i: (b, 0, ko // group + h), pipeline_mode=pl.Buffered(1)),
        pl.BlockSpec((group * LANES, Lp), lambda b, h, i: (h, b)),
    ]
    args = [q, k, vt]
    if decay:
        in_specs += [
            pl.BlockSpec((1, Lp, group * LANES), lambda b, h, i: (b, 0, h), pipeline_mode=pl.Buffered(1)),
            pl.BlockSpec((1, N_HEADS, Lp), lambda b, h, i: (b, 0, 0)),
        ]
        args += [kf, frow]
    return pl.pallas_call(
        kern,
        grid=(B, n_pairs // group, Lp // tile),
        in_specs=in_specs,
        out_specs=pl.BlockSpec((1, tile, group * LANES), lambda b, h, i: (b, i, h)),
        out_shape=jax.ShapeDtypeStruct((B, Lp, n_pairs * LANES), BF16),
        scratch_shapes=[
            pltpu.VMEM((n_heads, 1, tile), F32),
            pltpu.VMEM((group * ACC_ROWS, tile), F32),
            pltpu.VMEM((n_heads, tile, tile), F32),
            pltpu.VMEM((n_heads, tile, tile), F32),
            pltpu.VMEM((n_heads * tile, tile), BF16),
            pltpu.VMEM((n_heads * tile, tile), BF16),
            pltpu.VMEM((n_heads, 1, tile), F32),
            pltpu.VMEM((n_heads, 1, tile), F32),
        ],
        compiler_params=_params("arbitrary", "arbitrary", "arbitrary"),
        name="fox_attention" if decay else "mla_attention",
    )(*args)


def _sb_attn_kernel(q_ref, k_ref, vt_ref, tri_ref, o_ref, r_scr, acc_scr, *, tile, group):
    qi = pl.program_id(2)
    n_heads = 2 * group
    first = ROW0 // tile
    q = q_ref[0]
    qs = []
    for g in range(group):
        qs += list(_split_heads(q[:, g * LANES:(g + 1) * LANES], LANES))
    qcol = qi * tile + lax.broadcasted_iota(jnp.int32, (1, tile), 1)
    r_scr[...] = jnp.zeros_like(r_scr)
    acc_scr[...] = jnp.zeros_like(acc_scr)
    tri2 = tri_ref[...]

    def walk(tiles):
        ks, vts, valids = [], [], []
        for t, masked in tiles:
            start = pl.multiple_of(t * tile, tile)
            ks.append(k_ref[0, pl.ds(start, tile), :])
            vts.append(vt_ref[:, pl.ds(start, tile)])
            key = start + lax.broadcasted_iota(jnp.int32, (tile, 1), 0)
            valids.append((key >= ROW0) & (key < qcol) if masked else None)
        n = len(tiles)
        zs = [[_dot_nt(ks[i][:, (h // 2) * LANES:(h // 2 + 1) * LANES], qs[h]) for h in range(n_heads)]
              for i in range(n)]
        lks = [[None] * n_heads for _ in range(n)]
        for i in range(n):
            for h in range(n_heads):
                lk = _log_sigmoid_neg(zs[i][h])
                lks[i][h] = lk if valids[i] is None else jnp.where(valids[i], lk, 0.0)
        laters = [[None] * n_heads for _ in range(n)]
        for i in range(n):
            for h in range(n_heads):
                lk_hi = lks[i][h].astype(BF16)
                lk_lo = (lks[i][h] - lk_hi.astype(F32)).astype(BF16)
                laters[i][h] = _dot(tri2, jnp.concatenate([lk_hi, lk_lo], axis=0))
        ws = [[None] * n_heads for _ in range(n)]
        for h in range(n_heads):
            r = r_scr[h]
            for i in range(n):
                w = jnp.exp(lks[i][h] + zs[i][h] + (laters[i][h] + r))
                ws[i][h] = (w if valids[i] is None else jnp.where(valids[i], w, 0.0)).astype(BF16)
                r = r + jnp.sum(lks[i][h], axis=0, keepdims=True)
            r_scr[h] = r
        for g in range(group):
            rows = slice(g * LANES, (g + 1) * LANES)
            lhs = jnp.concatenate([_pair_values_t(vts[i][rows]) for i in range(n)], axis=1)
            rhs = jnp.concatenate([ws[i][h] for i in range(n) for h in (2 * g, 2 * g + 1)], axis=0)
            acc_scr[rows, :] += _dot(lhs, rhs)

    def live():
        return (jnp.max(r_scr[...]) > DEAD_LOG_WEIGHT).astype(jnp.int32)

    lead = qi - (SB_LEAD - 1) > first

    @pl.when(lead)
    def _():
        walk(tuple((qi - i, i == 0) for i in range(SB_LEAD)))

    @pl.when(jnp.logical_not(lead))
    def _():
        walk(((qi, True),))

    def body(c):
        walk(((c[0], False),))
        return c[0] - 1, live()

    t0 = jnp.where(lead, qi - SB_LEAD, qi - 1)
    _, alive = lax.while_loop(lambda c: (c[0] > first) & (c[1] > 0), body, (t0, live()))

    @pl.when((qi > first) & (alive > 0))
    def _():
        walk(((first, True),))

    for g in range(group):
        rows = slice(g * LANES, (g + 1) * LANES)
        o_ref[0, :, rows] = acc_scr[rows, :].T.astype(o_ref.dtype)


def _sb_attention(qk, vt, *, n_pairs):
    B, Lp, _ = qk.shape
    tile, group = SB_TILE, SB_GROUP
    assert n_pairs % group == 0
    n_groups = n_pairs // group
    width = group * LANES
    r = lax.broadcasted_iota(jnp.int32, (tile, tile), 0)
    c = lax.broadcasted_iota(jnp.int32, (tile, tile), 1)
    tri = (c > r).astype(BF16)
    tri2 = jnp.concatenate([tri, tri], axis=1)
    kern = functools.partial(_sb_attn_kernel, tile=tile, group=group)
    return pl.pallas_call(
        kern,
        grid=(B, n_groups, Lp // tile),
        in_specs=[
            pl.BlockSpec((1, tile, width), lambda b, h, i: (b, i, h)),
            pl.BlockSpec((1, Lp, width), lambda b, h, i: (b, 0, n_groups + h), pipeline_mode=pl.Buffered(1)),
            pl.BlockSpec((width, Lp), lambda b, h, i: (h, b), pipeline_mode=pl.Buffered(1)),
            pl.BlockSpec((tile, 2 * tile), lambda b, h, i: (0, 0)),
        ],
        out_specs=pl.BlockSpec((1, tile, width), lambda b, h, i: (b, i, h)),
        out_shape=jax.ShapeDtypeStruct((B, Lp, n_pairs * LANES), BF16),
        scratch_shapes=[
            pltpu.VMEM((2 * group, 1, tile), F32),
            pltpu.VMEM((width, tile), F32),
        ],
        compiler_params=_params("arbitrary", "arbitrary", "arbitrary"),
        name="sb_attention",
    )(qk, qk, vt, tri2)


def _rope_table_kernel(inv_ref, o_ref, *, tile):
    row = pl.program_id(0) * tile + lax.broadcasted_iota(jnp.int32, (tile, 1), 0)
    ang = (row - ROW0).astype(F32) * inv_ref[...]
    lane = lax.broadcasted_iota(jnp.int32, (1, LANES), 1)
    half = MLA_ROPE // 2
    in_lo = (lane >= MLA_NOPE) & (lane < MLA_NOPE + half)
    in_hi = (lane >= MLA_NOPE + half) & (lane < MLA_NOPE + MLA_ROPE)
    cos = jnp.cos(ang)
    sin = jnp.sin(ang)
    o_ref[0] = jnp.where(lane < MLA_NOPE, 1.0, jnp.where(in_lo | in_hi, cos, 0.0))
    o_ref[1] = jnp.where(in_lo, -sin, 0.0)
    o_ref[2] = jnp.where(in_hi, sin, 0.0)


def _rope_tables(rows, inv_lane):
    tile = _divisor_tile(rows, 512)
    return pl.pallas_call(
        functools.partial(_rope_table_kernel, tile=tile),
        grid=(rows // tile,),
        in_specs=[pl.BlockSpec((1, LANES), lambda i: (0, 0))],
        out_specs=pl.BlockSpec((3, tile, LANES), lambda i: (0, i, 0)),
        out_shape=jax.ShapeDtypeStruct((3, rows, LANES), F32),
        compiler_params=_params("arbitrary"),
        name="rope_tables",
    )(inv_lane)


def _mla_proj_kernel(h_ref, g_ref, wd_ref, qn_ref, kvn_ref, wq_ref, wk_ref, wvt_ref, tab_ref,
                     q_ref, k_ref, vt_ref, *, q_scale):
    a = _rmsnorm(h_ref[...], g_ref[...]).astype(BF16)
    down = _dot(a, wd_ref[...])
    cq = _rmsnorm(down[:, :MLA_Q_RANK], qn_ref[...]).astype(BF16)
    ckv = _rmsnorm(down[:, MLA_Q_RANK:MLA_Q_RANK + MLA_KV_RANK], kvn_ref[...]).astype(BF16)
    kr = down[:, MLA_Q_RANK + MLA_KV_RANK:]
    c_tab, s_lo, s_hi = tab_ref[0], tab_ref[1], tab_ref[2]
    half = MLA_ROPE // 2

    def rope(x):
        return (x * c_tab + pltpu.roll(x, LANES - half, 1) * s_lo + pltpu.roll(x, half, 1) * s_hi)

    q = _dot(cq, wq_ref[...])
    kn = _dot(ckv, wk_ref[...])
    vt_ref[...] = _dot_nt(wvt_ref[...], ckv).astype(BF16)
    kr_rot = rope(kr)
    for hd in range(N_HEADS):
        cols = slice(hd * LANES, (hd + 1) * LANES)
        q_ref[:, cols] = (rope(q[:, cols]) * q_scale).astype(BF16)
        k_ref[:, cols] = (kn[:, cols] + kr_rot).astype(BF16)


def _mla_proj(h2, g, wd, qn, kvn, wq, wk, wvt, inv_lane, rows_per_batch):
    M, D = h2.shape
    tm = _divisor_tile(rows_per_batch, 512)
    tiles_per_batch = rows_per_batch // tm
    tables = _rope_tables(rows_per_batch, inv_lane)
    kern = functools.partial(_mla_proj_kernel, q_scale=(MLA_NOPE + MLA_ROPE) ** -0.5 * LOG2E)
    full = lambda a: pl.BlockSpec(a.shape, lambda i: (0,) * a.ndim)
    g2, qn2, kvn2 = g.reshape(1, D), qn.reshape(1, -1), kvn.reshape(1, -1)
    return pl.pallas_call(
        kern,
        grid=(M // tm,),
        in_specs=[pl.BlockSpec((tm, D), lambda i: (i, 0)), full(g2), full(wd), full(qn2), full(kvn2),
                  full(wq), full(wk), full(wvt),
                  pl.BlockSpec((3, tm, LANES), lambda i: (0, i % tiles_per_batch, 0))],
        out_specs=[
            pl.BlockSpec((tm, N_HEADS * LANES), lambda i: (i, 0)),
            pl.BlockSpec((tm, N_HEADS * LANES), lambda i: (i, 0)),
            pl.BlockSpec((N_HEADS * MLA_V, tm), lambda i: (0, i)),
        ],
        out_shape=[
            jax.ShapeDtypeStruct((M, N_HEADS * LANES), BF16),
            jax.ShapeDtypeStruct((M, N_HEADS * LANES), BF16),
            jax.ShapeDtypeStruct((N_HEADS * MLA_V, M), BF16),
        ],
        compiler_params=_params("arbitrary"),
        name="mla_proj",
    )(h2, g2, wd, qn2, kvn2, wq, wk, wvt, tables)


def _forget_cumsum_kernel(f_ref, b_ref, sel_ref, kf_ref, frow_ref, carry, *, tile):
    t = pl.program_id(1)

    @pl.when(t == 0)
    def _():
        carry[...] = jnp.zeros_like(carry)

    row_in = lax.broadcasted_iota(jnp.int32, (tile, 1), 0)
    row = t * tile + row_in
    x = jnp.where(row >= ROW0, _log_sigmoid_neg(-(f_ref[0] + b_ref[...])), 0.0)
    shift = 1
    while shift < tile:
        x = x + jnp.where(row_in >= shift, pltpu.roll(x, shift, 0), 0.0)
        shift *= 2
    x = x + carry[...]
    carry[...] = x[tile - 1:tile, :]
    frow_ref[0] = x.T[:N_HEADS, :]
    rest = -x * LOG2E
    kf = jnp.zeros(kf_ref.shape[1:], F32)
    for piece in range(DECAY_PIECES):
        part = rest.astype(BF16)
        rest = rest - part.astype(F32)
        kf = kf + _dot(part, sel_ref[piece])
    kf_ref[0] = kf.astype(BF16)


def _forget_cumsum(flog, b_lane):
    B, Lp, _ = flog.shape
    tile = _divisor_tile(Lp, ROW_TILE)
    n_pairs = N_HEADS // 2
    head = lax.broadcasted_iota(jnp.int32, (DECAY_PIECES, LANES, n_pairs * LANES), 1)
    col = lax.broadcasted_iota(jnp.int32, (DECAY_PIECES, LANES, n_pairs * LANES), 2)
    piece = lax.broadcasted_iota(jnp.int32, (DECAY_PIECES, LANES, n_pairs * LANES), 0)
    sel = ((head < N_HEADS) & (col == (head // 2) * LANES + (head % 2) * DECAY_PIECES + piece)).astype(BF16)
    kern = functools.partial(_forget_cumsum_kernel, tile=tile)
    return pl.pallas_call(
        kern,
        grid=(B, Lp // tile),
        in_specs=[
            pl.BlockSpec((1, tile, LANES), lambda b, t: (b, t, 0)),
            pl.BlockSpec((1, LANES), lambda b, t: (0, 0)),
            pl.BlockSpec(sel.shape, lambda b, t: (0, 0, 0)),
        ],
        out_specs=[
            pl.BlockSpec((1, tile, n_pairs * LANES), lambda b, t: (b, t, 0)),
            pl.BlockSpec((1, N_HEADS, tile), lambda b, t: (b, 0, t)),
        ],
        out_shape=[
            jax.ShapeDtypeStruct((B, Lp, n_pairs * LANES), BF16),
            jax.ShapeDtypeStruct((B, N_HEADS, Lp), F32),
        ],
        scratch_shapes=[pltpu.VMEM((1, LANES), F32)],
        compiler_params=_params("arbitrary", "arbitrary"),
        name="forget_cumsum",
    )(flog, b_lane, sel)


def _final_norm_kernel(h_ref, g_ref, o_ref):
    o_ref[0] = _rmsnorm(h_ref[0], g_ref[...])


def _final_norm(h, g, seq):
    B, Lp, D = h.shape
    tile = PAD_FRONT
    skip = PAD_FRONT // tile
    return pl.pallas_call(
        _final_norm_kernel,
        grid=(B, seq // tile),
        in_specs=[
            pl.BlockSpec((1, tile, D), lambda b, t: (b, t + skip, 0)),
            pl.BlockSpec((1, D), lambda b, t: (0, 0)),
        ],
        out_specs=pl.BlockSpec((1, tile, D), lambda b, t: (b, t, 0)),
        out_shape=jax.ShapeDtypeStruct((B, seq, D), F32),
        compiler_params=_params("arbitrary", "arbitrary"),
        name="final_norm",
    )(h, g.reshape(1, D))


def _pad_heads(w, per_head):
    K = w.shape[0]
    w3 = w.reshape(K, N_HEADS, per_head)
    w3 = jnp.pad(w3, ((0, 0), (0, 0), (0, LANES - per_head)))
    return w3.reshape(K, N_HEADS * LANES)


def _mla_layer(h, g, w_down, q_norm, kv_norm, w_uq, w_ukv, w_o):
    B, Lp, D = h.shape
    M = B * Lp
    n_lat = MLA_Q_RANK + MLA_KV_RANK
    wd_rope = jnp.pad(w_down[:, n_lat:], ((0, 0), (MLA_NOPE, LANES - MLA_NOPE - MLA_ROPE)))
    wd_p = jnp.concatenate([w_down[:, :n_lat], wd_rope], axis=1).astype(BF16)
    wq_p = _pad_heads(w_uq, MLA_NOPE + MLA_ROPE).astype(BF16)
    wkv3 = w_ukv.reshape(MLA_KV_RANK, N_HEADS, MLA_NOPE + MLA_V)
    wk_p = _pad_heads(wkv3[:, :, :MLA_NOPE].reshape(MLA_KV_RANK, -1), MLA_NOPE).astype(BF16)
    wvt_p = wkv3[:, :, MLA_NOPE:].reshape(MLA_KV_RANK, N_HEADS * MLA_V).T.astype(BF16)
    inv = ROPE_THETA ** (-jnp.arange(0, MLA_ROPE, 2, dtype=F32) / MLA_ROPE)
    inv_lane = jnp.concatenate([jnp.zeros((MLA_NOPE,), F32), inv, inv,
                                jnp.zeros((LANES - MLA_NOPE - MLA_ROPE,), F32)]).reshape(1, LANES)
    q, k, vt = _mla_proj(h.reshape(M, D), g, wd_p, q_norm, kv_norm, wq_p, wk_p, wvt_p, inv_lane, Lp)
    o = _softmax_attention(q.reshape(B, Lp, -1), k.reshape(B, Lp, -1), vt,
                           width=2 * LANES, n_pairs=N_HEADS // 2)
    return o.reshape(M, -1), w_o.astype(BF16)


def kernel(x, meta, norm_mix, norm_ffn, pool_w, pool_scale, sb_w_qkv, sb_w_o, mla_w_down, mla_q_norm,
           mla_kv_norm, mla_w_uq, mla_w_ukv, mla_w_o, fox_w_qkvf, fox_b_f, fox_w_o, ffn_w_gate, ffn_w_up,
           ffn_w_down, final_norm):
    B, S, D = x.shape
    assert D == D_MODEL and S % ATT_TILE == 0 and PAD_FRONT % ATT_TILE == 0 and PAD_FRONT % SB_TILE == 0
    Lp = S + PAD_FRONT
    M = B * Lp
    HD = N_HEADS * HEAD_DIM
    n_pairs = N_HEADS // 2

    def sb_mixer(h, i, j):
        qk, vt = _qk_vt_proj(h.reshape(M, D), norm_mix[i], sb_w_qkv[j][:, :2 * HD].astype(BF16),
                             sb_w_qkv[j][:, 2 * HD:].T.astype(BF16), q_scale=HEAD_DIM ** -0.5, q_cols=HD)
        o = _sb_attention(qk.reshape(B, Lp, 2 * HD), vt, n_pairs=n_pairs)
        return o.reshape(M, HD), sb_w_o[j].astype(BF16)

    def mla_mixer(h, i, j):
        return _mla_layer(h, norm_mix[i], mla_w_down[j], mla_q_norm[j], mla_kv_norm[j], mla_w_uq[j],
                          mla_w_ukv[j], mla_w_o[j])

    def fox_mixer(h, i, j):
        wf = jnp.pad(fox_w_qkvf[j][:, 3 * HD:], ((0, 0), (0, LANES - N_HEADS))).astype(BF16)
        qk, vt, flog = _qk_vt_proj(h.reshape(M, D), norm_mix[i], fox_w_qkvf[j][:, :2 * HD].astype(BF16),
                                   fox_w_qkvf[j][:, 2 * HD:3 * HD].T.astype(BF16), wf,
                                   q_scale=HEAD_DIM ** -0.5 * LOG2E, q_cols=HD)
        b_lane = jnp.pad(fox_b_f[j].astype(F32), (0, LANES - N_HEADS)).reshape(1, LANES)
        kf, frow = _forget_cumsum(flog.reshape(B, Lp, LANES), b_lane)
        qk = qk.reshape(B, Lp, 2 * HD)
        o = _softmax_attention(qk, qk, vt, width=LANES, n_pairs=n_pairs, offs=(0, n_pairs), kf=kf, frow=frow)
        return o.reshape(M, HD), fox_w_o[j].astype(BF16)

    mixers = (None, sb_mixer, mla_mixer, fox_mixer)
    h = None
    for i in range(norm_mix.shape[0]):
        m, j = i % len(mixers), i // len(mixers)
        ffn_w = (norm_ffn[i], ffn_w_gate[i].astype(BF16), ffn_w_up[i].astype(BF16), ffn_w_down[i].astype(BF16))
        if m == 0:
            assert i == 0, "the pooling mixer doubles as the layout stage and must come first"
            h = _pool_layer(x, meta, norm_mix[i], pool_w[j], pool_scale[j])
            h = _ffn(h.reshape(M, D), *ffn_w).reshape(B, Lp, D)
        else:
            o2, wo = mixers[m](h, i, j)
            h = _ffn(h.reshape(M, D), *ffn_w, o2=o2, wo=wo).reshape(B, Lp, D)
    return _final_norm(h, final_norm, S)
```

```python
import functools

import jax
import jax.numpy as jnp
from jax import lax
from jax.experimental import pallas as pl
from jax.experimental.pallas import tpu as pltpu

F32 = jnp.float32
BF16 = jnp.bfloat16

D_MODEL = 1024
N_META = 16
EPS = 1e-6
POOL_WINDOWS = (2, 4, 8, 16)
POOL_GROUP = D_MODEL // len(POOL_WINDOWS)
N_HEADS = 16
HEAD_DIM = 64
MLA_Q_RANK = 384
MLA_KV_RANK = 256
MLA_NOPE = 64
MLA_ROPE = 32
MLA_V = 64
ROPE_THETA = 10000.0

LANES = 128
PAD_FRONT = 256
ROW0 = PAD_FRONT - N_META
HALO = max(POOL_WINDOWS)
MASK_VALUE = -1e30
LOG2E = 1.4426950408889634
DECAY_PIECES = 3
ACC_ROWS = LANES + 16
DEAD_LOG_WEIGHT = -104.0
VMEM_LIMIT = 56 * 1024 * 1024

ROW_TILE = 1024
ATT_TILE = 256
ATT_GROUP = 4
SB_TILE = 128
SB_LEAD = 3
SB_GROUP = 8
PROJ_TILE = 512
FF_TILE = 256


def _divisor_tile(n, target):
    best = None
    for t in range(8, min(n, target) + 1, 8):
        if n % t == 0:
            best = t
    assert best is not None, (n, target)
    return best


def _params(*sem):
    return pltpu.CompilerParams(dimension_semantics=sem, vmem_limit_bytes=VMEM_LIMIT)


def _rmsnorm(x, g):
    return x * lax.rsqrt(jnp.mean(x * x, axis=-1, keepdims=True) + EPS) * g


def _log_sigmoid_neg(z):
    return -(jnp.maximum(z, 0.0) + jnp.log(1.0 + jnp.exp(-jnp.abs(z))))


def _dot(a, b):
    return jnp.dot(a, b, preferred_element_type=F32)


def _dot_nt(a, b):
    return lax.dot_general(a, b, (((1,), (1,)), ((), ())), preferred_element_type=F32)


def _pool_kernel(x_ref, meta_ref, g_ref, w_ref, sc_ref, o_ref, abuf, *, tile):
    t = pl.program_id(1)
    front = jnp.concatenate([jnp.zeros((ROW0, D_MODEL), F32), meta_ref[...]], axis=0)
    h = jnp.where(t == 0, front, x_ref[0])
    row = t * tile + lax.broadcasted_iota(jnp.int32, (tile, 1), 0)
    a = jnp.where(row >= ROW0, _rmsnorm(h, g_ref[...]), 0.0)

    @pl.when(t == 0)
    def _():
        abuf[0:HALO, :] = jnp.zeros((HALO, D_MODEL), F32)

    @pl.when(t > 0)
    def _():
        abuf[0:HALO, :] = abuf[tile:tile + HALO, :]

    abuf[HALO:HALO + tile, :] = a
    pos1 = row - ROW0 + 1
    outs = []
    for g, win in enumerate(POOL_WINDOWS):
        cols = slice(g * POOL_GROUP, (g + 1) * POOL_GROUP)
        acc = abuf[:, cols]
        shift = 1
        while shift < win:
            acc = acc + pltpu.roll(acc, shift, 0)
            shift *= 2
        xg = a[:, cols]
        cnt = jnp.clip(pos1, 1, win).astype(F32)
        pooled = acc[HALO:] / cnt - xg
        outs.append(_dot(pooled.astype(BF16), w_ref[g]))
    mixed = jnp.concatenate(outs, axis=1) * sc_ref[...]
    o_ref[0] = h + mixed


def _pool_layer(x, meta, g, w, sc):
    B, S, D = x.shape
    tile = PAD_FRONT
    Lp = S + PAD_FRONT
    kern = functools.partial(_pool_kernel, tile=tile)
    return pl.pallas_call(
        kern,
        grid=(B, Lp // tile),
        in_specs=[
            pl.BlockSpec((1, tile, D), lambda b, t: (b, jnp.maximum(t - 1, 0), 0)),
            pl.BlockSpec((N_META, D), lambda b, t: (0, 0)),
            pl.BlockSpec((1, D), lambda b, t: (0, 0)),
            pl.BlockSpec((len(POOL_WINDOWS), POOL_GROUP, POOL_GROUP), lambda b, t: (0, 0, 0)),
            pl.BlockSpec((1, D), lambda b, t: (0, 0)),
        ],
        out_specs=pl.BlockSpec((1, tile, D), lambda b, t: (b, t, 0)),
        out_shape=jax.ShapeDtypeStruct((B, Lp, D), F32),
        scratch_shapes=[pltpu.VMEM((tile + HALO, D), F32)],
        compiler_params=_params("arbitrary", "arbitrary"),
        name="pool_layer",
    )(x, meta.astype(x.dtype), g.reshape(1, D), w.astype(BF16), sc.reshape(1, D))


def _qk_vt_proj_kernel(*refs, q_scale, q_cols, tn, has_gate):
    if has_gate:
        h_ref, g_ref, w_ref, wvt_ref, wf_ref, o_ref, vt_ref, f_ref = refs
    else:
        h_ref, g_ref, w_ref, wvt_ref, o_ref, vt_ref = refs
    a = _rmsnorm(h_ref[...], g_ref[...]).astype(BF16)
    for c in range(w_ref.shape[1] // tn):
        cols = slice(c * tn, (c + 1) * tn)
        part = _dot(a, w_ref[:, cols])
        if (c + 1) * tn <= q_cols:
            part = part * q_scale
        o_ref[:, cols] = part.astype(o_ref.dtype)
    vt_ref[...] = _dot_nt(wvt_ref[...], a).astype(vt_ref.dtype)
    if has_gate:
        f_ref[...] = _dot(a, wf_ref[...])


def _qk_vt_proj(h2, g, w_qk, w_vt, wf=None, *, q_scale, q_cols):
    M, D = h2.shape
    N = w_qk.shape[1]
    Dv = w_vt.shape[0]
    tn = PROJ_TILE
    assert N % tn == 0 and q_cols % tn == 0
    tm = _divisor_tile(M, ROW_TILE)
    has_gate = wf is not None
    resident = lambda shape: pl.BlockSpec(shape, lambda i: (0, 0), pipeline_mode=pl.Buffered(1))
    in_specs = [pl.BlockSpec((tm, D), lambda i: (i, 0)), resident((1, D)), resident((D, N)), resident((Dv, D))]
    out_specs = [pl.BlockSpec((tm, N), lambda i: (i, 0)), pl.BlockSpec((Dv, tm), lambda i: (0, i))]
    out_shape = [jax.ShapeDtypeStruct((M, N), BF16), jax.ShapeDtypeStruct((Dv, M), BF16)]
    args = [h2, g.reshape(1, D), w_qk, w_vt]
    if has_gate:
        in_specs.append(resident((D, LANES)))
        out_specs.append(pl.BlockSpec((tm, LANES), lambda i: (i, 0)))
        out_shape.append(jax.ShapeDtypeStruct((M, LANES), F32))
        args.append(wf)
    return pl.pallas_call(
        functools.partial(_qk_vt_proj_kernel, q_scale=q_scale, q_cols=q_cols, tn=tn, has_gate=has_gate),
        grid=(M // tm,),
        in_specs=in_specs,
        out_specs=out_specs,
        out_shape=out_shape,
        compiler_params=_params("arbitrary"),
        name="qk_vt_proj",
    )(*args)


def _ffn_kernel(*refs, tf, has_mix):
    if has_mix:
        h_ref, o_ref, wo_ref, g_ref, wg_ref, wu_ref, wd_ref, out_ref, acc_scr = refs
        h = h_ref[...] + _dot(o_ref[...], wo_ref[...])
    else:
        h_ref, g_ref, wg_ref, wu_ref, wd_ref, out_ref, acc_scr = refs
        h = h_ref[...]
    a = _rmsnorm(h, g_ref[...]).astype(BF16)
    for c in range(wg_ref.shape[1] // tf):
        cols = slice(c * tf, (c + 1) * tf)
        gate = _dot(a, wg_ref[:, cols])
        up = _dot(a, wu_ref[:, cols])
        act = (gate * jax.nn.sigmoid(gate) * up).astype(BF16)
        part = _dot(act, wd_ref[cols, :])
        if c == 0:
            acc_scr[...] = part
        else:
            acc_scr[...] += part
    out_ref[...] = h + acc_scr[...]


def _ffn(h2, g, wg, wu, wd, o2=None, wo=None):
    M, D = h2.shape
    F = wg.shape[1]
    assert F % FF_TILE == 0
    tm = _divisor_tile(M, ROW_TILE)
    has_mix = o2 is not None
    resident = lambda shape: pl.BlockSpec(shape, lambda i: (0, 0), pipeline_mode=pl.Buffered(1))
    rows = lambda width: pl.BlockSpec((tm, width), lambda i: (i, 0))
    in_specs, args = [rows(D)], [h2]
    if has_mix:
        in_specs += [rows(o2.shape[1]), resident(wo.shape)]
        args += [o2, wo]
    in_specs += [resident((1, D)), resident((D, F)), resident((D, F)), resident((F, D))]
    args += [g.reshape(1, D), wg, wu, wd]
    return pl.pallas_call(
        functools.partial(_ffn_kernel, tf=FF_TILE, has_mix=has_mix),
        grid=(M // tm,),
        in_specs=in_specs,
        out_specs=rows(D),
        out_shape=jax.ShapeDtypeStruct((M, D), F32),
        scratch_shapes=[pltpu.VMEM((tm, D), F32)],
        compiler_params=_params("arbitrary"),
        name="ffn",
    )(*args)


def _split_heads(x, width):
    if width == 2 * LANES:
        return x[:, :LANES], x[:, LANES:]
    lane = lax.broadcasted_iota(jnp.int32, (1, LANES), 1)
    zero = jnp.zeros_like(x)
    return jnp.where(lane < HEAD_DIM, x, zero), jnp.where(lane >= HEAD_DIM, x, zero)


def _split_keys(x, width):
    if width == 2 * LANES:
        return x[:, :LANES], x[:, LANES:]
    return x, x


def _pair_values_t(vt):
    sub = lax.broadcasted_iota(jnp.int32, (LANES, 1), 0)
    zero = jnp.zeros_like(vt)
    return jnp.concatenate(
        [jnp.where(sub < HEAD_DIM, vt, zero), jnp.where(sub >= HEAD_DIM, vt, zero)], axis=1)


def _pair_rows(x0, x1):
    sub = lax.broadcasted_iota(jnp.int32, (ACC_ROWS, 1), 0)
    return jnp.where((sub < HEAD_DIM) | (sub == LANES), x0, x1)


def _sum_rows(tile):
    sub = lax.broadcasted_iota(jnp.int32, (ACC_ROWS - LANES, 2 * tile), 0)
    col = lax.broadcasted_iota(jnp.int32, (ACC_ROWS - LANES, 2 * tile), 1)
    return jnp.where(sub == col // tile, 1.0, 0.0).astype(BF16)


def _softmax_attn_kernel(*refs, tile, width, decay, group):
    if decay:
        q_ref, k_ref, vt_ref, kf_ref, frow_ref, o_ref = refs[:6]
    else:
        q_ref, k_ref, vt_ref, o_ref = refs[:4]
    m_scr, acc_scr, raw0, raw1, p0, p1, al0, al1, mx0, mx1 = refs[-10:]
    raw_s, p_s, al_s, mx_s = (raw0, raw1), (p0, p1), (al0, al1), (mx0, mx1)
    hg = pl.program_id(1)
    qi = pl.program_id(2)
    n_heads = 2 * group
    q = q_ref[0]
    qs = []
    for g in range(group):
        qs += list(_split_heads(q[:, g * width:(g + 1) * width], width))
    qcol = qi * tile + lax.broadcasted_iota(jnp.int32, (1, tile), 1)
    if decay:
        lane = lax.broadcasted_iota(jnp.int32, (1, LANES), 1)
        for h in range(n_heads):
            e = h % 2
            ones = jnp.where((lane >= e * DECAY_PIECES) & (lane < (e + 1) * DECAY_PIECES), 1.0, 0.0)
            qs[h] = jnp.concatenate([qs[h], jnp.broadcast_to(ones.astype(BF16), (tile, LANES))], axis=1)
        head0 = n_heads * hg
        fq = tuple(frow_ref[0, pl.ds(head0 + h, 1), pl.ds(pl.multiple_of(qi * tile, tile), tile)] * LOG2E
                   for h in range(n_heads))

    sum_rows = _sum_rows(tile)

    def meta_keys():
        rows = slice(ROW0, PAD_FRONT)
        k = k_ref[0, rows, :]
        key = ROW0 + lax.broadcasted_iota(jnp.int32, (N_META, 1), 0)
        valid = key <= qcol
        ps = []
        for h in range(n_heads):
            g, e = divmod(h, 2)
            kk = _split_keys(k[:, g * width:(g + 1) * width], width)[e]
            if decay:
                kk = jnp.concatenate([kk, kf_ref[0, rows, g * LANES:(g + 1) * LANES]], axis=1)
            raw = jnp.where(valid, _dot_nt(kk, qs[h]), MASK_VALUE)
            m_new = jnp.max(raw, axis=0, keepdims=True)
            if decay:
                m_new = m_new + fq[h]
                ps.append(jnp.exp2(raw - (m_new - fq[h])).astype(BF16))
            else:
                ps.append(jnp.exp2(raw - m_new).astype(BF16))
            m_scr[h] = m_new
        vt = vt_ref[:, rows]
        meta_sums = _sum_rows(N_META)
        for g in range(group):
            lhs = jnp.concatenate([_pair_values_t(vt[g * LANES:(g + 1) * LANES]), meta_sums], axis=0)
            acc_scr[g * ACC_ROWS:(g + 1) * ACC_ROWS, :] = _dot(lhs, jnp.concatenate(ps[2 * g:2 * g + 2], axis=0))

    def scores(t, s):
        start = pl.multiple_of(t * tile, tile)
        k = k_ref[0, pl.ds(start, tile), :]
        if decay:
            kf = kf_ref[0, pl.ds(start, tile), :]
        for g in range(group):
            ks = _split_keys(k[:, g * width:(g + 1) * width], width)
            if decay:
                ks = tuple(jnp.concatenate([kk, kf[:, g * LANES:(g + 1) * LANES]], axis=1) for kk in ks)
            for e in range(2):
                raw = _dot_nt(ks[e], qs[2 * g + e])
                raw_s[s][2 * g + e] = raw
                mx_s[s][2 * g + e] = jnp.max(raw, axis=0, keepdims=True)

    def weights(t, s, masked):
        start = pl.multiple_of(t * tile, tile)
        if masked:
            key = start + lax.broadcasted_iota(jnp.int32, (tile, 1), 0)
            valid = key <= qcol
        for h in range(n_heads):
            raw = raw_s[s][h]
            m_old = m_scr[h]
            if masked:
                raw = jnp.where(valid, raw, MASK_VALUE)
                top = jnp.max(raw, axis=0, keepdims=True)
            else:
                top = mx_s[s][h]
            if decay:
                m_new = jnp.maximum(m_old, top + fq[h])
                p = jnp.exp2(raw - (m_new - fq[h]))
            else:
                m_new = jnp.maximum(m_old, top)
                p = jnp.exp2(raw - m_new)
            m_scr[h] = m_new
            p_s[s][h * tile:(h + 1) * tile, :] = p.astype(BF16)
            al_s[s][h] = jnp.exp2(m_old - m_new)

    def accumulate(t, s):
        vt = vt_ref[:, pl.ds(pl.multiple_of(t * tile, tile), tile)]
        for g in range(group):
            lhs = jnp.concatenate([_pair_values_t(vt[g * LANES:(g + 1) * LANES]), sum_rows], axis=0)
            pv = _dot(lhs, p_s[s][2 * g * tile:(2 * g + 2) * tile, :])
            rows = slice(g * ACC_ROWS, (g + 1) * ACC_ROWS)
            acc_scr[rows, :] = _pair_rows(al_s[s][2 * g], al_s[s][2 * g + 1]) * acc_scr[rows, :] + pv

    meta_keys()

    @pl.when(qi > 0)
    def _():
        p_s[0][...] = jnp.zeros_like(p_s[0])
        al_s[0][...] = jnp.ones_like(al_s[0])
        scores(1, 1)
        n_steady = qi - 1

        def pair(i, c):
            t = 2 * i + 1
            scores(t + 1, 0)
            weights(t, 1, False)
            accumulate(t - 1, 0)
            scores(t + 2, 1)
            weights(t + 1, 0, False)
            accumulate(t, 1)
            return c

        lax.fori_loop(0, n_steady // 2, pair, 0)

        @pl.when(n_steady % 2 == 1)
        def _():
            t = qi - 1
            scores(qi, 0)
            weights(t, 1, False)
            accumulate(t - 1, 0)
            weights(qi, 0, True)
            accumulate(t, 1)
            accumulate(qi, 0)

        @pl.when(n_steady % 2 == 0)
        def _():
            weights(qi, 1, True)
            accumulate(qi - 1, 0)
            accumulate(qi, 1)

    sub = lax.broadcasted_iota(jnp.int32, (LANES, 1), 0)
    for g in range(group):
        acc = acc_scr[g * ACC_ROWS:(g + 1) * ACC_ROWS, :]
        total = jnp.where(sub < HEAD_DIM, acc[LANES:LANES + 1], acc[LANES + 1:LANES + 2])
        o_ref[0, :, g * LANES:(g + 1) * LANES] = (acc[:LANES] / total).T.astype(o_ref.dtype)


def _softmax_attention(q, k, vt, *, width, n_pairs, offs=(0, 0), kf=None, frow=None):
    B, Lp, _ = q.shape
    qo, ko = offs
    tile = ATT_TILE
    group = ATT_GROUP
    assert n_pairs % group == 0 and qo % group == 0 and ko % group == 0
    n_heads = 2 * group
    decay = kf is not None
    kern = functools.partial(_softmax_attn_kernel, tile=tile, width=width, decay=decay, group=group)
    in_specs = [
        pl.BlockSpec((1, tile, group * width), lambda b, h, i: (b, i, qo // group + h)),
        pl.BlockSpec((1, Lp, group * width), lambda b, h, i: (b, 0, ko // group + h), pipeline_mode=pl.Buffered(1)),
        pl.BlockSpec((group * LANES, Lp), lambda b, h, i: (h, b)),
    ]
    args = [q, k, vt]
    if decay:
        in_specs += [
            pl.BlockSpec((1, Lp, group * LANES), lambda b, h, i: (b, 0, h), pipeline_mode=pl.Buffered(1)),
            pl.BlockSpec((1, N_HEADS, Lp), lambda b, h, i: (b, 0, 0)),
        ]
        args += [kf, frow]
    return pl.pallas_call(
        kern,
        grid=(B, n_pairs // group, Lp // tile),
        in_specs=in_specs,
        out_specs=pl.BlockSpec((1, tile, group * LANES), lambda b, h, i: (b, i, h)),
        out_shape=jax.ShapeDtypeStruct((B, Lp, n_pairs * LANES), BF16),
        scratch_shapes=[
            pltpu.VMEM((n_heads, 1, tile), F32),
            pltpu.VMEM((group * ACC_ROWS, tile), F32),
            pltpu.VMEM((n_heads, tile, tile), F32),
            pltpu.VMEM((n_heads, tile, tile), F32),
            pltpu.VMEM((n_heads * tile, tile), BF16),
            pltpu.VMEM((n_heads * tile, tile), BF16),
            pltpu.VMEM((n_heads, 1, tile), F32),
            pltpu.VMEM((n_heads, 1, tile), F32),
            pltpu.VMEM((n_heads, 1, tile), F32),
            pltpu.VMEM((n_heads, 1, tile), F32),
        ],
        compiler_params=_params("arbitrary", "arbitrary", "arbitrary"),
        name="fox_attention" if decay else "mla_attention",
    )(*args)


def _sb_attn_kernel(q_ref, k_ref, vt_ref, tri_ref, o_ref, r_scr, acc_scr, *, tile, group):
    qi = pl.program_id(2)
    n_heads = 2 * group
    first = ROW0 // tile
    q = q_ref[0]
    qs = []
    for g in range(group):
        qs += list(_split_heads(q[:, g * LANES:(g + 1) * LANES], LANES))
    qcol = qi * tile + lax.broadcasted_iota(jnp.int32, (1, tile), 1)
    r_scr[...] = jnp.zeros_like(r_scr)
    acc_scr[...] = jnp.zeros_like(acc_scr)
    tri2 = tri_ref[...]

    def walk(tiles):
        ks, vts, valids = [], [], []
        for t, masked in tiles:
            start = pl.multiple_of(t * tile, tile)
            ks.append(k_ref[0, pl.ds(start, tile), :])
            vts.append(vt_ref[:, pl.ds(start, tile)])
            key = start + lax.broadcasted_iota(jnp.int32, (tile, 1), 0)
            valids.append((key >= ROW0) & (key < qcol) if masked else None)
        n = len(tiles)
        zs = [[_dot_nt(ks[i][:, (h // 2) * LANES:(h // 2 + 1) * LANES], qs[h]) for h in range(n_heads)]
              for i in range(n)]
        lks = [[None] * n_heads for _ in range(n)]
        for i in range(n):
            for h in range(n_heads):
                lk = _log_sigmoid_neg(zs[i][h])
                lks[i][h] = lk if valids[i] is None else jnp.where(valids[i], lk, 0.0)
        laters = [[None] * n_heads for _ in range(n)]
        for i in range(n):
            for h in range(n_heads):
                lk_hi = lks[i][h].astype(BF16)
                lk_lo = (lks[i][h] - lk_hi.astype(F32)).astype(BF16)
                laters[i][h] = _dot(tri2, jnp.concatenate([lk_hi, lk_lo], axis=0))
        ws = [[None] * n_heads for _ in range(n)]
        for h in range(n_heads):
            r = r_scr[h]
            for i in range(n):
                w = jnp.exp(lks[i][h] + zs[i][h] + (laters[i][h] + r))
                ws[i][h] = (w if valids[i] is None else jnp.where(valids[i], w, 0.0)).astype(BF16)
                r = r + jnp.sum(lks[i][h], axis=0, keepdims=True)
            r_scr[h] = r
        for g in range(group):
            rows = slice(g * LANES, (g + 1) * LANES)
            lhs = jnp.concatenate([_pair_values_t(vts[i][rows]) for i in range(n)], axis=1)
            rhs = jnp.concatenate([ws[i][h] for i in range(n) for h in (2 * g, 2 * g + 1)], axis=0)
            acc_scr[rows, :] += _dot(lhs, rhs)

    def live():
        return (jnp.max(r_scr[...]) > DEAD_LOG_WEIGHT).astype(jnp.int32)

    lead = qi - (SB_LEAD - 1) > first

    @pl.when(lead)
    def _():
        walk(tuple((qi - i, i == 0) for i in range(SB_LEAD)))

    @pl.when(jnp.logical_not(lead))
    def _():
        walk(((qi, True),))

    def body(c):
        walk(((c[0], False),))
        return c[0] - 1, live()

    t0 = jnp.where(lead, qi - SB_LEAD, qi - 1)
    _, alive = lax.while_loop(lambda c: (c[0] > first) & (c[1] > 0), body, (t0, live()))

    @pl.when((qi > first) & (alive > 0))
    def _():
        walk(((first, True),))

    for g in range(group):
        rows = slice(g * LANES, (g + 1) * LANES)
        o_ref[0, :, rows] = acc_scr[rows, :].T.astype(o_ref.dtype)


def _sb_attention(qk, vt, *, n_pairs):
    B, Lp, _ = qk.shape
    tile, group = SB_TILE, SB_GROUP
    assert n_pairs % group == 0
    n_groups = n_pairs // group
    width = group * LANES
    r = lax.broadcasted_iota(jnp.int32, (tile, tile), 0)
    c = lax.broadcasted_iota(jnp.int32, (tile, tile), 1)
    tri = (c > r).astype(BF16)
    tri2 = jnp.concatenate([tri, tri], axis=1)
    kern = functools.partial(_sb_attn_kernel, tile=tile, group=group)
    return pl.pallas_call(
        kern,
        grid=(B, n_groups, Lp // tile),
        in_specs=[
            pl.BlockSpec((1, tile, width), lambda b, h, i: (b, i, h)),
            pl.BlockSpec((1, Lp, width), lambda b, h, i: (b, 0, n_groups + h), pipeline_mode=pl.Buffered(1)),
            pl.BlockSpec((width, Lp), lambda b, h, i: (h, b), pipeline_mode=pl.Buffered(1)),
            pl.BlockSpec((tile, 2 * tile), lambda b, h, i: (0, 0)),
        ],
        out_specs=pl.BlockSpec((1, tile, width), lambda b, h, i: (b, i, h)),
        out_shape=jax.ShapeDtypeStruct((B, Lp, n_pairs * LANES), BF16),
        scratch_shapes=[
            pltpu.VMEM((2 * group, 1, tile), F32),
            pltpu.VMEM((width, tile), F32),
        ],
        compiler_params=_params("arbitrary", "arbitrary", "arbitrary"),
        name="sb_attention",
    )(qk, qk, vt, tri2)


def _rope_table_kernel(inv_ref, o_ref, *, tile):
    row = pl.program_id(0) * tile + lax.broadcasted_iota(jnp.int32, (tile, 1), 0)
    ang = (row - ROW0).astype(F32) * inv_ref[...]
    lane = lax.broadcasted_iota(jnp.int32, (1, LANES), 1)
    half = MLA_ROPE // 2
    in_lo = (lane >= MLA_NOPE) & (lane < MLA_NOPE + half)
    in_hi = (lane >= MLA_NOPE + half) & (lane < MLA_NOPE + MLA_ROPE)
    cos = jnp.cos(ang)
    sin = jnp.sin(ang)
    o_ref[0] = jnp.where(lane < MLA_NOPE, 1.0, jnp.where(in_lo | in_hi, cos, 0.0))
    o_ref[1] = jnp.where(in_lo, -sin, 0.0)
    o_ref[2] = jnp.where(in_hi, sin, 0.0)


def _rope_tables(rows, inv_lane):
    tile = _divisor_tile(rows, 512)
    return pl.pallas_call(
        functools.partial(_rope_table_kernel, tile=tile),
        grid=(rows // tile,),
        in_specs=[pl.BlockSpec((1, LANES), lambda i: (0, 0))],
        out_specs=pl.BlockSpec((3, tile, LANES), lambda i: (0, i, 0)),
        out_shape=jax.ShapeDtypeStruct((3, rows, LANES), F32),
        compiler_params=_params("arbitrary"),
        name="rope_tables",
    )(inv_lane)


def _mla_proj_kernel(h_ref, g_ref, wd_ref, qn_ref, kvn_ref, wq_ref, wk_ref, wvt_ref, tab_ref,
                     q_ref, k_ref, vt_ref, *, q_scale):
    a = _rmsnorm(h_ref[...], g_ref[...]).astype(BF16)
    down = _dot(a, wd_ref[...])
    cq = _rmsnorm(down[:, :MLA_Q_RANK], qn_ref[...]).astype(BF16)
    ckv = _rmsnorm(down[:, MLA_Q_RANK:MLA_Q_RANK + MLA_KV_RANK], kvn_ref[...]).astype(BF16)
    kr = down[:, MLA_Q_RANK + MLA_KV_RANK:]
    c_tab, s_lo, s_hi = tab_ref[0], tab_ref[1], tab_ref[2]
    half = MLA_ROPE // 2

    def rope(x):
        return (x * c_tab + pltpu.roll(x, LANES - half, 1) * s_lo + pltpu.roll(x, half, 1) * s_hi)

    q = _dot(cq, wq_ref[...])
    kn = _dot(ckv, wk_ref[...])
    vt_ref[...] = _dot_nt(wvt_ref[...], ckv).astype(BF16)
    kr_rot = rope(kr)
    for hd in range(N_HEADS):
        cols = slice(hd * LANES, (hd + 1) * LANES)
        q_ref[:, cols] = (rope(q[:, cols]) * q_scale).astype(BF16)
        k_ref[:, cols] = (kn[:, cols] + kr_rot).astype(BF16)


def _mla_proj(h2, g, wd, qn, kvn, wq, wk, wvt, inv_lane, rows_per_batch):
    M, D = h2.shape
    tm = _divisor_tile(rows_per_batch, 512)
    tiles_per_batch = rows_per_batch // tm
    tables = _rope_tables(rows_per_batch, inv_lane)
    kern = functools.partial(_mla_proj_kernel, q_scale=(MLA_NOPE + MLA_ROPE) ** -0.5 * LOG2E)
    full = lambda a: pl.BlockSpec(a.shape, lambda i: (0,) * a.ndim)
    g2, qn2, kvn2 = g.reshape(1, D), qn.reshape(1, -1), kvn.reshape(1, -1)
    return pl.pallas_call(
        kern,
        grid=(M // tm,),
        in_specs=[pl.BlockSpec((tm, D), lambda i: (i, 0)), full(g2), full(wd), full(qn2), full(kvn2),
                  full(wq), full(wk), full(wvt),
                  pl.BlockSpec((3, tm, LANES), lambda i: (0, i % tiles_per_batch, 0))],
        out_specs=[
            pl.BlockSpec((tm, N_HEADS * LANES), lambda i: (i, 0)),
            pl.BlockSpec((tm, N_HEADS * LANES), lambda i: (i, 0)),
            pl.BlockSpec((N_HEADS * MLA_V, tm), lambda i: (0, i)),
        ],
        out_shape=[
            jax.ShapeDtypeStruct((M, N_HEADS * LANES), BF16),
            jax.ShapeDtypeStruct((M, N_HEADS * LANES), BF16),
            jax.ShapeDtypeStruct((N_HEADS * MLA_V, M), BF16),
        ],
        compiler_params=_params("arbitrary"),
        name="mla_proj",
    )(h2, g2, wd, qn2, kvn2, wq, wk, wvt, tables)


def _forget_cumsum_kernel(f_ref, b_ref, sel_ref, kf_ref, frow_ref, carry, *, tile):
    t = pl.program_id(1)

    @pl.when(t == 0)
    def _():
        carry[...] = jnp.zeros_like(carry)

    row_in = lax.broadcasted_iota(jnp.int32, (tile, 1), 0)
    row = t * tile + row_in
    x = jnp.where(row >= ROW0, _log_sigmoid_neg(-(f_ref[0] + b_ref[...])), 0.0)
    shift = 1
    while shift < tile:
        x = x + jnp.where(row_in >= shift, pltpu.roll(x, shift, 0), 0.0)
        shift *= 2
    x = x + carry[...]
    carry[...] = x[tile - 1:tile, :]
    frow_ref[0] = x.T[:N_HEADS, :]
    rest = -x * LOG2E
    kf = jnp.zeros(kf_ref.shape[1:], F32)
    for piece in range(DECAY_PIECES):
        part = rest.astype(BF16)
        rest = rest - part.astype(F32)
        kf = kf + _dot(part, sel_ref[piece])
    kf_ref[0] = kf.astype(BF16)


def _forget_cumsum(flog, b_lane):
    B, Lp, _ = flog.shape
    tile = _divisor_tile(Lp, ROW_TILE)
    n_pairs = N_HEADS // 2
    head = lax.broadcasted_iota(jnp.int32, (DECAY_PIECES, LANES, n_pairs * LANES), 1)
    col = lax.broadcasted_iota(jnp.int32, (DECAY_PIECES, LANES, n_pairs * LANES), 2)
    piece = lax.broadcasted_iota(jnp.int32, (DECAY_PIECES, LANES, n_pairs * LANES), 0)
    sel = ((head < N_HEADS) & (col == (head // 2) * LANES + (head % 2) * DECAY_PIECES + piece)).astype(BF16)
    kern = functools.partial(_forget_cumsum_kernel, tile=tile)
    return pl.pallas_call(
        kern,
        grid=(B, Lp // tile),
        in_specs=[
            pl.BlockSpec((1, tile, LANES), lambda b, t: (b, t, 0)),
            pl.BlockSpec((1, LANES), lambda b, t: (0, 0)),
            pl.BlockSpec(sel.shape, lambda b, t: (0, 0, 0)),
        ],
        out_specs=[
            pl.BlockSpec((1, tile, n_pairs * LANES), lambda b, t: (b, t, 0)),
            pl.BlockSpec((1, N_HEADS, tile), lambda b, t: (b, 0, t)),
        ],
        out_shape=[
            jax.ShapeDtypeStruct((B, Lp, n_pairs * LANES), BF16),
            jax.ShapeDtypeStruct((B, N_HEADS, Lp), F32),
        ],
        scratch_shapes=[pltpu.VMEM((1, LANES), F32)],
        compiler_params=_params("arbitrary", "arbitrary"),
        name="forget_cumsum",
    )(flog, b_lane, sel)


def _final_norm_kernel(h_ref, g_ref, o_ref):
    o_ref[0] = _rmsnorm(h_ref[0], g_ref[...])


def _final_norm(h, g, seq):
    B, Lp, D = h.shape
    tile = PAD_FRONT
    skip = PAD_FRONT // tile
    return pl.pallas_call(
        _final_norm_kernel,
        grid=(B, seq // tile),
        in_specs=[
            pl.BlockSpec((1, tile, D), lambda b, t: (b, t + skip, 0)),
            pl.BlockSpec((1, D), lambda b, t: (0, 0)),
        ],
        out_specs=pl.BlockSpec((1, tile, D), lambda b, t: (b, t, 0)),
        out_shape=jax.ShapeDtypeStruct((B, seq, D), F32),
        compiler_params=_params("arbitrary", "arbitrary"),
        name="final_norm",
    )(h, g.reshape(1, D))


def _pad_heads(w, per_head):
    K = w.shape[0]
    w3 = w.reshape(K, N_HEADS, per_head)
    w3 = jnp.pad(w3, ((0, 0), (0, 0), (0, LANES - per_head)))
    return w3.reshape(K, N_HEADS * LANES)


def _mla_layer(h, g, w_down, q_norm, kv_norm, w_uq, w_ukv, w_o):
    B, Lp, D = h.shape
    M = B * Lp
    n_lat = MLA_Q_RANK + MLA_KV_RANK
    wd_rope = jnp.pad(w_down[:, n_lat:], ((0, 0), (MLA_NOPE, LANES - MLA_NOPE - MLA_ROPE)))
    wd_p = jnp.concatenate([w_down[:, :n_lat], wd_rope], axis=1).astype(BF16)
    wq_p = _pad_heads(w_uq, MLA_NOPE + MLA_ROPE).astype(BF16)
    wkv3 = w_ukv.reshape(MLA_KV_RANK, N_HEADS, MLA_NOPE + MLA_V)
    wk_p = _pad_heads(wkv3[:, :, :MLA_NOPE].reshape(MLA_KV_RANK, -1), MLA_NOPE).astype(BF16)
    wvt_p = wkv3[:, :, MLA_NOPE:].reshape(MLA_KV_RANK, N_HEADS * MLA_V).T.astype(BF16)
    inv = ROPE_THETA ** (-jnp.arange(0, MLA_ROPE, 2, dtype=F32) / MLA_ROPE)
    inv_lane = jnp.concatenate([jnp.zeros((MLA_NOPE,), F32), inv, inv,
                                jnp.zeros((LANES - MLA_NOPE - MLA_ROPE,), F32)]).reshape(1, LANES)
    q, k, vt = _mla_proj(h.reshape(M, D), g, wd_p, q_norm, kv_norm, wq_p, wk_p, wvt_p, inv_lane, Lp)
    o = _softmax_attention(q.reshape(B, Lp, -1), k.reshape(B, Lp, -1), vt,
                           width=2 * LANES, n_pairs=N_HEADS // 2)
    return o.reshape(M, -1), w_o.astype(BF16)


def kernel(x, meta, norm_mix, norm_ffn, pool_w, pool_scale, sb_w_qkv, sb_w_o, mla_w_down, mla_q_norm,
           mla_kv_norm, mla_w_uq, mla_w_ukv, mla_w_o, fox_w_qkvf, fox_b_f, fox_w_o, ffn_w_gate, ffn_w_up,
           ffn_w_down, final_norm):
    B, S, D = x.shape
    assert D == D_MODEL and S % ATT_TILE == 0 and PAD_FRONT % ATT_TILE == 0 and PAD_FRONT % SB_TILE == 0
    Lp = S + PAD_FRONT
    M = B * Lp
    HD = N_HEADS * HEAD_DIM
    n_pairs = N_HEADS // 2

    def sb_mixer(h, i, j):
        qk, vt = _qk_vt_proj(h.reshape(M, D), norm_mix[i], sb_w_qkv[j][:, :2 * HD].astype(BF16),
                             sb_w_qkv[j][:, 2 * HD:].T.astype(BF16), q_scale=HEAD_DIM ** -0.5, q_cols=HD)
        o = _sb_attention(qk.reshape(B, Lp, 2 * HD), vt, n_pairs=n_pairs)
        return o.reshape(M, HD), sb_w_o[j].astype(BF16)

    def mla_mixer(h, i, j):
        return _mla_layer(h, norm_mix[i], mla_w_down[j], mla_q_norm[j], mla_kv_norm[j], mla_w_uq[j],
                          mla_w_ukv[j], mla_w_o[j])

    def fox_mixer(h, i, j):
        wf = jnp.pad(fox_w_qkvf[j][:, 3 * HD:], ((0, 0), (0, LANES - N_HEADS))).astype(BF16)
        qk, vt, flog = _qk_vt_proj(h.reshape(M, D), norm_mix[i], fox_w_qkvf[j][:, :2 * HD].astype(BF16),
                                   fox_w_qkvf[j][:, 2 * HD:3 * HD].T.astype(BF16), wf,
                                   q_scale=HEAD_DIM ** -0.5 * LOG2E, q_cols=HD)
        b_lane = jnp.pad(fox_b_f[j].astype(F32), (0, LANES - N_HEADS)).reshape(1, LANES)
        kf, frow = _forget_cumsum(flog.reshape(B, Lp, LANES), b_lane)
        qk = qk.reshape(B, Lp, 2 * HD)
        o = _softmax_attention(qk, qk, vt, width=LANES, n_pairs=n_pairs, offs=(0, n_pairs), kf=kf, frow=frow)
        return o.reshape(M, HD), fox_w_o[j].astype(BF16)

    mixers = (None, sb_mixer, mla_mixer, fox_mixer)
    h = None
    for i in range(norm_mix.shape[0]):
        m, j = i % len(mixers), i // len(mixers)
        ffn_w = (norm_ffn[i], ffn_w_gate[i].astype(BF16), ffn_w_up[i].astype(BF16), ffn_w_down[i].astype(BF16))
        if m == 0:
            assert i == 0, "the pooling mixer doubles as the layout stage and must come first"
            h = _pool_layer(x, meta, norm_mix[i], pool_w[j], pool_scale[j])
            h = _ffn(h.reshape(M, D), *ffn_w).reshape(B, Lp, D)
        else:
            o2, wo = mixers[m](h, i, j)
            h = _ffn(h.reshape(M, D), *ffn_w, o2=o2, wo=wo).reshape(B, Lp, D)
    return _final_norm(h, final_norm, S)
```

```python
import functools

import jax
import jax.numpy as jnp
from jax import lax
from jax.experimental import pallas as pl
from jax.experimental.pallas import tpu as pltpu

F32 = jnp.float32
BF16 = jnp.bfloat16

D_MODEL = 1024
N_META = 16
EPS = 1e-6
POOL_WINDOWS = (2, 4, 8, 16)
POOL_GROUP = D_MODEL // len(POOL_WINDOWS)
N_HEADS = 16
HEAD_DIM = 64
MLA_Q_RANK = 384
MLA_KV_RANK = 256
MLA_NOPE = 64
MLA_ROPE = 32
MLA_V = 64
ROPE_THETA = 10000.0

LANES = 128
PAD_FRONT = 256
ROW0 = PAD_FRONT - N_META
HALO = max(POOL_WINDOWS)
MASK_VALUE = -1e30
LOG2E = 1.4426950408889634
DECAY_PIECES = 3
ACC_ROWS = LANES + 16
DEAD_LOG_WEIGHT = -104.0
VMEM_LIMIT = 56 * 1024 * 1024

ROW_TILE = 1024
ATT_TILE = 256
ATT_GROUP = 4
SB_TILE = 128
SB_LEAD = 3
SB_GROUP = 8
PROJ_TILE = 512
FF_TILE = 256


def _divisor_tile(n, target):
    best = None
    for t in range(8, min(n, target) + 1, 8):
        if n % t == 0:
            best = t
    assert best is not None, (n, target)
    return best


def _params(*sem):
    return pltpu.CompilerParams(dimension_semantics=sem, vmem_limit_bytes=VMEM_LIMIT)


def _rmsnorm(x, g):
    return x * lax.rsqrt(jnp.mean(x * x, axis=-1, keepdims=True) + EPS) * g


def _log_sigmoid_neg(z):
    return -(jnp.maximum(z, 0.0) + jnp.log(1.0 + jnp.exp(-jnp.abs(z))))


def _dot(a, b):
    return jnp.dot(a, b, preferred_element_type=F32)


def _dot_nt(a, b):
    return lax.dot_general(a, b, (((1,), (1,)), ((), ())), preferred_element_type=F32)


def _pool_kernel(x_ref, meta_ref, g_ref, w_ref, sc_ref, o_ref, abuf, *, tile):
    t = pl.program_id(1)
    front = jnp.concatenate([jnp.zeros((ROW0, D_MODEL), F32), meta_ref[...]], axis=0)
    h = jnp.where(t == 0, front, x_ref[0])
    row = t * tile + lax.broadcasted_iota(jnp.int32, (tile, 1), 0)
    a = jnp.where(row >= ROW0, _rmsnorm(h, g_ref[...]), 0.0)

    @pl.when(t == 0)
    def _():
        abuf[0:HALO, :] = jnp.zeros((HALO, D_MODEL), F32)

    @pl.when(t > 0)
    def _():
        abuf[0:HALO, :] = abuf[tile:tile + HALO, :]

    abuf[HALO:HALO + tile, :] = a
    pos1 = row - ROW0 + 1
    outs = []
    for g, win in enumerate(POOL_WINDOWS):
        cols = slice(g * POOL_GROUP, (g + 1) * POOL_GROUP)
        acc = abuf[:, cols]
        shift = 1
        while shift < win:
            acc = acc + pltpu.roll(acc, shift, 0)
            shift *= 2
        xg = a[:, cols]
        cnt = jnp.clip(pos1, 1, win).astype(F32)
        pooled = acc[HALO:] / cnt - xg
        outs.append(_dot(pooled.astype(BF16), w_ref[g]))
    mixed = jnp.concatenate(outs, axis=1) * sc_ref[...]
    o_ref[0] = h + mixed


def _pool_layer(x, meta, g, w, sc):
    B, S, D = x.shape
    tile = PAD_FRONT
    Lp = S + PAD_FRONT
    kern = functools.partial(_pool_kernel, tile=tile)
    return pl.pallas_call(
        kern,
        grid=(B, Lp // tile),
        in_specs=[
            pl.BlockSpec((1, tile, D), lambda b, t: (b, jnp.maximum(t - 1, 0), 0)),
            pl.BlockSpec((N_META, D), lambda b, t: (0, 0)),
            pl.BlockSpec((1, D), lambda b, t: (0, 0)),
            pl.BlockSpec((len(POOL_WINDOWS), POOL_GROUP, POOL_GROUP), lambda b, t: (0, 0, 0)),
            pl.BlockSpec((1, D), lambda b, t: (0, 0)),
        ],
        out_specs=pl.BlockSpec((1, tile, D), lambda b, t: (b, t, 0)),
        out_shape=jax.ShapeDtypeStruct((B, Lp, D), F32),
        scratch_shapes=[pltpu.VMEM((tile + HALO, D), F32)],
        compiler_params=_params("arbitrary", "arbitrary"),
        name="pool_layer",
    )(x, meta.astype(x.dtype), g.reshape(1, D), w.astype(BF16), sc.reshape(1, D))


def _qk_vt_proj_kernel(*refs, q_scale, q_cols, tn, has_gate):
    if has_gate:
        h_ref, g_ref, w_ref, wvt_ref, wf_ref, o_ref, vt_ref, f_ref = refs
    else:
        h_ref, g_ref, w_ref, wvt_ref, o_ref, vt_ref = refs
    a = _rmsnorm(h_ref[...], g_ref[...]).astype(BF16)
    for c in range(w_ref.shape[1] // tn):
        cols = slice(c * tn, (c + 1) * tn)
        part = _dot(a, w_ref[:, cols])
        if (c + 1) * tn <= q_cols:
            part = part * q_scale
        o_ref[:, cols] = part.astype(o_ref.dtype)
    vt_ref[...] = _dot_nt(wvt_ref[...], a).astype(vt_ref.dtype)
    if has_gate:
        f_ref[...] = _dot(a, wf_ref[...])


def _qk_vt_proj(h2, g, w_qk, w_vt, wf=None, *, q_scale, q_cols):
    M, D = h2.shape
    N = w_qk.shape[1]
    Dv = w_vt.shape[0]
    tn = PROJ_TILE
    assert N % tn == 0 and q_cols % tn == 0
    tm = _divisor_tile(M, ROW_TILE)
    has_gate = wf is not None
    resident = lambda shape: pl.BlockSpec(shape, lambda i: (0, 0), pipeline_mode=pl.Buffered(1))
    in_specs = [pl.BlockSpec((tm, D), lambda i: (i, 0)), resident((1, D)), resident((D, N)), resident((Dv, D))]
    out_specs = [pl.BlockSpec((tm, N), lambda i: (i, 0)), pl.BlockSpec((Dv, tm), lambda i: (0, i))]
    out_shape = [jax.ShapeDtypeStruct((M, N), BF16), jax.ShapeDtypeStruct((Dv, M), BF16)]
    args = [h2, g.reshape(1, D), w_qk, w_vt]
    if has_gate:
        in_specs.append(resident((D, LANES)))
        out_specs.append(pl.BlockSpec((tm, LANES), lambda i: (i, 0)))
        out_shape.append(jax.ShapeDtypeStruct((M, LANES), F32))
        args.append(wf)
    return pl.pallas_call(
        functools.partial(_qk_vt_proj_kernel, q_scale=q_scale, q_cols=q_cols, tn=tn, has_gate=has_gate),
        grid=(M // tm,),
        in_specs=in_specs,
        out_specs=out_specs,
        out_shape=out_shape,
        compiler_params=_params("arbitrary"),
        name="qk_vt_proj",
    )(*args)


def _ffn_kernel(*refs, tf, has_mix):
    if has_mix:
        h_ref, o_ref, wo_ref, g_ref, wg_ref, wu_ref, wd_ref, out_ref, acc_scr = refs
        h = h_ref[...] + _dot(o_ref[...], wo_ref[...])
    else:
        h_ref, g_ref, wg_ref, wu_ref, wd_ref, out_ref, acc_scr = refs
        h = h_ref[...]
    a = _rmsnorm(h, g_ref[...]).astype(BF16)
    for c in range(wg_ref.shape[1] // tf):
        cols = slice(c * tf, (c + 1) * tf)
        gate = _dot(a, wg_ref[:, cols])
        up = _dot(a, wu_ref[:, cols])
        act = (gate * jax.nn.sigmoid(gate) * up).astype(BF16)
        part = _dot(act, wd_ref[cols, :])
        if c == 0:
            acc_scr[...] = part
        else:
            acc_scr[...] += part
    out_ref[...] = h + acc_scr[...]


def _ffn(h2, g, wg, wu, wd, o2=None, wo=None):
    M, D = h2.shape
    F = wg.shape[1]
    assert F % FF_TILE == 0
    tm = _divisor_tile(M, ROW_TILE)
    has_mix = o2 is not None
    resident = lambda shape: pl.BlockSpec(shape, lambda i: (0, 0), pipeline_mode=pl.Buffered(1))
    rows = lambda width: pl.BlockSpec((tm, width), lambda i: (i, 0))
    in_specs, args = [rows(D)], [h2]
    if has_mix:
        in_specs += [rows(o2.shape[1]), resident(wo.shape)]
        args += [o2, wo]
    in_specs += [resident((1, D)), resident((D, F)), resident((D, F)), resident((F, D))]
    args += [g.reshape(1, D), wg, wu, wd]
    return pl.pallas_call(
        functools.partial(_ffn_kernel, tf=FF_TILE, has_mix=has_mix),
        grid=(M // tm,),
        in_specs=in_specs,
        out_specs=rows(D),
        out_shape=jax.ShapeDtypeStruct((M, D), F32),
        scratch_shapes=[pltpu.VMEM((tm, D), F32)],
        compiler_params=_params("arbitrary"),
        name="ffn",
    )(*args)


def _split_heads(x, width):
    if width == 2 * LANES:
        return x[:, :LANES], x[:, LANES:]
    lane = lax.broadcasted_iota(jnp.int32, (1, LANES), 1)
    zero = jnp.zeros_like(x)
    return jnp.where(lane < HEAD_DIM, x, zero), jnp.where(lane >= HEAD_DIM, x, zero)


def _split_keys(x, width):
    if width == 2 * LANES:
        return x[:, :LANES], x[:, LANES:]
    return x, x


def _pair_values_t(vt):
    sub = lax.broadcasted_iota(jnp.int32, (LANES, 1), 0)
    zero = jnp.zeros_like(vt)
    return jnp.concatenate(
        [jnp.where(sub < HEAD_DIM, vt, zero), jnp.where(sub >= HEAD_DIM, vt, zero)], axis=1)


def _pair_rows(x0, x1):
    sub = lax.broadcasted_iota(jnp.int32, (ACC_ROWS, 1), 0)
    return jnp.where((sub < HEAD_DIM) | (sub == LANES), x0, x1)


def _sum_rows(tile):
    sub = lax.broadcasted_iota(jnp.int32, (ACC_ROWS - LANES, 2 * tile), 0)
    col = lax.broadcasted_iota(jnp.int32, (ACC_ROWS - LANES, 2 * tile), 1)
    return jnp.where(sub == col // tile, 1.0, 0.0).astype(BF16)


def _softmax_attn_kernel(*refs, tile, width, decay, group):
    if decay:
        q_ref, k_ref, vt_ref, kf_ref, frow_ref, o_ref = refs[:6]
    else:
        q_ref, k_ref, vt_ref, o_ref = refs[:4]
    m_scr, acc_scr, raw0, raw1, p0, p1, al0, al1, mx0, mx1 = refs[-10:]
    raw_s, p_s, al_s, mx_s = (raw0, raw1), (p0, p1), (al0, al1), (mx0, mx1)
    hg = pl.program_id(1)
    qi = pl.program_id(2)
    n_heads = 2 * group
    q = q_ref[0]
    qs = []
    for g in range(group):
        qs += list(_split_heads(q[:, g * width:(g + 1) * width], width))
    qcol = qi * tile + lax.broadcasted_iota(jnp.int32, (1, tile), 1)
    if decay:
        lane = lax.broadcasted_iota(jnp.int32, (1, LANES), 1)
        for h in range(n_heads):
            e = h % 2
            ones = jnp.where((lane >= e * DECAY_PIECES) & (lane < (e + 1) * DECAY_PIECES), 1.0, 0.0)
            qs[h] = jnp.concatenate([qs[h], jnp.broadcast_to(ones.astype(BF16), (tile, LANES))], axis=1)
        head0 = n_heads * hg
        fq = tuple(frow_ref[0, pl.ds(head0 + h, 1), pl.ds(pl.multiple_of(qi * tile, tile), tile)] * LOG2E
                   for h in range(n_heads))
    qts = [x.astype(F32).T.astype(BF16) for x in qs]

    sum_rows = _sum_rows(tile)

    def meta_keys():
        rows = slice(ROW0, PAD_FRONT)
        k = k_ref[0, rows, :]
        key = ROW0 + lax.broadcasted_iota(jnp.int32, (N_META, 1), 0)
        valid = key <= qcol
        ps = []
        for h in range(n_heads):
            g, e = divmod(h, 2)
            kk = _split_keys(k[:, g * width:(g + 1) * width], width)[e]
            if decay:
                kk = jnp.concatenate([kk, kf_ref[0, rows, g * LANES:(g + 1) * LANES]], axis=1)
            raw = jnp.where(valid, _dot(kk, qts[h]), MASK_VALUE)
            m_new = jnp.max(raw, axis=0, keepdims=True)
            if decay:
                m_new = m_new + fq[h]
                ps.append(jnp.exp2(raw - (m_new - fq[h])).astype(BF16))
            else:
                ps.append(jnp.exp2(raw - m_new).astype(BF16))
            m_scr[h] = m_new
        vt = vt_ref[:, rows]
        meta_sums = _sum_rows(N_META)
        for g in range(group):
            lhs = jnp.concatenate([_pair_values_t(vt[g * LANES:(g + 1) * LANES]), meta_sums], axis=0)
            acc_scr[g * ACC_ROWS:(g + 1) * ACC_ROWS, :] = _dot(lhs, jnp.concatenate(ps[2 * g:2 * g + 2], axis=0))

    def scores(t, s):
        start = pl.multiple_of(t * tile, tile)
        k = k_ref[0, pl.ds(start, tile), :]
        if decay:
            kf = kf_ref[0, pl.ds(start, tile), :]
        for g in range(group):
            ks = _split_keys(k[:, g * width:(g + 1) * width], width)
            if decay:
                ks = tuple(jnp.concatenate([kk, kf[:, g * LANES:(g + 1) * LANES]], axis=1) for kk in ks)
            for e in range(2):
                raw = _dot(ks[e], qts[2 * g + e])
                raw_s[s][2 * g + e] = raw
                mx_s[s][2 * g + e] = jnp.max(raw, axis=0, keepdims=True)

    def weights(t, s, masked):
        start = pl.multiple_of(t * tile, tile)
        if masked:
            key = start + lax.broadcasted_iota(jnp.int32, (tile, 1), 0)
            valid = key <= qcol
        for h in range(n_heads):
            raw = raw_s[s][h]
            m_old = m_scr[h]
            if masked:
                raw = jnp.where(valid, raw, MASK_VALUE)
                top = jnp.max(raw, axis=0, keepdims=True)
            else:
                top = mx_s[s][h]
            if decay:
                m_new = jnp.maximum(m_old, top + fq[h])
                p = jnp.exp2(raw - (m_new - fq[h]))
            else:
                m_new = jnp.maximum(m_old, top)
                p = jnp.exp2(raw - m_new)
            m_scr[h] = m_new
            p_s[s][h * tile:(h + 1) * tile, :] = p.astype(BF16)
            al_s[s][h] = jnp.exp2(m_old - m_new)

    def accumulate(t, s):
        vt = vt_ref[:, pl.ds(pl.multiple_of(t * tile, tile), tile)]
        for g in range(group):
            lhs = jnp.concatenate([_pair_values_t(vt[g * LANES:(g + 1) * LANES]), sum_rows], axis=0)
            pv = _dot(lhs, p_s[s][2 * g * tile:(2 * g + 2) * tile, :])
            rows = slice(g * ACC_ROWS, (g + 1) * ACC_ROWS)
            acc_scr[rows, :] = _pair_rows(al_s[s][2 * g], al_s[s][2 * g + 1]) * acc_scr[rows, :] + pv

    meta_keys()

    @pl.when(qi > 0)
    def _():
        p_s[0][...] = jnp.zeros_like(p_s[0])
        al_s[0][...] = jnp.ones_like(al_s[0])
        scores(1, 1)
        n_steady = qi - 1

        def pair(i, c):
            t = 2 * i + 1
            scores(t + 1, 0)
            weights(t, 1, False)
            accumulate(t - 1, 0)
            scores(t + 2, 1)
            weights(t + 1, 0, False)
            accumulate(t, 1)
            return c

        lax.fori_loop(0, n_steady // 2, pair, 0)

        @pl.when(n_steady % 2 == 1)
        def _():
            t = qi - 1
            scores(qi, 0)
            weights(t, 1, False)
            accumulate(t - 1, 0)
            weights(qi, 0, True)
            accumulate(t, 1)
            accumulate(qi, 0)

        @pl.when(n_steady % 2 == 0)
        def _():
            weights(qi, 1, True)
            accumulate(qi - 1, 0)
            accumulate(qi, 1)

    sub = lax.broadcasted_iota(jnp.int32, (LANES, 1), 0)
    for g in range(group):
        acc = acc_scr[g * ACC_ROWS:(g + 1) * ACC_ROWS, :]
        total = jnp.where(sub < HEAD_DIM, acc[LANES:LANES + 1], acc[LANES + 1:LANES + 2])
        o_ref[0, :, g * LANES:(g + 1) * LANES] = (acc[:LANES] / total).T.astype(o_ref.dtype)


def _softmax_attention(q, k, vt, *, width, n_pairs, offs=(0, 0), kf=None, frow=None):
    B, Lp, _ = q.shape
    qo, ko = offs
    tile = ATT_TILE
    group = ATT_GROUP
    assert n_pairs % group == 0 and qo % group == 0 and ko % group == 0
    n_heads = 2 * group
    decay = kf is not None
    kern = functools.partial(_softmax_attn_kernel, tile=tile, width=width, decay=decay, group=group)
    in_specs = [
        pl.BlockSpec((1, tile, group * width), lambda b, h, i: (b, i, qo // group + h)),
        pl.BlockSpec((1, Lp, group * width), lambda b, h, i: (b, 0, ko // group + h), pipeline_mode=pl.Buffered(1)),
        pl.BlockSpec((group * LANES, Lp), lambda b, h, i: (h, b)),
    ]
    args = [q, k, vt]
    if decay:
        in_specs += [
            pl.BlockSpec((1, Lp, group * LANES), lambda b, h, i: (b, 0, h), pipeline_mode=pl.Buffered(1)),
            pl.BlockSpec((1, N_HEADS, Lp), lambda b, h, i: (b, 0, 0)),
        ]
        args += [kf, frow]
    return pl.pallas_call(
        kern,
        grid=(B, n_pairs // group, Lp // tile),
        in_specs=in_specs,
        out_specs=pl.BlockSpec((1, tile, group * LANES), lambda b, h, i: (b, i, h)),
        out_shape=jax.ShapeDtypeStruct((B, Lp, n_pairs * LANES), BF16),
        scratch_shapes=[
            pltpu.VMEM((n_heads, 1, tile), F32),
            pltpu.VMEM((group * ACC_ROWS, tile), F32),
            pltpu.VMEM((n_heads, tile, tile), F32),
            pltpu.VMEM((n_heads, tile, tile), F32),
            pltpu.VMEM((n_heads * tile, tile), BF16),
            pltpu.VMEM((n_heads * tile, tile), BF16),
            pltpu.VMEM((n_heads, 1, tile), F32),
            pltpu.VMEM((n_heads, 1, tile), F32),
            pltpu.VMEM((n_heads, 1, tile), F32),
            pltpu.VMEM((n_heads, 1, tile), F32),
        ],
        compiler_params=_params("arbitrary", "arbitrary", "arbitrary"),
        name="fox_attention" if decay else "mla_attention",
    )(*args)


def _sb_attn_kernel(q_ref, k_ref, vt_ref, tri_ref, o_ref, r_scr, acc_scr, *, tile, group):
    qi = pl.program_id(2)
    n_heads = 2 * group
    first = ROW0 // tile
    q = q_ref[0]
    qs = []
    for g in range(group):
        qs += list(_split_heads(q[:, g * LANES:(g + 1) * LANES], LANES))
    qcol = qi * tile + lax.broadcasted_iota(jnp.int32, (1, tile), 1)
    r_scr[...] = jnp.zeros_like(r_scr)
    acc_scr[...] = jnp.zeros_like(acc_scr)
    tri2 = tri_ref[...]

    def walk(tiles):
        ks, vts, valids = [], [], []
        for t, masked in tiles:
            start = pl.multiple_of(t * tile, tile)
            ks.append(k_ref[0, pl.ds(start, tile), :])
            vts.append(vt_ref[:, pl.ds(start, tile)])
            key = start + lax.broadcasted_iota(jnp.int32, (tile, 1), 0)
            valids.append((key >= ROW0) & (key < qcol) if masked else None)
        n = len(tiles)
        zs = [[_dot_nt(ks[i][:, (h // 2) * LANES:(h // 2 + 1) * LANES], qs[h]) for h in range(n_heads)]
              for i in range(n)]
        lks = [[None] * n_heads for _ in range(n)]
        for i in range(n):
            for h in range(n_heads):
                lk = _log_sigmoid_neg(zs[i][h])
                lks[i][h] = lk if valids[i] is None else jnp.where(valids[i], lk, 0.0)
        laters = [[None] * n_heads for _ in range(n)]
        for i in range(n):
            for h in range(n_heads):
                lk_hi = lks[i][h].astype(BF16)
                lk_lo = (lks[i][h] - lk_hi.astype(F32)).astype(BF16)
                laters[i][h] = _dot(tri2, jnp.concatenate([lk_hi, lk_lo], axis=0))
        ws = [[None] * n_heads for _ in range(n)]
        for h in range(n_heads):
            r = r_scr[h]
            for i in range(n):
                w = jnp.exp(lks[i][h] + zs[i][h] + (laters[i][h] + r))
                ws[i][h] = (w if valids[i] is None else jnp.where(valids[i], w, 0.0)).astype(BF16)
                r = r + jnp.sum(lks[i][h], axis=0, keepdims=True)
            r_scr[h] = r
        for g in range(group):
            rows = slice(g * LANES, (g + 1) * LANES)
            lhs = jnp.concatenate([_pair_values_t(vts[i][rows]) for i in range(n)], axis=1)
            rhs = jnp.concatenate([ws[i][h] for i in range(n) for h in (2 * g, 2 * g + 1)], axis=0)
            acc_scr[rows, :] += _dot(lhs, rhs)

    def live():
        return (jnp.max(r_scr[...]) > DEAD_LOG_WEIGHT).astype(jnp.int32)

    lead = qi - (SB_LEAD - 1) > first

    @pl.when(lead)
    def _():
        walk(tuple((qi - i, i == 0) for i in range(SB_LEAD)))

    @pl.when(jnp.logical_not(lead))
    def _():
        walk(((qi, True),))

    def body(c):
        walk(((c[0], False),))
        return c[0] - 1, live()

    t0 = jnp.where(lead, qi - SB_LEAD, qi - 1)
    _, alive = lax.while_loop(lambda c: (c[0] > first) & (c[1] > 0), body, (t0, live()))

    @pl.when((qi > first) & (alive > 0))
    def _():
        walk(((first, True),))

    for g in range(group):
        rows = slice(g * LANES, (g + 1) * LANES)
        o_ref[0, :, rows] = acc_scr[rows, :].T.astype(o_ref.dtype)


def _sb_attention(qk, vt, *, n_pairs):
    B, Lp, _ = qk.shape
    tile, group = SB_TILE, SB_GROUP
    assert n_pairs % group == 0
    n_groups = n_pairs // group
    width = group * LANES
    r = lax.broadcasted_iota(jnp.int32, (tile, tile), 0)
    c = lax.broadcasted_iota(jnp.int32, (tile, tile), 1)
    tri = (c > r).astype(BF16)
    tri2 = jnp.concatenate([tri, tri], axis=1)
    kern = functools.partial(_sb_attn_kernel, tile=tile, group=group)
    return pl.pallas_call(
        kern,
        grid=(B, n_groups, Lp // tile),
        in_specs=[
            pl.BlockSpec((1, tile, width), lambda b, h, i: (b, i, h)),
            pl.BlockSpec((1, Lp, width), lambda b, h, i: (b, 0, n_groups + h), pipeline_mode=pl.Buffered(1)),
            pl.BlockSpec((width, Lp), lambda b, h, i: (h, b), pipeline_mode=pl.Buffered(1)),
            pl.BlockSpec((tile, 2 * tile), lambda b, h, i: (0, 0)),
        ],
        out_specs=pl.BlockSpec((1, tile, width), lambda b, h, i: (b, i, h)),
        out_shape=jax.ShapeDtypeStruct((B, Lp, n_pairs * LANES), BF16),
        scratch_shapes=[
            pltpu.VMEM((2 * group, 1, tile), F32),
            pltpu.VMEM((width, tile), F32),
        ],
        compiler_params=_params("arbitrary", "arbitrary", "arbitrary"),
        name="sb_attention",
    )(qk, qk, vt, tri2)


def _rope_table_kernel(inv_ref, o_ref, *, tile):
    row = pl.program_id(0) * tile + lax.broadcasted_iota(jnp.int32, (tile, 1), 0)
    ang = (row - ROW0).astype(F32) * inv_ref[...]
    lane = lax.broadcasted_iota(jnp.int32, (1, LANES), 1)
    half = MLA_ROPE // 2
    in_lo = (lane >= MLA_NOPE) & (lane < MLA_NOPE + half)
    in_hi = (lane >= MLA_NOPE + half) & (lane < MLA_NOPE + MLA_ROPE)
    cos = jnp.cos(ang)
    sin = jnp.sin(ang)
    o_ref[0] = jnp.where(lane < MLA_NOPE, 1.0, jnp.where(in_lo | in_hi, cos, 0.0))
    o_ref[1] = jnp.where(in_lo, -sin, 0.0)
    o_ref[2] = jnp.where(in_hi, sin, 0.0)


def _rope_tables(rows, inv_lane):
    tile = _divisor_tile(rows, 512)
    return pl.pallas_call(
        functools.partial(_rope_table_kernel, tile=tile),
        grid=(rows // tile,),
        in_specs=[pl.BlockSpec((1, LANES), lambda i: (0, 0))],
        out_specs=pl.BlockSpec((3, tile, LANES), lambda i: (0, i, 0)),
        out_shape=jax.ShapeDtypeStruct((3, rows, LANES), F32),
        compiler_params=_params("arbitrary"),
        name="rope_tables",
    )(inv_lane)


def _mla_proj_kernel(h_ref, g_ref, wd_ref, qn_ref, kvn_ref, wq_ref, wk_ref, wvt_ref, tab_ref,
                     q_ref, k_ref, vt_ref, *, q_scale):
    a = _rmsnorm(h_ref[...], g_ref[...]).astype(BF16)
    down = _dot(a, wd_ref[...])
    cq = _rmsnorm(down[:, :MLA_Q_RANK], qn_ref[...]).astype(BF16)
    ckv = _rmsnorm(down[:, MLA_Q_RANK:MLA_Q_RANK + MLA_KV_RANK], kvn_ref[...]).astype(BF16)
    kr = down[:, MLA_Q_RANK + MLA_KV_RANK:]
    c_tab, s_lo, s_hi = tab_ref[0], tab_ref[1], tab_ref[2]
    half = MLA_ROPE // 2

    def rope(x):
        return (x * c_tab + pltpu.roll(x, LANES - half, 1) * s_lo + pltpu.roll(x, half, 1) * s_hi)

    q = _dot(cq, wq_ref[...])
    kn = _dot(ckv, wk_ref[...])
    vt_ref[...] = _dot_nt(wvt_ref[...], ckv).astype(BF16)
    kr_rot = rope(kr)
    for hd in range(N_HEADS):
        cols = slice(hd * LANES, (hd + 1) * LANES)
        q_ref[:, cols] = (rope(q[:, cols]) * q_scale).astype(BF16)
        k_ref[:, cols] = (kn[:, cols] + kr_rot).astype(BF16)


def _mla_proj(h2, g, wd, qn, kvn, wq, wk, wvt, inv_lane, rows_per_batch):
    M, D = h2.shape
    tm = _divisor_tile(rows_per_batch, 512)
    tiles_per_batch = rows_per_batch // tm
    tables = _rope_tables(rows_per_batch, inv_lane)
    kern = functools.partial(_mla_proj_kernel, q_scale=(MLA_NOPE + MLA_ROPE) ** -0.5 * LOG2E)
    full = lambda a: pl.BlockSpec(a.shape, lambda i: (0,) * a.ndim)
    g2, qn2, kvn2 = g.reshape(1, D), qn.reshape(1, -1), kvn.reshape(1, -1)
    return pl.pallas_call(
        kern,
        grid=(M // tm,),
        in_specs=[pl.BlockSpec((tm, D), lambda i: (i, 0)), full(g2), full(wd), full(qn2), full(kvn2),
                  full(wq), full(wk), full(wvt),
                  pl.BlockSpec((3, tm, LANES), lambda i: (0, i % tiles_per_batch, 0))],
        out_specs=[
            pl.BlockSpec((tm, N_HEADS * LANES), lambda i: (i, 0)),
            pl.BlockSpec((tm, N_HEADS * LANES), lambda i: (i, 0)),
            pl.BlockSpec((N_HEADS * MLA_V, tm), lambda i: (0, i)),
        ],
        out_shape=[
            jax.ShapeDtypeStruct((M, N_HEADS * LANES), BF16),
            jax.ShapeDtypeStruct((M, N_HEADS * LANES), BF16),
            jax.ShapeDtypeStruct((N_HEADS * MLA_V, M), BF16),
        ],
        compiler_params=_params("arbitrary"),
        name="mla_proj",
    )(h2, g2, wd, qn2, kvn2, wq, wk, wvt, tables)


def _forget_cumsum_kernel(f_ref, b_ref, sel_ref, kf_ref, frow_ref, carry, *, tile):
    t = pl.program_id(1)

    @pl.when(t == 0)
    def _():
        carry[...] = jnp.zeros_like(carry)

    row_in = lax.broadcasted_iota(jnp.int32, (tile, 1), 0)
    row = t * tile + row_in
    x = jnp.where(row >= ROW0, _log_sigmoid_neg(-(f_ref[0] + b_ref[...])), 0.0)
    shift = 1
    while shift < tile:
        x = x + jnp.where(row_in >= shift, pltpu.roll(x, shift, 0), 0.0)
        shift *= 2
    x = x + carry[...]
    carry[...] = x[tile - 1:tile, :]
    frow_ref[0] = x.T[:N_HEADS, :]
    rest = -x * LOG2E
    kf = jnp.zeros(kf_ref.shape[1:], F32)
    for piece in range(DECAY_PIECES):
        part = rest.astype(BF16)
        rest = rest - part.astype(F32)
        kf = kf + _dot(part, sel_ref[piece])
    kf_ref[0] = kf.astype(BF16)


def _forget_cumsum(flog, b_lane):
    B, Lp, _ = flog.shape
    tile = _divisor_tile(Lp, ROW_TILE)
    n_pairs = N_HEADS // 2
    head = lax.broadcasted_iota(jnp.int32, (DECAY_PIECES, LANES, n_pairs * LANES), 1)
    col = lax.broadcasted_iota(jnp.int32, (DECAY_PIECES, LANES, n_pairs * LANES), 2)
    piece = lax.broadcasted_iota(jnp.int32, (DECAY_PIECES, LANES, n_pairs * LANES), 0)
    sel = ((head < N_HEADS) & (col == (head // 2) * LANES + (head % 2) * DECAY_PIECES + piece)).astype(BF16)
    kern = functools.partial(_forget_cumsum_kernel, tile=tile)
    return pl.pallas_call(
        kern,
        grid=(B, Lp // tile),
        in_specs=[
            pl.BlockSpec((1, tile, LANES), lambda b, t: (b, t, 0)),
            pl.BlockSpec((1, LANES), lambda b, t: (0, 0)),
            pl.BlockSpec(sel.shape, lambda b, t: (0, 0, 0)),
        ],
        out_specs=[
            pl.BlockSpec((1, tile, n_pairs * LANES), lambda b, t: (b, t, 0)),
            pl.BlockSpec((1, N_HEADS, tile), lambda b, t: (b, 0, t)),
        ],
        out_shape=[
            jax.ShapeDtypeStruct((B, Lp, n_pairs * LANES), BF16),
            jax.ShapeDtypeStruct((B, N_HEADS, Lp), F32),
        ],
        scratch_shapes=[pltpu.VMEM((1, LANES), F32)],
        compiler_params=_params("arbitrary", "arbitrary"),
        name="forget_cumsum",
    )(flog, b_lane, sel)


def _final_norm_kernel(h_ref, g_ref, o_ref):
    o_ref[0] = _rmsnorm(h_ref[0], g_ref[...])


def _final_norm(h, g, seq):
    B, Lp, D = h.shape
    tile = PAD_FRONT
    skip = PAD_FRONT // tile
    return pl.pallas_call(
        _final_norm_kernel,
        grid=(B, seq // tile),
        in_specs=[
            pl.BlockSpec((1, tile, D), lambda b, t: (b, t + skip, 0)),
            pl.BlockSpec((1, D), lambda b, t: (0, 0)),
        ],
        out_specs=pl.BlockSpec((1, tile, D), lambda b, t: (b, t, 0)),
        out_shape=jax.ShapeDtypeStruct((B, seq, D), F32),
        compiler_params=_params("arbitrary", "arbitrary"),
        name="final_norm",
    )(h, g.reshape(1, D))


def _pad_heads(w, per_head):
    K = w.shape[0]
    w3 = w.reshape(K, N_HEADS, per_head)
    w3 = jnp.pad(w3, ((0, 0), (0, 0), (0, LANES - per_head)))
    return w3.reshape(K, N_HEADS * LANES)


def _mla_layer(h, g, w_down, q_norm, kv_norm, w_uq, w_ukv, w_o):
    B, Lp, D = h.shape
    M = B * Lp
    n_lat = MLA_Q_RANK + MLA_KV_RANK
    wd_rope = jnp.pad(w_down[:, n_lat:], ((0, 0), (MLA_NOPE, LANES - MLA_NOPE - MLA_ROPE)))
    wd_p = jnp.concatenate([w_down[:, :n_lat], wd_rope], axis=1).astype(BF16)
    wq_p = _pad_heads(w_uq, MLA_NOPE + MLA_ROPE).astype(BF16)
    wkv3 = w_ukv.reshape(MLA_KV_RANK, N_HEADS, MLA_NOPE + MLA_V)
    wk_p = _pad_heads(wkv3[:, :, :MLA_NOPE].reshape(MLA_KV_RANK, -1), MLA_NOPE).astype(BF16)
    wvt_p = wkv3[:, :, MLA_NOPE:].reshape(MLA_KV_RANK, N_HEADS * MLA_V).T.astype(BF16)
    inv = ROPE_THETA ** (-jnp.arange(0, MLA_ROPE, 2, dtype=F32) / MLA_ROPE)
    inv_lane = jnp.concatenate([jnp.zeros((MLA_NOPE,), F32), inv, inv,
                                jnp.zeros((LANES - MLA_NOPE - MLA_ROPE,), F32)]).reshape(1, LANES)
    q, k, vt = _mla_proj(h.reshape(M, D), g, wd_p, q_norm, kv_norm, wq_p, wk_p, wvt_p, inv_lane, Lp)
    o = _softmax_attention(q.reshape(B, Lp, -1), k.reshape(B, Lp, -1), vt,
                           width=2 * LANES, n_pairs=N_HEADS // 2)
    return o.reshape(M, -1), w_o.astype(BF16)


def kernel(x, meta, norm_mix, norm_ffn, pool_w, pool_scale, sb_w_qkv, sb_w_o, mla_w_down, mla_q_norm,
           mla_kv_norm, mla_w_uq, mla_w_ukv, mla_w_o, fox_w_qkvf, fox_b_f, fox_w_o, ffn_w_gate, ffn_w_up,
           ffn_w_down, final_norm):
    B, S, D = x.shape
    assert D == D_MODEL and S % ATT_TILE == 0 and PAD_FRONT % ATT_TILE == 0 and PAD_FRONT % SB_TILE == 0
    Lp = S + PAD_FRONT
    M = B * Lp
    HD = N_HEADS * HEAD_DIM
    n_pairs = N_HEADS // 2

    def sb_mixer(h, i, j):
        qk, vt = _qk_vt_proj(h.reshape(M, D), norm_mix[i], sb_w_qkv[j][:, :2 * HD].astype(BF16),
                             sb_w_qkv[j][:, 2 * HD:].T.astype(BF16), q_scale=HEAD_DIM ** -0.5, q_cols=HD)
        o = _sb_attention(qk.reshape(B, Lp, 2 * HD), vt, n_pairs=n_pairs)
        return o.reshape(M, HD), sb_w_o[j].astype(BF16)

    def mla_mixer(h, i, j):
        return _mla_layer(h, norm_mix[i], mla_w_down[j], mla_q_norm[j], mla_kv_norm[j], mla_w_uq[j],
                          mla_w_ukv[j], mla_w_o[j])

    def fox_mixer(h, i, j):
        wf = jnp.pad(fox_w_qkvf[j][:, 3 * HD:], ((0, 0), (0, LANES - N_HEADS))).astype(BF16)
        qk, vt, flog = _qk_vt_proj(h.reshape(M, D), norm_mix[i], fox_w_qkvf[j][:, :2 * HD].astype(BF16),
                                   fox_w_qkvf[j][:, 2 * HD:3 * HD].T.astype(BF16), wf,
                                   q_scale=HEAD_DIM ** -0.5 * LOG2E, q_cols=HD)
        b_lane = jnp.pad(fox_b_f[j].astype(F32), (0, LANES - N_HEADS)).reshape(1, LANES)
        kf, frow = _forget_cumsum(flog.reshape(B, Lp, LANES), b_lane)
        qk = qk.reshape(B, Lp, 2 * HD)
        o = _softmax_attention(qk, qk, vt, width=LANES, n_pairs=n_pairs, offs=(0, n_pairs), kf=kf, frow=frow)
        return o.reshape(M, HD), fox_w_o[j].astype(BF16)

    mixers = (None, sb_mixer, mla_mixer, fox_mixer)
    h = None
    for i in range(norm_mix.shape[0]):
        m, j = i % len(mixers), i // len(mixers)
        ffn_w = (norm_ffn[i], ffn_w_gate[i].astype(BF16), ffn_w_up[i].astype(BF16), ffn_w_down[i].astype(BF16))
        if m == 0:
            assert i == 0, "the pooling mixer doubles as the layout stage and must come first"
            h = _pool_layer(x, meta, norm_mix[i], pool_w[j], pool_scale[j])
            h = _ffn(h.reshape(M, D), *ffn_w).reshape(B, Lp, D)
        else:
            o2, wo = mixers[m](h, i, j)
            h = _ffn(h.reshape(M, D), *ffn_w, o2=o2, wo=wo).reshape(B, Lp, D)
    return _final_norm(h, final_norm, S)
```

```python
import functools

import jax
import jax.numpy as jnp
from jax import lax
from jax.experimental import pallas as pl
from jax.experimental.pallas import tpu as pltpu

F32 = jnp.float32
BF16 = jnp.bfloat16

D_MODEL = 1024
N_META = 16
EPS = 1e-6
POOL_WINDOWS = (2, 4, 8, 16)
POOL_GROUP = D_MODEL // len(POOL_WINDOWS)
N_HEADS = 16
HEAD_DIM = 64
MLA_Q_RANK = 384
MLA_KV_RANK = 256
MLA_NOPE = 64
MLA_ROPE = 32
MLA_V = 64
ROPE_THETA = 10000.0

LANES = 128
PAD_FRONT = 256
ROW0 = PAD_FRONT - N_META
HALO = max(POOL_WINDOWS)
MASK_VALUE = -1e30
LOG2E = 1.4426950408889634
DECAY_PIECES = 3
ACC_ROWS = LANES + 16
DEAD_LOG_WEIGHT = -104.0
VMEM_LIMIT = 56 * 1024 * 1024

ROW_TILE = 1024
ATT_TILE = 256
ATT_GROUP = 4
SB_TILE = 128
SB_LEAD = 3
SB_GROUP = 8
PROJ_TILE = 512
FF_TILE = 256


def _divisor_tile(n, target):
    best = None
    for t in range(8, min(n, target) + 1, 8):
        if n % t == 0:
            best = t
    assert best is not None, (n, target)
    return best


def _params(*sem):
    return pltpu.CompilerParams(dimension_semantics=sem, vmem_limit_bytes=VMEM_LIMIT)


def _rmsnorm(x, g):
    return x * lax.rsqrt(jnp.mean(x * x, axis=-1, keepdims=True) + EPS) * g


def _log_sigmoid_neg(z):
    return -(jnp.maximum(z, 0.0) + jnp.log(1.0 + jnp.exp(-jnp.abs(z))))


def _dot(a, b):
    return jnp.dot(a, b, preferred_element_type=F32)


def _dot_nt(a, b):
    return lax.dot_general(a, b, (((1,), (1,)), ((), ())), preferred_element_type=F32)


def _pool_kernel(x_ref, meta_ref, g_ref, w_ref, sc_ref, o_ref, abuf, *, tile):
    t = pl.program_id(1)
    front = jnp.concatenate([jnp.zeros((ROW0, D_MODEL), F32), meta_ref[...]], axis=0)
    h = jnp.where(t == 0, front, x_ref[0])
    row = t * tile + lax.broadcasted_iota(jnp.int32, (tile, 1), 0)
    a = jnp.where(row >= ROW0, _rmsnorm(h, g_ref[...]), 0.0)

    @pl.when(t == 0)
    def _():
        abuf[0:HALO, :] = jnp.zeros((HALO, D_MODEL), F32)

    @pl.when(t > 0)
    def _():
        abuf[0:HALO, :] = abuf[tile:tile + HALO, :]

    abuf[HALO:HALO + tile, :] = a
    pos1 = row - ROW0 + 1
    outs = []
    for g, win in enumerate(POOL_WINDOWS):
        cols = slice(g * POOL_GROUP, (g + 1) * POOL_GROUP)
        acc = abuf[:, cols]
        shift = 1
        while shift < win:
            acc = acc + pltpu.roll(acc, shift, 0)
            shift *= 2
        xg = a[:, cols]
        cnt = jnp.clip(pos1, 1, win).astype(F32)
        pooled = acc[HALO:] / cnt - xg
        outs.append(_dot(pooled.astype(BF16), w_ref[g]))
    mixed = jnp.concatenate(outs, axis=1) * sc_ref[...]
    o_ref[0] = h + mixed


def _pool_layer(x, meta, g, w, sc):
    B, S, D = x.shape
    tile = PAD_FRONT
    Lp = S + PAD_FRONT
    kern = functools.partial(_pool_kernel, tile=tile)
    return pl.pallas_call(
        kern,
        grid=(B, Lp // tile),
        in_specs=[
            pl.BlockSpec((1, tile, D), lambda b, t: (b, jnp.maximum(t - 1, 0), 0)),
            pl.BlockSpec((N_META, D), lambda b, t: (0, 0)),
            pl.BlockSpec((1, D), lambda b, t: (0, 0)),
            pl.BlockSpec((len(POOL_WINDOWS), POOL_GROUP, POOL_GROUP), lambda b, t: (0, 0, 0)),
            pl.BlockSpec((1, D), lambda b, t: (0, 0)),
        ],
        out_specs=pl.BlockSpec((1, tile, D), lambda b, t: (b, t, 0)),
        out_shape=jax.ShapeDtypeStruct((B, Lp, D), F32),
        scratch_shapes=[pltpu.VMEM((tile + HALO, D), F32)],
        compiler_params=_params("arbitrary", "arbitrary"),
        name="pool_layer",
    )(x, meta.astype(x.dtype), g.reshape(1, D), w.astype(BF16), sc.reshape(1, D))


def _qk_vt_proj_kernel(*refs, q_scale, q_cols, tn, has_gate):
    if has_gate:
        h_ref, g_ref, w_ref, wvt_ref, wf_ref, o_ref, vt_ref, f_ref = refs
    else:
        h_ref, g_ref, w_ref, wvt_ref, o_ref, vt_ref = refs
    a = _rmsnorm(h_ref[...], g_ref[...]).astype(BF16)
    for c in range(w_ref.shape[1] // tn):
        cols = slice(c * tn, (c + 1) * tn)
        part = _dot(a, w_ref[:, cols])
        if (c + 1) * tn <= q_cols:
            part = part * q_scale
        o_ref[:, cols] = part.astype(o_ref.dtype)
    vt_ref[...] = _dot_nt(wvt_ref[...], a).astype(vt_ref.dtype)
    if has_gate:
        f_ref[...] = _dot(a, wf_ref[...])


def _qk_vt_proj(h2, g, w_qk, w_vt, wf=None, *, q_scale, q_cols):
    M, D = h2.shape
    N = w_qk.shape[1]
    Dv = w_vt.shape[0]
    tn = PROJ_TILE
    assert N % tn == 0 and q_cols % tn == 0
    tm = _divisor_tile(M, ROW_TILE)
    has_gate = wf is not None
    resident = lambda shape: pl.BlockSpec(shape, lambda i: (0, 0), pipeline_mode=pl.Buffered(1))
    in_specs = [pl.BlockSpec((tm, D), lambda i: (i, 0)), resident((1, D)), resident((D, N)), resident((Dv, D))]
    out_specs = [pl.BlockSpec((tm, N), lambda i: (i, 0)), pl.BlockSpec((Dv, tm), lambda i: (0, i))]
    out_shape = [jax.ShapeDtypeStruct((M, N), BF16), jax.ShapeDtypeStruct((Dv, M), BF16)]
    args = [h2, g.reshape(1, D), w_qk, w_vt]
    if has_gate:
        in_specs.append(resident((D, LANES)))
        out_specs.append(pl.BlockSpec((tm, LANES), lambda i: (i, 0)))
        out_shape.append(jax.ShapeDtypeStruct((M, LANES), F32))
        args.append(wf)
    return pl.pallas_call(
        functools.partial(_qk_vt_proj_kernel, q_scale=q_scale, q_cols=q_cols, tn=tn, has_gate=has_gate),
        grid=(M // tm,),
        in_specs=in_specs,
        out_specs=out_specs,
        out_shape=out_shape,
        compiler_params=_params("arbitrary"),
        name="qk_vt_proj",
    )(*args)


def _ffn_kernel(*refs, tf, has_mix):
    if has_mix:
        h_ref, o_ref, wo_ref, g_ref, wg_ref, wu_ref, wd_ref, out_ref, acc_scr = refs
        h = h_ref[...] + _dot(o_ref[...], wo_ref[...])
    else:
        h_ref, g_ref, wg_ref, wu_ref, wd_ref, out_ref, acc_scr = refs
        h = h_ref[...]
    a = _rmsnorm(h, g_ref[...]).astype(BF16)
    for c in range(wg_ref.shape[1] // tf):
        cols = slice(c * tf, (c + 1) * tf)
        gate = _dot(a, wg_ref[:, cols])
        up = _dot(a, wu_ref[:, cols])
        act = (gate * jax.nn.sigmoid(gate) * up).astype(BF16)
        part = _dot(act, wd_ref[cols, :])
        if c == 0:
            acc_scr[...] = part
        else:
            acc_scr[...] += part
    out_ref[...] = h + acc_scr[...]


def _ffn(h2, g, wg, wu, wd, o2=None, wo=None):
    M, D = h2.shape
    F = wg.shape[1]
    assert F % FF_TILE == 0
    tm = _divisor_tile(M, ROW_TILE)
    has_mix = o2 is not None
    resident = lambda shape: pl.BlockSpec(shape, lambda i: (0, 0), pipeline_mode=pl.Buffered(1))
    rows = lambda width: pl.BlockSpec((tm, width), lambda i: (i, 0))
    in_specs, args = [rows(D)], [h2]
    if has_mix:
        in_specs += [rows(o2.shape[1]), resident(wo.shape)]
        args += [o2, wo]
    in_specs += [resident((1, D)), resident((D, F)), resident((D, F)), resident((F, D))]
    args += [g.reshape(1, D), wg, wu, wd]
    return pl.pallas_call(
        functools.partial(_ffn_kernel, tf=FF_TILE, has_mix=has_mix),
        grid=(M // tm,),
        in_specs=in_specs,
        out_specs=rows(D),
        out_shape=jax.ShapeDtypeStruct((M, D), F32),
        scratch_shapes=[pltpu.VMEM((tm, D), F32)],
        compiler_params=_params("arbitrary"),
        name="ffn",
    )(*args)


def _split_heads(x, width):
    if width == 2 * LANES:
        return x[:, :LANES], x[:, LANES:]
    lane = lax.broadcasted_iota(jnp.int32, (1, LANES), 1)
    zero = jnp.zeros_like(x)
    return jnp.where(lane < HEAD_DIM, x, zero), jnp.where(lane >= HEAD_DIM, x, zero)


def _split_keys(x, width):
    if width == 2 * LANES:
        return x[:, :LANES], x[:, LANES:]
    return x, x


def _pair_values_t(vt):
    sub = lax.broadcasted_iota(jnp.int32, (LANES, 1), 0)
    zero = jnp.zeros_like(vt)
    return jnp.concatenate(
        [jnp.where(sub < HEAD_DIM, vt, zero), jnp.where(sub >= HEAD_DIM, vt, zero)], axis=1)


def _pair_rows(x0, x1):
    sub = lax.broadcasted_iota(jnp.int32, (ACC_ROWS, 1), 0)
    return jnp.where((sub < HEAD_DIM) | (sub == LANES), x0, x1)


def _sum_rows(tile):
    sub = lax.broadcasted_iota(jnp.int32, (ACC_ROWS - LANES, 2 * tile), 0)
    col = lax.broadcasted_iota(jnp.int32, (ACC_ROWS - LANES, 2 * tile), 1)
    return jnp.where(sub == col // tile, 1.0, 0.0).astype(BF16)


def _softmax_attn_kernel(*refs, tile, width, decay, group):
    if decay:
        q_ref, k_ref, vt_ref, kf_ref, frow_ref, o_ref = refs[:6]
    else:
        q_ref, k_ref, vt_ref, o_ref = refs[:4]
    m_scr, acc_scr, raw0, raw1, p0, p1, al0, al1, mx0, mx1 = refs[-10:]
    raw_s, p_s, al_s, mx_s = (raw0, raw1), (p0, p1), (al0, al1), (mx0, mx1)
    hg = pl.program_id(1)
    qi = pl.program_id(2)
    n_heads = 2 * group
    q = q_ref[0]
    qs = []
    for g in range(group):
        qs += list(_split_heads(q[:, g * width:(g + 1) * width], width))
    qcol = qi * tile + lax.broadcasted_iota(jnp.int32, (1, tile), 1)
    if decay:
        lane = lax.broadcasted_iota(jnp.int32, (1, LANES), 1)
        for h in range(n_heads):
            e = h % 2
            ones = jnp.where((lane >= e * DECAY_PIECES) & (lane < (e + 1) * DECAY_PIECES), 1.0, 0.0)
            qs[h] = jnp.concatenate([qs[h], jnp.broadcast_to(ones.astype(BF16), (tile, LANES))], axis=1)
        head0 = n_heads * hg
        fq = tuple(frow_ref[0, pl.ds(head0 + h, 1), pl.ds(pl.multiple_of(qi * tile, tile), tile)] * LOG2E
                   for h in range(n_heads))
    qts = [x.astype(F32).T.astype(BF16) for x in qs]

    sum_rows = _sum_rows(tile)

    def meta_keys():
        rows = slice(ROW0, PAD_FRONT)
        k = k_ref[0, rows, :]
        key = ROW0 + lax.broadcasted_iota(jnp.int32, (N_META, 1), 0)
        valid = key <= qcol
        ps = []
        for h in range(n_heads):
            g, e = divmod(h, 2)
            kk = _split_keys(k[:, g * width:(g + 1) * width], width)[e]
            if decay:
                kk = jnp.concatenate([kk, kf_ref[0, rows, g * LANES:(g + 1) * LANES]], axis=1)
            raw = jnp.where(valid, _dot(kk, qts[h]), MASK_VALUE)
            m_new = jnp.max(raw, axis=0, keepdims=True)
            if decay:
                m_new = m_new + fq[h]
                ps.append(jnp.exp2(raw - (m_new - fq[h])).astype(BF16))
            else:
                ps.append(jnp.exp2(raw - m_new).astype(BF16))
            m_scr[h] = m_new
        vt = vt_ref[:, rows]
        meta_sums = _sum_rows(N_META)
        for g in range(group):
            lhs = jnp.concatenate([_pair_values_t(vt[g * LANES:(g + 1) * LANES]), meta_sums], axis=0)
            acc_scr[g * ACC_ROWS:(g + 1) * ACC_ROWS, :] = _dot(lhs, jnp.concatenate(ps[2 * g:2 * g + 2], axis=0))

    def scores(t, s):
        start = pl.multiple_of(t * tile, tile)
        k = k_ref[0, pl.ds(start, tile), :]
        if decay:
            kf = kf_ref[0, pl.ds(start, tile), :]
        for g in range(group):
            ks = _split_keys(k[:, g * width:(g + 1) * width], width)
            if decay:
                ks = tuple(jnp.concatenate([kk, kf[:, g * LANES:(g + 1) * LANES]], axis=1) for kk in ks)
            for e in range(2):
                raw = _dot(ks[e], qts[2 * g + e])
                raw_s[s][2 * g + e] = raw
                mx_s[s][2 * g + e] = jnp.max(raw, axis=0, keepdims=True)

    def weights(t, s, masked):
        start = pl.multiple_of(t * tile, tile)
        if masked:
            key = start + lax.broadcasted_iota(jnp.int32, (tile, 1), 0)
            valid = key <= qcol
        for h in range(n_heads):
            raw = raw_s[s][h]
            m_old = m_scr[h]
            if masked:
                raw = jnp.where(valid, raw, MASK_VALUE)
                top = jnp.max(raw, axis=0, keepdims=True)
            else:
                top = mx_s[s][h]
            if decay:
                m_new = jnp.maximum(m_old, top + fq[h])
                p = jnp.exp2(raw - (m_new - fq[h]))
            else:
                m_new = jnp.maximum(m_old, top)
                p = jnp.exp2(raw - m_new)
            m_scr[h] = m_new
            p_s[s][h * tile:(h + 1) * tile, :] = p.astype(BF16)
            al_s[s][h] = jnp.exp2(m_old - m_new)

    def accumulate(t, s):
        vt = vt_ref[:, pl.ds(pl.multiple_of(t * tile, tile), tile)]
        for g in range(group):
            lhs = jnp.concatenate([_pair_values_t(vt[g * LANES:(g + 1) * LANES]), sum_rows], axis=0)
            pv = _dot(lhs, p_s[s][2 * g * tile:(2 * g + 2) * tile, :])
            rows = slice(g * ACC_ROWS, (g + 1) * ACC_ROWS)
            acc_scr[rows, :] = _pair_rows(al_s[s][2 * g], al_s[s][2 * g + 1]) * acc_scr[rows, :] + pv

    meta_keys()

    @pl.when(qi > 0)
    def _():
        p_s[0][...] = jnp.zeros_like(p_s[0])
        al_s[0][...] = jnp.ones_like(al_s[0])
        scores(1, 1)
        n_steady = qi - 1

        def pair(i, c):
            t = 2 * i + 1
            scores(t + 1, 0)
            weights(t, 1, False)
            accumulate(t - 1, 0)
            scores(t + 2, 1)
            weights(t + 1, 0, False)
            accumulate(t, 1)
            return c

        lax.fori_loop(0, n_steady // 2, pair, 0)

        @pl.when(n_steady % 2 == 1)
        def _():
            t = qi - 1
            scores(qi, 0)
            weights(t, 1, False)
            accumulate(t - 1, 0)
            weights(qi, 0, True)
            accumulate(t, 1)
            accumulate(qi, 0)

        @pl.when(n_steady % 2 == 0)
        def _():
            weights(qi, 1, True)
            accumulate(qi - 1, 0)
            accumulate(qi, 1)

    sub = lax.broadcasted_iota(jnp.int32, (LANES, 1), 0)
    for g in range(group):
        acc = acc_scr[g * ACC_ROWS:(g + 1) * ACC_ROWS, :]
        total = jnp.where(sub < HEAD_DIM, acc[LANES:LANES + 1], acc[LANES + 1:LANES + 2])
        o_ref[0, :, g * LANES:(g + 1) * LANES] = (acc[:LANES] / total).T.astype(o_ref.dtype)


def _softmax_attention(q, k, vt, *, width, n_pairs, offs=(0, 0), kf=None, frow=None):
    B, Lp, _ = q.shape
    qo, ko = offs
    tile = ATT_TILE
    group = ATT_GROUP
    assert n_pairs % group == 0 and qo % group == 0 and ko % group == 0
    n_heads = 2 * group
    decay = kf is not None
    kern = functools.partial(_softmax_attn_kernel, tile=tile, width=width, decay=decay, group=group)
    in_specs = [
        pl.BlockSpec((1, tile, group * width), lambda b, h, i: (b, i, qo // group + h)),
        pl.BlockSpec((1, Lp, group * width), lambda b, h, i: (b, 0, ko // group + h), pipeline_mode=pl.Buffered(1)),
        pl.BlockSpec((group * LANES, Lp), lambda b, h, i: (h, b)),
    ]
    args = [q, k, vt]
    if decay:
        in_specs += [
            pl.BlockSpec((1, Lp, group * LANES), lambda b, h, i: (b, 0, h), pipeline_mode=pl.Buffered(1)),
            pl.BlockSpec((1, N_HEADS, Lp), lambda b, h, i: (b, 0, 0)),
        ]
        args += [kf, frow]
    return pl.pallas_call(
        kern,
        grid=(B, n_pairs // group, Lp // tile),
        in_specs=in_specs,
        out_specs=pl.BlockSpec((1, tile, group * LANES), lambda b, h, i: (b, i, h)),
        out_shape=jax.ShapeDtypeStruct((B, Lp, n_pairs * LANES), BF16),
        scratch_shapes=[
            pltpu.VMEM((n_heads, 1, tile), F32),
            pltpu.VMEM((group * ACC_ROWS, tile), F32),
            pltpu.VMEM((n_heads, tile, tile), F32),
            pltpu.VMEM((n_heads, tile, tile), F32),
            pltpu.VMEM((n_heads * tile, tile), BF16),
            pltpu.VMEM((n_heads * tile, tile), BF16),
            pltpu.VMEM((n_heads, 1, tile), F32),
            pltpu.VMEM((n_heads, 1, tile), F32),
            pltpu.VMEM((n_heads, 1, tile), F32),
            pltpu.VMEM((n_heads, 1, tile), F32),
        ],
        compiler_params=_params("arbitrary", "arbitrary", "arbitrary"),
        name="fox_attention" if decay else "mla_attention",
    )(*args)


def _sb_attn_kernel(q_ref, k_ref, vt_ref, tri_ref, o_ref, r_scr, acc_scr, *, tile, group):
    qi = pl.program_id(2)
    n_heads = 2 * group
    first = ROW0 // tile
    q = q_ref[0]
    qs = []
    for g in range(group):
        qs += list(_split_heads(q[:, g * LANES:(g + 1) * LANES], LANES))
    qts = [x.astype(F32).T.astype(BF16) for x in qs]
    qcol = qi * tile + lax.broadcasted_iota(jnp.int32, (1, tile), 1)
    r_scr[...] = jnp.zeros_like(r_scr)
    acc_scr[...] = jnp.zeros_like(acc_scr)
    tri2 = tri_ref[...]

    def walk(tiles):
        ks, vts, valids = [], [], []
        for t, masked in tiles:
            start = pl.multiple_of(t * tile, tile)
            ks.append(k_ref[0, pl.ds(start, tile), :])
            vts.append(vt_ref[:, pl.ds(start, tile)])
            key = start + lax.broadcasted_iota(jnp.int32, (tile, 1), 0)
            valids.append((key >= ROW0) & (key < qcol) if masked else None)
        n = len(tiles)
        zs = [[_dot(ks[i][:, (h // 2) * LANES:(h // 2 + 1) * LANES], qts[h]) for h in range(n_heads)]
              for i in range(n)]
        lks = [[None] * n_heads for _ in range(n)]
        for i in range(n):
            for h in range(n_heads):
                lk = _log_sigmoid_neg(zs[i][h])
                lks[i][h] = lk if valids[i] is None else jnp.where(valids[i], lk, 0.0)
        laters = [[None] * n_heads for _ in range(n)]
        for i in range(n):
            for h in range(n_heads):
                lk_hi = lks[i][h].astype(BF16)
                lk_lo = (lks[i][h] - lk_hi.astype(F32)).astype(BF16)
                laters[i][h] = _dot(tri2, jnp.concatenate([lk_hi, lk_lo], axis=0))
        ws = [[None] * n_heads for _ in range(n)]
        for h in range(n_heads):
            r = r_scr[h]
            for i in range(n):
                w = jnp.exp(lks[i][h] + zs[i][h] + (laters[i][h] + r))
                ws[i][h] = (w if valids[i] is None else jnp.where(valids[i], w, 0.0)).astype(BF16)
                r = r + jnp.sum(lks[i][h], axis=0, keepdims=True)
            r_scr[h] = r
        for g in range(group):
            rows = slice(g * LANES, (g + 1) * LANES)
            lhs = jnp.concatenate([_pair_values_t(vts[i][rows]) for i in range(n)], axis=1)
            rhs = jnp.concatenate([ws[i][h] for i in range(n) for h in (2 * g, 2 * g + 1)], axis=0)
            acc_scr[rows, :] += _dot(lhs, rhs)

    def live():
        return (jnp.max(r_scr[...]) > DEAD_LOG_WEIGHT).astype(jnp.int32)

    lead = qi - (SB_LEAD - 1) > first

    @pl.when(lead)
    def _():
        walk(tuple((qi - i, i == 0) for i in range(SB_LEAD)))

    @pl.when(jnp.logical_not(lead))
    def _():
        walk(((qi, True),))

    def body(c):
        walk(((c[0], False),))
        return c[0] - 1, live()

    t0 = jnp.where(lead, qi - SB_LEAD, qi - 1)
    _, alive = lax.while_loop(lambda c: (c[0] > first) & (c[1] > 0), body, (t0, live()))

    @pl.when((qi > first) & (alive > 0))
    def _():
        walk(((first, True),))

    for g in range(group):
        rows = slice(g * LANES, (g + 1) * LANES)
        o_ref[0, :, rows] = acc_scr[rows, :].T.astype(o_ref.dtype)


def _sb_attention(qk, vt, *, n_pairs):
    B, Lp, _ = qk.shape
    tile, group = SB_TILE, SB_GROUP
    assert n_pairs % group == 0
    n_groups = n_pairs // group
    width = group * LANES
    r = lax.broadcasted_iota(jnp.int32, (tile, tile), 0)
    c = lax.broadcasted_iota(jnp.int32, (tile, tile), 1)
    tri = (c > r).astype(BF16)
    tri2 = jnp.concatenate([tri, tri], axis=1)
    kern = functools.partial(_sb_attn_kernel, tile=tile, group=group)
    return pl.pallas_call(
        kern,
        grid=(B, n_groups, Lp // tile),
        in_specs=[
            pl.BlockSpec((1, tile, width), lambda b, h, i: (b, i, h)),
            pl.BlockSpec((1, Lp, width), lambda b, h, i: (b, 0, n_groups + h), pipeline_mode=pl.Buffered(1)),
            pl.BlockSpec((width, Lp), lambda b, h, i: (h, b), pipeline_mode=pl.Buffered(1)),
            pl.BlockSpec((tile, 2 * tile), lambda b, h, i: (0, 0)),
        ],
        out_specs=pl.BlockSpec((1, tile, width), lambda b, h, i: (b, i, h)),
        out_shape=jax.ShapeDtypeStruct((B, Lp, n_pairs * LANES), BF16),
        scratch_shapes=[
            pltpu.VMEM((2 * group, 1, tile), F32),
            pltpu.VMEM((width, tile), F32),
        ],
        compiler_params=_params("arbitrary", "arbitrary", "arbitrary"),
        name="sb_attention",
    )(qk, qk, vt, tri2)


def _rope_table_kernel(inv_ref, o_ref, *, tile):
    row = pl.program_id(0) * tile + lax.broadcasted_iota(jnp.int32, (tile, 1), 0)
    ang = (row - ROW0).astype(F32) * inv_ref[...]
    lane = lax.broadcasted_iota(jnp.int32, (1, LANES), 1)
    half = MLA_ROPE // 2
    in_lo = (lane >= MLA_NOPE) & (lane < MLA_NOPE + half)
    in_hi = (lane >= MLA_NOPE + half) & (lane < MLA_NOPE + MLA_ROPE)
    cos = jnp.cos(ang)
    sin = jnp.sin(ang)
    o_ref[0] = jnp.where(lane < MLA_NOPE, 1.0, jnp.where(in_lo | in_hi, cos, 0.0))
    o_ref[1] = jnp.where(in_lo, -sin, 0.0)
    o_ref[2] = jnp.where(in_hi, sin, 0.0)


def _rope_tables(rows, inv_lane):
    tile = _divisor_tile(rows, 512)
    return pl.pallas_call(
        functools.partial(_rope_table_kernel, tile=tile),
        grid=(rows // tile,),
        in_specs=[pl.BlockSpec((1, LANES), lambda i: (0, 0))],
        out_specs=pl.BlockSpec((3, tile, LANES), lambda i: (0, i, 0)),
        out_shape=jax.ShapeDtypeStruct((3, rows, LANES), F32),
        compiler_params=_params("arbitrary"),
        name="rope_tables",
    )(inv_lane)


def _mla_proj_kernel(h_ref, g_ref, wd_ref, qn_ref, kvn_ref, wq_ref, wk_ref, wvt_ref, tab_ref,
                     q_ref, k_ref, vt_ref, *, q_scale):
    a = _rmsnorm(h_ref[...], g_ref[...]).astype(BF16)
    down = _dot(a, wd_ref[...])
    cq = _rmsnorm(down[:, :MLA_Q_RANK], qn_ref[...]).astype(BF16)
    ckv = _rmsnorm(down[:, MLA_Q_RANK:MLA_Q_RANK + MLA_KV_RANK], kvn_ref[...]).astype(BF16)
    kr = down[:, MLA_Q_RANK + MLA_KV_RANK:]
    c_tab, s_lo, s_hi = tab_ref[0], tab_ref[1], tab_ref[2]
    half = MLA_ROPE // 2

    def rope(x):
        return (x * c_tab + pltpu.roll(x, LANES - half, 1) * s_lo + pltpu.roll(x, half, 1) * s_hi)

    q = _dot(cq, wq_ref[...])
    kn = _dot(ckv, wk_ref[...])
    vt_ref[...] = _dot_nt(wvt_ref[...], ckv).astype(BF16)
    kr_rot = rope(kr)
    for hd in range(N_HEADS):
        cols = slice(hd * LANES, (hd + 1) * LANES)
        q_ref[:, cols] = (rope(q[:, cols]) * q_scale).astype(BF16)
        k_ref[:, cols] = (kn[:, cols] + kr_rot).astype(BF16)


def _mla_proj(h2, g, wd, qn, kvn, wq, wk, wvt, inv_lane, rows_per_batch):
    M, D = h2.shape
    tm = _divisor_tile(rows_per_batch, 512)
    tiles_per_batch = rows_per_batch // tm
    tables = _rope_tables(rows_per_batch, inv_lane)
    kern = functools.partial(_mla_proj_kernel, q_scale=(MLA_NOPE + MLA_ROPE) ** -0.5 * LOG2E)
    full = lambda a: pl.BlockSpec(a.shape, lambda i: (0,) * a.ndim)
    g2, qn2, kvn2 = g.reshape(1, D), qn.reshape(1, -1), kvn.reshape(1, -1)
    return pl.pallas_call(
        kern,
        grid=(M // tm,),
        in_specs=[pl.BlockSpec((tm, D), lambda i: (i, 0)), full(g2), full(wd), full(qn2), full(kvn2),
                  full(wq), full(wk), full(wvt),
                  pl.BlockSpec((3, tm, LANES), lambda i: (0, i % tiles_per_batch, 0))],
        out_specs=[
            pl.BlockSpec((tm, N_HEADS * LANES), lambda i: (i, 0)),
            pl.BlockSpec((tm, N_HEADS * LANES), lambda i: (i, 0)),
            pl.BlockSpec((N_HEADS * MLA_V, tm), lambda i: (0, i)),
        ],
        out_shape=[
            jax.ShapeDtypeStruct((M, N_HEADS * LANES), BF16),
            jax.ShapeDtypeStruct((M, N_HEADS * LANES), BF16),
            jax.ShapeDtypeStruct((N_HEADS * MLA_V, M), BF16),
        ],
        compiler_params=_params("arbitrary"),
        name="mla_proj",
    )(h2, g2, wd, qn2, kvn2, wq, wk, wvt, tables)


def _forget_cumsum_kernel(f_ref, b_ref, sel_ref, kf_ref, frow_ref, carry, *, tile):
    t = pl.program_id(1)

    @pl.when(t == 0)
    def _():
        carry[...] = jnp.zeros_like(carry)

    row_in = lax.broadcasted_iota(jnp.int32, (tile, 1), 0)
    row = t * tile + row_in
    x = jnp.where(row >= ROW0, _log_sigmoid_neg(-(f_ref[0] + b_ref[...])), 0.0)
    shift = 1
    while shift < tile:
        x = x + jnp.where(row_in >= shift, pltpu.roll(x, shift, 0), 0.0)
        shift *= 2
    x = x + carry[...]
    carry[...] = x[tile - 1:tile, :]
    frow_ref[0] = x.T[:N_HEADS, :]
    rest = -x * LOG2E
    kf = jnp.zeros(kf_ref.shape[1:], F32)
    for piece in range(DECAY_PIECES):
        part = rest.astype(BF16)
        rest = rest - part.astype(F32)
        kf = kf + _dot(part, sel_ref[piece])
    kf_ref[0] = kf.astype(BF16)


def _forget_cumsum(flog, b_lane):
    B, Lp, _ = flog.shape
    tile = _divisor_tile(Lp, ROW_TILE)
    n_pairs = N_HEADS // 2
    head = lax.broadcasted_iota(jnp.int32, (DECAY_PIECES, LANES, n_pairs * LANES), 1)
    col = lax.broadcasted_iota(jnp.int32, (DECAY_PIECES, LANES, n_pairs * LANES), 2)
    piece = lax.broadcasted_iota(jnp.int32, (DECAY_PIECES, LANES, n_pairs * LANES), 0)
    sel = ((head < N_HEADS) & (col == (head // 2) * LANES + (head % 2) * DECAY_PIECES + piece)).astype(BF16)
    kern = functools.partial(_forget_cumsum_kernel, tile=tile)
    return pl.pallas_call(
        kern,
        grid=(B, Lp // tile),
        in_specs=[
            pl.BlockSpec((1, tile, LANES), lambda b, t: (b, t, 0)),
            pl.BlockSpec((1, LANES), lambda b, t: (0, 0)),
            pl.BlockSpec(sel.shape, lambda b, t: (0, 0, 0)),
        ],
        out_specs=[
            pl.BlockSpec((1, tile, n_pairs * LANES), lambda b, t: (b, t, 0)),
            pl.BlockSpec((1, N_HEADS, tile), lambda b, t: (b, 0, t)),
        ],
        out_shape=[
            jax.ShapeDtypeStruct((B, Lp, n_pairs * LANES), BF16),
            jax.ShapeDtypeStruct((B, N_HEADS, Lp), F32),
        ],
        scratch_shapes=[pltpu.VMEM((1, LANES), F32)],
        compiler_params=_params("arbitrary", "arbitrary"),
        name="forget_cumsum",
    )(flog, b_lane, sel)


def _final_norm_kernel(h_ref, g_ref, o_ref):
    o_ref[0] = _rmsnorm(h_ref[0], g_ref[...])


def _final_norm(h, g, seq):
    B, Lp, D = h.shape
    tile = PAD_FRONT
    skip = PAD_FRONT // tile
    return pl.pallas_call(
        _final_norm_kernel,
        grid=(B, seq // tile),
        in_specs=[
            pl.BlockSpec((1, tile, D), lambda b, t: (b, t + skip, 0)),
            pl.BlockSpec((1, D), lambda b, t: (0, 0)),
        ],
        out_specs=pl.BlockSpec((1, tile, D), lambda b, t: (b, t, 0)),
        out_shape=jax.ShapeDtypeStruct((B, seq, D), F32),
        compiler_params=_params("arbitrary", "arbitrary"),
        name="final_norm",
    )(h, g.reshape(1, D))


def _pad_heads(w, per_head):
    K = w.shape[0]
    w3 = w.reshape(K, N_HEADS, per_head)
    w3 = jnp.pad(w3, ((0, 0), (0, 0), (0, LANES - per_head)))
    return w3.reshape(K, N_HEADS * LANES)


def _mla_layer(h, g, w_down, q_norm, kv_norm, w_uq, w_ukv, w_o):
    B, Lp, D = h.shape
    M = B * Lp
    n_lat = MLA_Q_RANK + MLA_KV_RANK
    wd_rope = jnp.pad(w_down[:, n_lat:], ((0, 0), (MLA_NOPE, LANES - MLA_NOPE - MLA_ROPE)))
    wd_p = jnp.concatenate([w_down[:, :n_lat], wd_rope], axis=1).astype(BF16)
    wq_p = _pad_heads(w_uq, MLA_NOPE + MLA_ROPE).astype(BF16)
    wkv3 = w_ukv.reshape(MLA_KV_RANK, N_HEADS, MLA_NOPE + MLA_V)
    wk_p = _pad_heads(wkv3[:, :, :MLA_NOPE].reshape(MLA_KV_RANK, -1), MLA_NOPE).astype(BF16)
    wvt_p = wkv3[:, :, MLA_NOPE:].reshape(MLA_KV_RANK, N_HEADS * MLA_V).T.astype(BF16)
    inv = ROPE_THETA ** (-jnp.arange(0, MLA_ROPE, 2, dtype=F32) / MLA_ROPE)
    inv_lane = jnp.concatenate([jnp.zeros((MLA_NOPE,), F32), inv, inv,
                                jnp.zeros((LANES - MLA_NOPE - MLA_ROPE,), F32)]).reshape(1, LANES)
    q, k, vt = _mla_proj(h.reshape(M, D), g, wd_p, q_norm, kv_norm, wq_p, wk_p, wvt_p, inv_lane, Lp)
    o = _softmax_attention(q.reshape(B, Lp, -1), k.reshape(B, Lp, -1), vt,
                           width=2 * LANES, n_pairs=N_HEADS // 2)
    return o.reshape(M, -1), w_o.astype(BF16)


def kernel(x, meta, norm_mix, norm_ffn, pool_w, pool_scale, sb_w_qkv, sb_w_o, mla_w_down, mla_q_norm,
           mla_kv_norm, mla_w_uq, mla_w_ukv, mla_w_o, fox_w_qkvf, fox_b_f, fox_w_o, ffn_w_gate, ffn_w_up,
           ffn_w_down, final_norm):
    B, S, D = x.shape
    assert D == D_MODEL and S % ATT_TILE == 0 and PAD_FRONT % ATT_TILE == 0 and PAD_FRONT % SB_TILE == 0
    Lp = S + PAD_FRONT
    M = B * Lp
    HD = N_HEADS * HEAD_DIM
    n_pairs = N_HEADS // 2

    def sb_mixer(h, i, j):
        qk, vt = _qk_vt_proj(h.reshape(M, D), norm_mix[i], sb_w_qkv[j][:, :2 * HD].astype(BF16),
                             sb_w_qkv[j][:, 2 * HD:].T.astype(BF16), q_scale=HEAD_DIM ** -0.5, q_cols=HD)
        o = _sb_attention(qk.reshape(B, Lp, 2 * HD), vt, n_pairs=n_pairs)
        return o.reshape(M, HD), sb_w_o[j].astype(BF16)

    def mla_mixer(h, i, j):
        return _mla_layer(h, norm_mix[i], mla_w_down[j], mla_q_norm[j], mla_kv_norm[j], mla_w_uq[j],
                          mla_w_ukv[j], mla_w_o[j])

    def fox_mixer(h, i, j):
        wf = jnp.pad(fox_w_qkvf[j][:, 3 * HD:], ((0, 0), (0, LANES - N_HEADS))).astype(BF16)
        qk, vt, flog = _qk_vt_proj(h.reshape(M, D), norm_mix[i], fox_w_qkvf[j][:, :2 * HD].astype(BF16),
                                   fox_w_qkvf[j][:, 2 * HD:3 * HD].T.astype(BF16), wf,
                                   q_scale=HEAD_DIM ** -0.5 * LOG2E, q_cols=HD)
        b_lane = jnp.pad(fox_b_f[j].astype(F32), (0, LANES - N_HEADS)).reshape(1, LANES)
        kf, frow = _forget_cumsum(flog.reshape(B, Lp, LANES), b_lane)
        qk = qk.reshape(B, Lp, 2 * HD)
        o = _softmax_attention(qk, qk, vt, width=LANES, n_pairs=n_pairs, offs=(0, n_pairs), kf=kf, frow=frow)
        return o.reshape(M, HD), fox_w_o[j].astype(BF16)

    mixers = (None, sb_mixer, mla_mixer, fox_mixer)
    h = None
    for i in range(norm_mix.shape[0]):
        m, j = i % len(mixers), i // len(mixers)
        ffn_w = (norm_ffn[i], ffn_w_gate[i].astype(BF16), ffn_w_up[i].astype(BF16), ffn_w_down[i].astype(BF16))
        if m == 0:
            assert i == 0, "the pooling mixer doubles as the layout stage and must come first"
            h = _pool_layer(x, meta, norm_mix[i], pool_w[j], pool_scale[j])
            h = _ffn(h.reshape(M, D), *ffn_w).reshape(B, Lp, D)
        else:
            o2, wo = mixers[m](h, i, j)
            h = _ffn(h.reshape(M, D), *ffn_w, o2=o2, wo=wo).reshape(B, Lp, D)
    return _final_norm(h, final_norm, S)
```
